```python
import math
import jax, jax.numpy as jnp
from jax import lax
import numpy as np

D_MODEL = 1024
BATCH = 8
SEQ = 4096
DEPTH = 2

GRID_W = 64
CTX_LEN = 256
HEAD_DIM = 64
ROPE_BASE = 10000.0
RMS_EPS = 1e-6

GLA_HEADS = 4
GLA_DK = 32
GLA_DV = 64
GLA_LR = 16
GLA_NORMALIZER = 16.0
GLA_CHUNK = 64
DIFF_HEADS = 4
DIFF_HD = 64
DIFF_QB = 128
NA_HEADS = 4
NA_HD = 64
WIN_R = 8
WIN_C = 16

GLA_QK = GLA_HEADS * GLA_DK
GLA_V = GLA_HEADS * GLA_DV
DIFF_QK = DIFF_HEADS * 2 * DIFF_HD
DIFF_V = DIFF_HEADS * 2 * DIFF_HD
NA_W = NA_HEADS * NA_HD
MIX_WIDTH = GLA_V + DIFF_V + NA_W
IN_NAMES = ('gq', 'gk', 'gv', 'gg', 'gdf', 'gdb', 'dq', 'dk', 'dv', 'nq', 'nk', 'nv')
IN_SIZES = (GLA_QK, GLA_QK, GLA_V, GLA_V, GLA_LR, GLA_LR, DIFF_QK, DIFF_QK, DIFF_V, NA_W, NA_W, NA_W)
IN_COLS = sum(IN_SIZES)

FFN_DENSE = 2816
N_EXPERTS = 8
TOP_K = 2
FFN_EXPERT = 3584
N_DENSE = (DEPTH + 1) // 2
N_MOE = DEPTH // 2

kernel_name = 'hymba_gla_diff_natten_moe_prefix_dit'


def rmsnorm(x, g):
    xf = x.astype(jnp.float32)
    y = xf * lax.rsqrt(jnp.mean(xf * xf, axis=-1, keepdims=True) + RMS_EPS)
    return (y * g.astype(jnp.float32)).astype(x.dtype)


def modulate(h, shift, scale):
    return h * (1.0 + scale) + shift


def project(h, w_in):
    z = jnp.einsum('bld,de->ble', h, w_in)
    offs = [int(o) for o in np.cumsum(IN_SIZES)[:-1]]
    return dict(zip(IN_NAMES, jnp.split(z, offs, axis=-1)))


def axial_rope_angles(n_tokens):
    t = jnp.arange(n_tokens)
    row = (t // GRID_W).astype(jnp.float32)
    col = (t % GRID_W).astype(jnp.float32)
    quarter = HEAD_DIM // 4
    inv = 1.0 / (ROPE_BASE ** (jnp.arange(quarter, dtype=jnp.float32) / quarter))
    return row[:, None] * inv, col[:, None] * inv


def rope_rotate(x, ang):
    x1, x2 = jnp.split(x, 2, axis=-1)
    cos, sin = jnp.cos(ang), jnp.sin(ang)
    return jnp.concatenate([x1 * cos - x2 * sin, x1 * sin + x2 * cos], axis=-1)


def apply_axial_rope(x, ang_r, ang_c):
    xf = x.astype(jnp.float32)
    half = HEAD_DIM // 2
    ar = ang_r[:, None, None, :]
    ac = ang_c[:, None, None, :]
    out = jnp.concatenate([rope_rotate(xf[..., :half], ar), rope_rotate(xf[..., half:], ac)], axis=-1)
    return out.astype(x.dtype)


def gla_log_decay(zd, w_up, b):
    logits = jnp.einsum('blr,rk->blk', zd, w_up) + b
    return jax.nn.log_sigmoid(logits.astype(jnp.float32)) / GLA_NORMALIZER


def gla_inputs(p, w_dec_up, b_dec):
    B, L, _ = p['gq'].shape
    heads = lambda t, d: t.reshape(B, L, GLA_HEADS, d)
    q = heads(p['gq'], GLA_DK) * GLA_DK ** -0.5
    k = heads(p['gk'], GLA_DK)
    v = heads(p['gv'], GLA_DV)
    lg_f = heads(gla_log_decay(p['gdf'], w_dec_up[0], b_dec[0]), GLA_DK)
    lg_b = heads(gla_log_decay(p['gdb'], w_dec_up[1], b_dec[1]), GLA_DK)
    return q, k, v, lg_f, lg_b


def gla_chunked(q, k, v, log_g, s0, with_output):
    B, L, H, _ = q.shape
    dv = v.shape[-1]
    n = L // GLA_CHUNK
    blk = lambda t: t.astype(jnp.float32).reshape(B, n, GLA_CHUNK, H, t.shape[-1])
    q, k, v, log_g = blk(q), blk(k), blk(v), blk(log_g)
    b = jnp.cumsum(log_g, axis=2)
    b_end = b[:, :, -1]
    k_end = k * jnp.exp(b_end[:, :, None] - b)
    u = jnp.einsum('bnshk,bnshv->bnhkv', k_end, v)

    def step(s, inp):
        lb, u_c = inp
        return jnp.exp(lb)[..., None] * s + u_c, s

    s_fin, s_prev = lax.scan(step, s0, (jnp.moveaxis(b_end, 1, 0), jnp.moveaxis(u, 1, 0)))
    if not with_output:
        return None, s_fin
    q_dec = q * jnp.exp(b)
    k_inv = k * jnp.exp(-b)
    lower = jnp.tril(jnp.ones((GLA_CHUNK, GLA_CHUNK), dtype=bool))
    att = jnp.where(lower, jnp.einsum('bnchk,bnshk->bnhcs', q_dec, k_inv), 0.0)
    o = (jnp.einsum('bnhcs,bnshv->bnchv', att, v)
         + jnp.einsum('bnchk,nbhkv->bnchv', q_dec, s_prev))
    return o.reshape(B, L, H, dv), s_fin


def gla_bidir(q, k, v, lg_f, lg_b, s0_f, s0_b, with_output):
    flip = lambda t: jnp.flip(t, axis=1)
    o_f, s_f = gla_chunked(q, k, v, lg_f, s0_f, with_output)
    o_b, s_b = gla_chunked(flip(q), flip(k), flip(v), flip(lg_b), s0_b, with_output)
    o = o_f + flip(o_b) if with_output else None
    return o, s_f, s_b


def gla_out(o, gate, g):
    B, L = o.shape[:2]
    o = rmsnorm(o, g).reshape(B, L, GLA_V).astype(gate.dtype)
    return o * jax.nn.silu(gate)


def diff_lambda_value(lam, layer_idx):
    lam_init = 0.8 - 0.6 * math.exp(-0.3 * layer_idx)
    lf = lam.astype(jnp.float32)
    lam_full = jnp.exp(jnp.sum(lf[0] * lf[1])) - jnp.exp(jnp.sum(lf[2] * lf[3])) + lam_init
    return lam_full, lam_init


def diff_attend(q, k, v, lam):
    s = jnp.einsum('bhmqd,bhmkd->bhmqk', q * DIFF_HD ** -0.5, k).astype(jnp.float32)
    p = jax.nn.softmax(s, axis=-1)
    a = p[:, :, 0] - lam * p[:, :, 1]
    return jnp.einsum('bhqk,bhkv->bhqv', a.astype(v.dtype), v)


def diff_blocked(q, k, v, lam):
    B, H, M, L, d = q.shape
    nb = L // DIFF_QB
    qb = jnp.moveaxis(q.reshape(B, H, M, nb, DIFF_QB, d), 3, 0)
    ob = lax.map(lambda qq: diff_attend(qq, k, v, lam), qb)
    return jnp.moveaxis(ob, 0, 2).reshape(B, H, L, v.shape[-1])


def diff_out(o, g, lam_init):
    o = rmsnorm(o, g) * (1.0 - lam_init)
    B, H, L, dv = o.shape
    return o.transpose(0, 2, 1, 3).reshape(B, L, H * dv)


def neighbourhood_attention(q, k, v, k_ctx, v_ctx, rpb):
    B, S, H, d = q.shape
    rows = S // GRID_W
    wr = min(WIN_R, rows)
    grid = lambda t: t.reshape(B, rows, GRID_W, H, d)
    qg, kg, vg = grid(q), grid(k), grid(v)
    cols = jnp.arange(GRID_W)
    cstart = jnp.clip(cols - WIN_C // 2, 0, GRID_W - WIN_C)
    cidx = cstart[:, None] + jnp.arange(WIN_C)[None, :]
    cbias = cidx - cols[:, None] + WIN_C - 1
    nloc = wr * WIN_C
    scale = d ** -0.5

    def row_fn(r):
        rs = jnp.clip(r - WIN_R // 2, 0, rows - wr)
        kw = lax.dynamic_slice_in_dim(kg, rs, wr, axis=1)[:, :, cidx]
        vw = lax.dynamic_slice_in_dim(vg, rs, wr, axis=1)[:, :, cidx]
        qr = lax.dynamic_index_in_dim(qg, r, axis=1, keepdims=False) * scale
        rbias = rs + jnp.arange(wr) - r + WIN_R - 1
        bias = rpb[:, rbias[None, :, None], cbias[:, None, :]]
        s_loc = jnp.einsum('bchd,brcjhd->bhcrj', qr, kw).astype(jnp.float32) + bias.astype(jnp.float32)
        s_ctx = jnp.einsum('bchd,bkhd->bhck', qr, k_ctx).astype(jnp.float32)
        s = jnp.concatenate([s_loc.reshape(B, H, GRID_W, nloc), s_ctx], axis=-1)
        p = jax.nn.softmax(s, axis=-1).astype(v.dtype)
        p_loc = p[..., :nloc].reshape(B, H, GRID_W, wr, WIN_C)
        p_ctx = p[..., nloc:]
        return (jnp.einsum('bhcrj,brcjhd->bchd', p_loc, vw)
                + jnp.einsum('bhck,bkhd->bchd', p_ctx, v_ctx))

    out = lax.map(row_fn, jnp.arange(rows))
    return jnp.moveaxis(out, 0, 1).reshape(B, S, H * d)


def softmax_attend(q, k, v):
    s = jnp.einsum('bqhd,bkhd->bhqk', q * q.shape[-1] ** -0.5, k).astype(jnp.float32)
    p = jax.nn.softmax(s, axis=-1).astype(v.dtype)
    return jnp.einsum('bhqk,bkhd->bqhd', p, v)


def mixer(h_lat, h_ctx, w_in, w_dec_up, b_dec, g_gla, lam, g_diff, rpb, w_out,
          layer_idx, ang_r, ang_c, with_ctx_out):
    B, S, _ = h_lat.shape
    C = h_ctx.shape[1]
    pl, pc = project(h_lat, w_in), project(h_ctx, w_in)

    ql, kl, vl, lfl, lbl = gla_inputs(pl, w_dec_up, b_dec)
    qc, kc, vc, lfc, lbc = gla_inputs(pc, w_dec_up, b_dec)
    zero = jnp.zeros((B, GLA_HEADS, GLA_DK, GLA_DV), jnp.float32)
    oc_gla, s_f, s_b = gla_bidir(qc, kc, vc, lfc, lbc, zero, zero, with_ctx_out)
    ol_gla, _, _ = gla_bidir(ql, kl, vl, lfl, lbl, s_f, s_b, True)
    a_lat = gla_out(ol_gla, pl['gg'], g_gla)

    lam_full, lam_init = diff_lambda_value(lam, layer_idx)

    def diff_qkv(p, L):
        return (p['dq'].reshape(B, L, DIFF_HEADS, 2, DIFF_HD),
                p['dk'].reshape(B, L, DIFF_HEADS, 2, DIFF_HD),
                p['dv'].reshape(B, L, DIFF_HEADS, 2 * DIFF_HD))

    dql, dkl, dvl = diff_qkv(pl, S)
    dql = apply_axial_rope(dql, ang_r, ang_c)
    dkl = apply_axial_rope(dkl, ang_r, ang_c)
    dqc, dkc, dvc = diff_qkv(pc, C)
    maps_first = lambda t: t.transpose(0, 2, 3, 1, 4)
    heads_first = lambda t: t.transpose(0, 2, 1, 3)
    k_all = jnp.concatenate([maps_first(dkl), maps_first(dkc)], axis=3)
    v_all = jnp.concatenate([heads_first(dvl), heads_first(dvc)], axis=2)
    b_lat = diff_out(diff_blocked(maps_first(dql), k_all, v_all, lam_full), g_diff, lam_init)

    na = lambda p, name, L: p[name].reshape(B, L, NA_HEADS, NA_HD)
    nkc, nvc = na(pc, 'nk', C), na(pc, 'nv', C)
    c_lat = neighbourhood_attention(na(pl, 'nq', S), na(pl, 'nk', S), na(pl, 'nv', S), nkc, nvc, rpb)

    y_lat = jnp.concatenate([a_lat, b_lat, c_lat], axis=-1) @ w_out
    if not with_ctx_out:
        return y_lat, None
    a_ctx = gla_out(oc_gla, pc['gg'], g_gla)
    b_ctx = diff_out(diff_attend(maps_first(dqc), maps_first(dkc), heads_first(dvc), lam_full), g_diff, lam_init)
    c_ctx = softmax_attend(na(pc, 'nq', C), nkc, nvc).reshape(B, C, NA_W)
    y_ctx = jnp.concatenate([a_ctx, b_ctx, c_ctx], axis=-1) @ w_out
    return y_lat, y_ctx


def swiglu(h, wg, wu, wd):
    return (jax.nn.silu(h @ wg) * (h @ wu)) @ wd


def moe_swiglu(h, w_router, wg, wu, wd):
    shp = h.shape
    t = h.reshape(-1, shp[-1])
    logits = (t @ w_router).astype(jnp.float32)
    top_v, top_i = lax.top_k(logits, TOP_K)
    top_w = jax.nn.softmax(top_v, axis=-1)
    gates = jnp.sum(jax.nn.one_hot(top_i, N_EXPERTS, dtype=jnp.float32) * top_w[..., None], axis=1)
    y = jnp.zeros_like(t)
    for e in range(N_EXPERTS):
        y = y + gates[:, e:e + 1].astype(t.dtype) * swiglu(t, wg[e], wu[e], wd[e])
    return y.reshape(shp)


def setup_inputs(seed: int = 0) -> dict:
    key = jax.random.key(seed)
    ks = jax.random.split(key, 24)
    f32 = jnp.float32
    nrm = lambda k, shape, s: jax.random.normal(k, shape, f32) * s
    D = D_MODEL
    return {
        'x': nrm(ks[0], (BATCH, SEQ, D), 1.0),
        'c': nrm(ks[1], (BATCH, D), 1.0),
        'ctx': nrm(ks[2], (BATCH, CTX_LEN, D), 1.0),
        'c_ctx': nrm(ks[3], (D,), 1.0),
        'w_mod': nrm(ks[4], (DEPTH, D, 6 * D), 0.5 * D ** -0.5),
        'b_mod': nrm(ks[5], (DEPTH, 6 * D), 0.01),
        'g_norm1': 1.0 + nrm(ks[6], (DEPTH, D), 0.1),
        'g_norm2': 1.0 + nrm(ks[7], (DEPTH, D), 0.1),
        'w_in': nrm(ks[8], (DEPTH, D, IN_COLS), D ** -0.5),
        'gla_w_dec_up': nrm(ks[9], (DEPTH, 2, GLA_LR, GLA_QK), GLA_LR ** -0.5),
        'gla_b_dec': nrm(ks[10], (DEPTH, 2, GLA_QK), 0.1),
        'gla_g_norm': 1.0 + nrm(ks[11], (DEPTH, GLA_DV), 0.1),
        'diff_lambda': nrm(ks[12], (DEPTH, 4, DIFF_HD), 0.1),
        'diff_g_norm': 1.0 + nrm(ks[13], (DEPTH, 2 * DIFF_HD), 0.1),
        'na_rpb': nrm(ks[14], (DEPTH, NA_HEADS, 2 * WIN_R - 1, 2 * WIN_C - 1), 0.1),
        'w_out': nrm(ks[15], (DEPTH, MIX_WIDTH, D), MIX_WIDTH ** -0.5),
        'w_ffn_gate': nrm(ks[16], (N_DENSE, D, FFN_DENSE), D ** -0.5),
        'w_ffn_up': nrm(ks[17], (N_DENSE, D, FFN_DENSE), D ** -0.5),
        'w_ffn_down': nrm(ks[18], (N_DENSE, FFN_DENSE, D), FFN_DENSE ** -0.5),
        'w_router': nrm(ks[19], (N_MOE, D, N_EXPERTS), D ** -0.5),
        'w_moe_gate': nrm(ks[20], (N_MOE, N_EXPERTS, D, FFN_EXPERT), D ** -0.5),
        'w_moe_up': nrm(ks[21], (N_MOE, N_EXPERTS, D, FFN_EXPERT), D ** -0.5),
        'w_moe_down': nrm(ks[22], (N_MOE, N_EXPERTS, FFN_EXPERT, D), FFN_EXPERT ** -0.5),
        'g_final': 1.0 + nrm(ks[23], (D,), 0.1),
    }


def reference(x, c, ctx, c_ctx, w_mod, b_mod, g_norm1, g_norm2, w_in, gla_w_dec_up, gla_b_dec,
              gla_g_norm, diff_lambda, diff_g_norm, na_rpb, w_out, w_ffn_gate, w_ffn_up, w_ffn_down,
              w_router, w_moe_gate, w_moe_up, w_moe_down, g_final):
    S = x.shape[1]
    ang_r, ang_c = axial_rope_angles(S)
    ctx_s = ctx
    for l in range(DEPTH):
        last = l == DEPTH - 1
        sh1, sc1, ga1, sh2, sc2, ga2 = jnp.split(jax.nn.silu(c) @ w_mod[l] + b_mod[l], 6, axis=-1)
        csh1, csc1, cga1, csh2, csc2, cga2 = jnp.split(jax.nn.silu(c_ctx) @ w_mod[l] + b_mod[l], 6, axis=-1)

        h_lat = modulate(rmsnorm(x, g_norm1[l]), sh1[:, None], sc1[:, None])
        h_ctx = modulate(rmsnorm(ctx_s, g_norm1[l]), csh1, csc1)
        y_lat, y_ctx = mixer(h_lat, h_ctx, w_in[l], gla_w_dec_up[l], gla_b_dec[l], gla_g_norm[l],
                             diff_lambda[l], diff_g_norm[l], na_rpb[l], w_out[l],
                             l, ang_r, ang_c, not last)
        x = x + ga1[:, None] * y_lat
        if not last:
            ctx_s = ctx_s + cga1 * y_ctx

        if l % 2 == 0:
            i = l // 2
            ffn = lambda h, i=i: swiglu(h, w_ffn_gate[i], w_ffn_up[i], w_ffn_down[i])
        else:
            i = l // 2
            ffn = lambda h, i=i: moe_swiglu(h, w_router[i], w_moe_gate[i], w_moe_up[i], w_moe_down[i])
        h_lat = modulate(rmsnorm(x, g_norm2[l]), sh2[:, None], sc2[:, None])
        x = x + ga2[:, None] * ffn(h_lat)
        if not last:
            h_ctx = modulate(rmsnorm(ctx_s, g_norm2[l]), csh2, csc2)
            ctx_s = ctx_s + cga2 * ffn(h_ctx)
    return rmsnorm(x, g_final)
```

```python
import functools
import math

import numpy as np
import jax
import jax.numpy as jnp
from jax import lax
from jax.experimental import pallas as pl
from jax.experimental.pallas import tpu as pltpu

F32 = jnp.float32
BF16 = jnp.bfloat16
HIGHEST = lax.Precision.HIGHEST

GRID_W = 64
HEAD_DIM = 64
ROPE_BASE = 10000.0
RMS_EPS = 1e-6
GLA_HEADS = 4
GLA_DK = 32
GLA_DV = 64
GLA_LR = 16
GLA_NORMALIZER = 16.0
GLA_CHUNK = 64
DIFF_HEADS = 4
DIFF_HD = 64
NA_HEADS = 4
NA_HD = 64
WIN_R = 8
WIN_C = 16
N_EXPERTS = 8
GLA_QK = GLA_HEADS * GLA_DK
GLA_V = GLA_HEADS * GLA_DV
DIFF_W = DIFF_HEADS * 2 * DIFF_HD
NA_W = NA_HEADS * NA_HD
IN_SIZES = (GLA_QK, GLA_QK, GLA_V, GLA_V, GLA_LR, GLA_LR, DIFF_W, DIFF_W, DIFF_W, NA_W, NA_W, NA_W)

LANES = 128
TOK_TILE = 512
ATT_TILE = 256
DIFF_KT = 512
NA_QROWS = 4
NA_KROWS = 12
FFN_CHUNK = 256
MOE_TILE = 512
MOE_FCHUNK = 512
MOD_ROWS = 16
NEG_BIG = -1e30
VMEM_LIMIT = 56 * 1024 * 1024

C_GLA = 0
C_DQ = C_GLA + 2 * GLA_QK + 2 * GLA_V
C_DK = C_DQ + DIFF_W
C_DV = C_DK + DIFF_W
C_NQ = C_DV + DIFF_W
C_NK = C_NQ + NA_W
C_NV = C_NK + NA_W
C_DEC = C_NV + NA_W
C_END = C_DEC + LANES


def _silu(x):
    return x * (1.0 / (1.0 + jnp.exp(-x)))


def _dot(a, b):
    return jnp.dot(a, b, preferred_element_type=F32)


def _dot_nt(a, b):
    return lax.dot_general(a, b, (((1,), (1,)), ((), ())), preferred_element_type=F32)


def _dot_tn(a, b):
    return lax.dot_general(a, b, (((0,), (0,)), ((), ())), preferred_element_type=F32)


def _params(sem, vmem=VMEM_LIMIT):
    return pltpu.CompilerParams(dimension_semantics=sem, vmem_limit_bytes=vmem)


def _mod_kernel(cc_ref, w_ref, b_ref, o_ref):
    s = _silu(cc_ref[...]).astype(BF16)
    o_ref[0] = _dot(s, w_ref[0].astype(BF16)) + b_ref[0]


def _modulation(cc, w_mod, b_mod):
    depth, d, six_d = w_mod.shape
    n = six_d // d
    return pl.pallas_call(
        _mod_kernel,
        grid=(depth, n),
        in_specs=[pl.BlockSpec((MOD_ROWS, d), lambda l, j: (0, 0)),
                  pl.BlockSpec((1, d, d), lambda l, j: (l, 0, j)),
                  pl.BlockSpec((1, 1, d), lambda l, j: (l, 0, j))],
        out_specs=pl.BlockSpec((1, MOD_ROWS, d), lambda l, j: (l, 0, j)),
        out_shape=jax.ShapeDtypeStruct((depth, MOD_ROWS, six_d), F32),
        compiler_params=_params(("parallel", "parallel")),
        name="modulation",
    )(cc, w_mod, b_mod.reshape(depth, 1, six_d))


def _rope(z, a, bt, ct):
    outs = []
    for s in range(z.shape[1] // LANES):
        zs = z[:, s * LANES:(s + 1) * LANES]
        outs.append(zs * a + pltpu.roll(zs, LANES - 16, 1) * bt + pltpu.roll(zs, 16, 1) * ct)
    return jnp.concatenate(outs, axis=1)


def _inproj_kernel(x_ref, mod_ref, g_ref, w_ref, wup_ref, bdec_ref, ra_ref, rb_ref, rc_ref,
                   gla_ref, lg_ref, dq_ref, dk_ref, dv_ref, nq_ref, nk_ref, nv_ref):
    x = x_ref[...]
    m = mod_ref[0, 0]
    h = x * lax.rsqrt(jnp.mean(x * x, axis=-1, keepdims=True) + RMS_EPS) * g_ref[...]
    hb = (h * (1.0 + m[1:2]) + m[0:1]).astype(BF16)

    def proj(c0, c1):
        return _dot(hb, w_ref[:, c0:c1])

    gla_ref[...] = proj(C_GLA, C_DQ).astype(BF16)
    a, bt, ct = ra_ref[...], rb_ref[...], rc_ref[...]
    dq_ref[...] = (_rope(proj(C_DQ, C_DK), a, bt, ct) * DIFF_HD ** -0.5).astype(BF16)
    dk_ref[...] = _rope(proj(C_DK, C_DV), a, bt, ct).astype(BF16)
    dv_ref[...] = proj(C_DV, C_NQ).astype(BF16)
    nq_ref[...] = (proj(C_NQ, C_NK) * NA_HD ** -0.5).astype(BF16)
    nk_ref[...] = proj(C_NK, C_NV).astype(BF16)
    nv_ref[...] = proj(C_NV, C_DEC).astype(BF16)
    logits = _dot(proj(C_DEC, C_END).astype(BF16), wup_ref[...]) + bdec_ref[...]
    log_sig = jnp.minimum(logits, 0.0) - jnp.log1p(jnp.exp(-jnp.abs(logits)))
    lg_ref[...] = log_sig * (1.0 / GLA_NORMALIZER)


def _input_projection(x_all, mod, layer, g1, w_cat, wup2, bdec2, rope_tabs, dims):
    n_tot, d = x_all.shape
    n_lat_tiles = dims["n_lat"] // TOK_TILE
    tiles_per_batch = dims["S"] // TOK_TILE
    n_tiles = n_tot // TOK_TILE
    nb = dims["B"]

    def mod_map(i):
        return (layer, jnp.where(i < n_lat_tiles, i // tiles_per_batch, nb), 0, 0)

    def rope_map(i):
        return (jnp.where(i < n_lat_tiles, i % tiles_per_batch, tiles_per_batch), 0)

    row = lambda w: pl.BlockSpec((TOK_TILE, w), lambda i: (i, 0))
    const = lambda shape: pl.BlockSpec(shape, lambda i: (0,) * len(shape))
    tab = pl.BlockSpec((TOK_TILE, LANES), rope_map)
    widths = (C_DQ, 2 * GLA_QK, DIFF_W, DIFF_W, DIFF_W, NA_W, NA_W, NA_W)
    dtypes = (BF16, F32, BF16, BF16, BF16, BF16, BF16, BF16)
    return pl.pallas_call(
        _inproj_kernel,
        grid=(n_tiles,),
        in_specs=[row(d), pl.BlockSpec((1, 1, 6, d), mod_map), const((1, d)), const((d, C_END)),
                  const((LANES, 2 * GLA_QK)), const((1, 2 * GLA_QK)), tab, tab, tab],
        out_specs=[row(w) for w in widths],
        out_shape=[jax.ShapeDtypeStruct((n_tot, w), dt) for w, dt in zip(widths, dtypes)],
        compiler_params=_params(("parallel",)),
        name="input_projection",
    )(x_all, mod, g1, w_cat, wup2, bdec2, *rope_tabs)


def _gla_chunk(g_ref, l_ref, o_ref, s_ref, c, fwd, consts):
    tri, tri4, head_k, head_v, bd = consts
    rows = slice(c * GLA_CHUNK, (c + 1) * GLA_CHUNK)
    q = g_ref[rows, 0:GLA_QK].astype(F32) * GLA_DK ** -0.5
    k = g_ref[rows, GLA_QK:2 * GLA_QK].astype(F32)
    v = g_ref[rows, 2 * GLA_QK:2 * GLA_QK + GLA_V]
    lcol = 0 if fwd else GLA_QK
    lg = l_ref[rows, lcol:lcol + GLA_QK]
    b = jnp.dot(tri, lg, precision=HIGHEST, preferred_element_type=F32)
    b_end = b[GLA_CHUNK - 1:GLA_CHUNK] if fwd else b[0:1]
    q_dec = q * jnp.exp(b)
    k_inv = (k * jnp.exp(-b)).astype(BF16)
    k_end = (k * jnp.exp(b_end - b)).astype(BF16)
    qs = jnp.concatenate([jnp.where(head_k == h, q_dec, 0.0) for h in range(GLA_HEADS)], axis=0)
    att = jnp.where(tri4, _dot_nt(qs.astype(BF16), k_inv), 0.0)
    ov = _dot(att.astype(BF16), v)
    o = _dot(q_dec.astype(BF16), s_ref[...].astype(BF16))
    for h in range(GLA_HEADS):
        o = o + jnp.where(head_v == h, ov[h * GLA_CHUNK:(h + 1) * GLA_CHUNK], 0.0)
    o_ref[rows, :] = o
    total = lax.dot_general(lg, jnp.ones((GLA_CHUNK, GLA_V), F32), (((0,), (0,)), ((), ())),
                            precision=HIGHEST, preferred_element_type=F32)
    s_ref[...] = jnp.where(bd, jnp.exp(total) * s_ref[...] + _dot_tn(k_end, v), 0.0)


def _gla_kernel(gf_ref, lf_ref, gb_ref, lb_ref, of_ref, ob_ref, sf_ref, sb_ref):
    @pl.when(pl.program_id(1) == 0)
    def _():
        sf_ref[...] = jnp.zeros_like(sf_ref)
        sb_ref[...] = jnp.zeros_like(sb_ref)

    r = lax.broadcasted_iota(jnp.int32, (GLA_CHUNK, GLA_CHUNK), 0)
    c = lax.broadcasted_iota(jnp.int32, (GLA_CHUNK, GLA_CHUNK), 1)
    r4 = lax.broadcasted_iota(jnp.int32, (GLA_HEADS * GLA_CHUNK, GLA_CHUNK), 0) % GLA_CHUNK
    c4 = lax.broadcasted_iota(jnp.int32, (GLA_HEADS * GLA_CHUNK, GLA_CHUNK), 1)
    head_k = lax.broadcasted_iota(jnp.int32, (1, GLA_QK), 1) // GLA_DK
    head_v = lax.broadcasted_iota(jnp.int32, (1, GLA_V), 1) // GLA_DV
    bd = (lax.broadcasted_iota(jnp.int32, (GLA_QK, GLA_V), 0) // GLA_DK
          == lax.broadcasted_iota(jnp.int32, (GLA_QK, GLA_V), 1) // GLA_DV)
    lower = ((c <= r).astype(F32), c4 <= r4, head_k, head_v, bd)
    upper = ((c >= r).astype(F32), c4 >= r4, head_k, head_v, bd)
    n_chunks = ATT_TILE // GLA_CHUNK
    for j in range(n_chunks):
        _gla_chunk(gf_ref, lf_ref, of_ref, sf_ref, j, True, lower)
        _gla_chunk(gb_ref, lb_ref, ob_ref, sb_ref, n_chunks - 1 - j, False, upper)


def _gla(gla, lg, dims):
    n_tot = gla.shape[0]
    nb, steps = dims["B"], 1 + dims["S"] // ATT_TILE
    lat_blocks = dims["n_lat"] // ATT_TILE
    per_batch = dims["S"] // ATT_TILE

    def fwd_map(b, i):
        return (jnp.where(i == 0, lat_blocks + b, b * per_batch + i - 1), 0)

    def bwd_map(b, i):
        return (jnp.where(i == 0, lat_blocks + b, b * per_batch + per_batch - i), 0)

    return pl.pallas_call(
        _gla_kernel,
        grid=(nb, steps),
        in_specs=[pl.BlockSpec((ATT_TILE, C_DQ), fwd_map), pl.BlockSpec((ATT_TILE, 2 * GLA_QK), fwd_map),
                  pl.BlockSpec((ATT_TILE, C_DQ), bwd_map), pl.BlockSpec((ATT_TILE, 2 * GLA_QK), bwd_map)],
        out_specs=[pl.BlockSpec((ATT_TILE, GLA_V), fwd_map), pl.BlockSpec((ATT_TILE, GLA_V), bwd_map)],
        out_shape=[jax.ShapeDtypeStruct((n_tot, GLA_V), F32)] * 2,
        scratch_shapes=[pltpu.VMEM((GLA_QK, GLA_V), F32), pltpu.VMEM((GLA_QK, GLA_V), F32)],
        compiler_params=_params(("parallel", "arbitrary")),
        name="gla_scan",
    )(gla, lg, gla, lg)


def _diff_kernel(lam_init, n_lat_q, q_ref, kl_ref, vl_ref, kc_ref, vc_ref, lam_ref, g_ref, o_ref,
                 m_ref, l_ref, acc_ref):
    q = q_ref[...]
    lane = lax.broadcasted_iota(jnp.int32, q.shape, 1)
    zero = jnp.zeros_like(q)
    qs = jnp.concatenate([jnp.where(lane < DIFF_HD, q, zero), jnp.where(lane >= DIFF_HD, q, zero)], axis=0)
    m_ref[...] = jnp.full_like(m_ref, -jnp.inf)
    l_ref[...] = jnp.zeros_like(l_ref)
    acc_ref[...] = jnp.zeros_like(acc_ref)

    def step(k, v):
        s = _dot_nt(qs, k)
        m_old = m_ref[...]
        m_new = jnp.maximum(m_old, jnp.max(s, axis=-1, keepdims=True))
        alpha = jnp.exp(m_old - m_new)
        p = jnp.exp(s - m_new)
        l_ref[...] = alpha * l_ref[...] + jnp.sum(p, axis=-1, keepdims=True)
        acc_ref[...] = alpha * acc_ref[...] + _dot(p.astype(BF16), v)
        m_ref[...] = m_new

    step(kc_ref[...], vc_ref[...])

    @pl.when(pl.program_id(2) < n_lat_q)
    def _():
        def body(j, carry):
            rows = pl.ds(pl.multiple_of(j * DIFF_KT, DIFF_KT), DIFF_KT)
            step(kl_ref[rows, :], vl_ref[rows, :])
            return carry
        lax.fori_loop(0, kl_ref.shape[0] // DIFF_KT, body, 0)

    lam = lam_ref[...]
    lam_full = (jnp.exp(jnp.sum(lam[0:1] * lam[1:2], axis=-1, keepdims=True))
                - jnp.exp(jnp.sum(lam[2:3] * lam[3:4], axis=-1, keepdims=True)) + lam_init)
    t = q.shape[0]
    on = acc_ref[...] / l_ref[...]
    o = on[:t] - lam_full * on[t:]
    o = o * lax.rsqrt(jnp.mean(o * o, axis=-1, keepdims=True) + RMS_EPS) * g_ref[...]
    o_ref[...] = (o * (1.0 - lam_init)).astype(BF16)


def _diff_attention(dq, dk, dv, lam, g_diff, layer, with_ctx, dims):
    n_tot = dq.shape[0]
    nb, s_len = dims["B"], dims["S"]
    n_lat_q = s_len // ATT_TILE
    lat_blocks = dims["n_lat"] // ATT_TILE
    lam_init = 0.8 - 0.6 * math.exp(-0.3 * layer)
    hw = 2 * DIFF_HD

    def q_map(b, h, i):
        return (jnp.where(i < n_lat_q, b * n_lat_q + i, lat_blocks + b), h)

    lat = pl.BlockSpec((s_len, hw), lambda b, h, i: (b, h))
    ctx = pl.BlockSpec((ATT_TILE, hw), lambda b, h, i: (lat_blocks + b, h))
    return pl.pallas_call(
        functools.partial(_diff_kernel, lam_init, n_lat_q),
        grid=(nb, DIFF_HEADS, n_lat_q + (1 if with_ctx else 0)),
        in_specs=[pl.BlockSpec((ATT_TILE, hw), q_map), lat, lat, ctx, ctx,
                  pl.BlockSpec((4, DIFF_HD), lambda b, h, i: (0, 0)),
                  pl.BlockSpec((1, hw), lambda b, h, i: (0, 0))],
        out_specs=pl.BlockSpec((ATT_TILE, hw), q_map),
        out_shape=jax.ShapeDtypeStruct((n_tot if with_ctx else dims["n_lat"], DIFF_W), BF16),
        scratch_shapes=[pltpu.VMEM((2 * ATT_TILE, 1), F32), pltpu.VMEM((2 * ATT_TILE, 1), F32),
                        pltpu.VMEM((2 * ATT_TILE, hw), F32)],
        compiler_params=_params(("parallel", "parallel", "arbitrary")),
        name="diff_attention",
    )(dq, dk, dv, dk, dv, lam, g_diff)


def _na_bias_tables(rpb, rows):
    wr = min(WIN_R, rows)
    n_groups = rows // NA_QROWS
    tabs = []
    for g in (0, 1, n_groups - 1):
        r0 = g * NA_QROWS
        u0 = int(np.clip(r0 - WIN_R // 2, 0, rows - NA_KROWS))
        r = r0 + np.arange(NA_QROWS)[:, None, None, None]
        qc = np.arange(GRID_W)[None, :, None, None]
        kr = u0 + np.arange(NA_KROWS)[None, None, :, None]
        kc = np.arange(GRID_W)[None, None, None, :]
        rs = np.clip(r - WIN_R // 2, 0, rows - wr)
        cs = np.clip(qc - WIN_C // 2, 0, GRID_W - WIN_C)
        valid = (kr >= rs) & (kr < rs + wr) & (kc >= cs) & (kc < cs + WIN_C)
        rb = np.clip(kr - r + WIN_R - 1, 0, 2 * WIN_R - 2)
        cb = np.clip(kc - qc + WIN_C - 1, 0, 2 * WIN_C - 2)
        shape = (NA_QROWS, GRID_W, NA_KROWS, GRID_W)
        rb, cb, valid = (np.broadcast_to(t, shape).reshape(NA_QROWS * GRID_W, NA_KROWS * GRID_W)
                         for t in (rb, cb, valid))
        tabs.append(jnp.where(valid[None], rpb[:, rb, cb].astype(F32), NEG_BIG))
    return jnp.stack(tabs)


def _na_kernel(rows, q_ref, kl_ref, vl_ref, kc_ref, vc_ref, bias_ref, o_ref):
    g = pl.program_id(1)
    n_groups = rows // NA_QROWS
    q = q_ref[...]
    t = q.shape[0]
    head = lax.broadcasted_iota(jnp.int32, q.shape, 1) // NA_HD
    zero = jnp.zeros_like(q)
    qs = jnp.concatenate([jnp.where(head == h, q, zero) for h in range(NA_HEADS)], axis=0)
    kc, vc = kc_ref[...], vc_ref[...]
    s_ctx = _dot_nt(qs, kc)
    m_ctx = jnp.max(s_ctx, axis=-1, keepdims=True)

    def finish(o_stack, l):
        o_stack = o_stack / l
        o = jnp.zeros((t, q.shape[1]), F32)
        for h in range(NA_HEADS):
            o = o + jnp.where(head == h, o_stack[h * t:(h + 1) * t], 0.0)
        o_ref[...] = o.astype(BF16)

    @pl.when(g < n_groups)
    def _():
        u0 = jnp.clip(g * NA_QROWS - WIN_R // 2, 0, rows - NA_KROWS)
        win = pl.ds(pl.multiple_of(u0 * GRID_W, GRID_W), NA_KROWS * GRID_W)
        kw, vw = kl_ref[win, :], vl_ref[win, :]
        s_loc = _dot_nt(qs, kw) + bias_ref[0].reshape(NA_HEADS * t, NA_KROWS * GRID_W)
        m = jnp.maximum(m_ctx, jnp.max(s_loc, axis=-1, keepdims=True))
        p_loc = jnp.exp(s_loc - m)
        p_ctx = jnp.exp(s_ctx - m)
        l = jnp.sum(p_loc, axis=-1, keepdims=True) + jnp.sum(p_ctx, axis=-1, keepdims=True)
        finish(_dot(p_loc.astype(BF16), vw) + _dot(p_ctx.astype(BF16), vc), l)

    @pl.when(g >= n_groups)
    def _():
        p_ctx = jnp.exp(s_ctx - m_ctx)
        finish(_dot(p_ctx.astype(BF16), vc), jnp.sum(p_ctx, axis=-1, keepdims=True))


def _neighbourhood_attention(nq, nk, nv, bias_tabs, with_ctx, dims):
    n_tot = nq.shape[0]
    nb, s_len = dims["B"], dims["S"]
    rows = s_len // GRID_W
    n_groups = rows // NA_QROWS
    lat_blocks = dims["n_lat"] // ATT_TILE

    def q_map(b, g):
        return (jnp.where(g < n_groups, b * n_groups + g, lat_blocks + b), 0)

    def bias_map(b, g):
        return (jnp.where(g == 0, 0, jnp.where(g >= n_groups - 1, 2, 1)), 0, 0, 0)

    lat = pl.BlockSpec((s_len, NA_W), lambda b, g: (b, 0))
    ctx = pl.BlockSpec((ATT_TILE, NA_W), lambda b, g: (lat_blocks + b, 0))
    return pl.pallas_call(
        functools.partial(_na_kernel, rows),
        grid=(nb, n_groups + (1 if with_ctx else 0)),
        in_specs=[pl.BlockSpec((ATT_TILE, NA_W), q_map), lat, lat, ctx, ctx,
                  pl.BlockSpec((1,) + bias_tabs.shape[1:], bias_map)],
        out_specs=pl.BlockSpec((ATT_TILE, NA_W), q_map),
        out_shape=jax.ShapeDtypeStruct((n_tot if with_ctx else dims["n_lat"], NA_W), BF16),
        compiler_params=_params(("parallel", "arbitrary")),
        name="neighbourhood_attention",
    )(nq, nk, nv, nk, nv, bias_tabs)


def _route_top2(logits):
    lane = lax.broadcasted_iota(jnp.int32, logits.shape, 1)
    lane_f = lane.astype(F32)
    lg = jnp.where(lane < N_EXPERTS, logits, -jnp.inf)
    m1 = jnp.max(lg, axis=-1, keepdims=True)
    i1 = jnp.min(jnp.where(lg == m1, lane_f, float(LANES)), axis=-1, keepdims=True)
    lg2 = jnp.where(lane_f == i1, -jnp.inf, lg)
    m2 = jnp.max(lg2, axis=-1, keepdims=True)
    i2 = jnp.min(jnp.where(lg2 == m2, lane_f, float(LANES)), axis=-1, keepdims=True)
    e = jnp.exp(m2 - m1)
    w1 = 1.0 / (1.0 + e)
    w2 = e / (1.0 + e)
    return jnp.where(lane == 0, i1, jnp.where(lane == 1, i2, jnp.where(lane == 2, w1,
                                                                         jnp.where(lane == 3, w2, 0.0))))


def _outproj_kernel(moe, x_ref, of_ref, ob_ref, gate_ref, bd_ref, cn_ref, mod_ref, ggla_ref, g2_ref, w_ref,
                    *rest):
    if moe:
        wr_ref, x_out_ref, h_ref, route_ref = rest
    else:
        x_out_ref, h_ref = rest
    m = mod_ref[0, 0]
    o = of_ref[...] + ob_ref[...]
    avg = (lax.broadcasted_iota(jnp.int32, (GLA_V, GLA_V), 0) // GLA_DV
           == lax.broadcasted_iota(jnp.int32, (GLA_V, GLA_V), 1) // GLA_DV).astype(F32) * (1.0 / GLA_DV)
    ms = jnp.dot(o * o, avg, precision=HIGHEST, preferred_element_type=F32)
    a = o * lax.rsqrt(ms + RMS_EPS) * ggla_ref[...] * _silu(gate_ref[...].astype(F32))
    y = (_dot(a.astype(BF16), w_ref[0:GLA_V]) + _dot(bd_ref[...], w_ref[GLA_V:GLA_V + DIFF_W])
         + _dot(cn_ref[...], w_ref[GLA_V + DIFF_W:]))
    x = x_ref[...] + m[2:3] * y
    x_out_ref[...] = x
    h = x * lax.rsqrt(jnp.mean(x * x, axis=-1, keepdims=True) + RMS_EPS) * g2_ref[...]
    h = h * (1.0 + m[4:5]) + m[3:4]
    h_ref[...] = h.astype(h_ref.dtype)
    if moe:
        route_ref[...] = _route_top2(jnp.dot(h, wr_ref[...], precision=HIGHEST, preferred_element_type=F32))


def _output_projection(x_all, o_f, o_b, gla, b_diff, c_na, mod, layer, g_gla, g2, w_out, w_router, n_rows, dims):
    d = x_all.shape[1]
    n_lat_tiles = dims["n_lat"] // TOK_TILE
    tiles_per_batch = dims["S"] // TOK_TILE
    nb = dims["B"]
    moe = w_router is not None

    def mod_map(i):
        return (layer, jnp.where(i < n_lat_tiles, i // tiles_per_batch, nb), 0, 0)

    row = lambda w, cb=0: pl.BlockSpec((TOK_TILE, w), lambda i: (i, cb))
    const = lambda shape: pl.BlockSpec(shape, lambda i: (0,) * len(shape))
    in_specs = [row(d), row(GLA_V), row(GLA_V), row(GLA_V, (2 * GLA_QK + GLA_V) // GLA_V), row(DIFF_W), row(NA_W),
                pl.BlockSpec((1, 1, 6, d), mod_map), const((1, GLA_V)), const((1, d)), const((d, d))]
    args = [x_all, o_f, o_b, gla, b_diff, c_na, mod, g_gla, g2, w_out]
    out_specs = [row(d), row(d)]
    out_shape = [jax.ShapeDtypeStruct((n_rows, d), F32), jax.ShapeDtypeStruct((n_rows, d), F32 if moe else BF16)]
    if moe:
        in_specs.append(const((d, LANES)))
        args.append(w_router)
        out_specs.append(row(LANES))
        out_shape.append(jax.ShapeDtypeStruct((n_rows, LANES), F32))
    return pl.pallas_call(
        functools.partial(_outproj_kernel, moe),
        grid=(n_rows // TOK_TILE,),
        in_specs=in_specs, out_specs=out_specs, out_shape=out_shape,
        compiler_params=_params(("parallel",)),
        name="output_projection",
    )(*args)


def _ffn_kernel(h_ref, x_ref, mod_ref, wg_ref, wu_ref, wd_ref, o_ref, acc_ref):
    h = h_ref[...]
    n_chunks = wg_ref.shape[1] // FFN_CHUNK
    for j in range(n_chunks):
        cols = slice(j * FFN_CHUNK, (j + 1) * FFN_CHUNK)
        act = (_silu(_dot(h, wg_ref[:, cols])) * _dot(h, wu_ref[:, cols])).astype(BF16)
        part = _dot(act, wd_ref[cols, :])
        if j == 0:
            acc_ref[...] = part
        else:
            acc_ref[...] += part
    o_ref[...] = x_ref[...] + mod_ref[0, 0][5:6] * acc_ref[...]


def _dense_ffn(h, x_all, mod, layer, wg, wu, wd, dims):
    n_rows, d = x_all.shape
    f = wg.shape[1]
    n_lat_tiles = dims["n_lat"] // TOK_TILE
    tiles_per_batch = dims["S"] // TOK_TILE
    nb = dims["B"]

    def mod_map(i):
        return (layer, jnp.where(i < n_lat_tiles, i // tiles_per_batch, nb), 0, 0)

    row = lambda: pl.BlockSpec((TOK_TILE, d), lambda i: (i, 0))
    const = lambda shape: pl.BlockSpec(shape, lambda i: (0, 0))
    return pl.pallas_call(
        _ffn_kernel,
        grid=(n_rows // TOK_TILE,),
        in_specs=[row(), row(), pl.BlockSpec((1, 1, 6, d), mod_map), const((d, f)), const((d, f)), const((f, d))],
        out_specs=row(),
        out_shape=jax.ShapeDtypeStruct((n_rows, d), F32),
        scratch_shapes=[pltpu.VMEM((TOK_TILE, d), F32)],
        compiler_params=_params(("parallel",)),
        name="dense_swiglu",
    )(h, x_all, mod, wg, wu, wd)


def _moe_kernel(te_ref, nact_ref, tok_ref, h_hbm, gate_ref, wg_ref, wu_ref, wd_ref, o_ref,
                xf_ref, xb_ref, acc_ref, sem):
    i, f = pl.program_id(0), pl.program_id(1)
    active = i < nact_ref[0]
    tile = xf_ref.shape[0]

    @pl.when(jnp.logical_and(active, f == 0))
    def _():
        def issue(r, carry):
            pltpu.make_async_copy(h_hbm.at[pl.ds(tok_ref[0, 0, r], 1)], xf_ref.at[pl.ds(r, 1)], sem).start()
            return carry
        lax.fori_loop(0, tile, issue, 0)
        pltpu.make_async_copy(h_hbm.at[pl.ds(0, tile)], xf_ref, sem).wait()
        xb_ref[...] = xf_ref[...].astype(BF16)

    @pl.when(active)
    def _():
        x = xb_ref[...]
        act = (_silu(_dot(x, wg_ref[0])) * _dot(x, wu_ref[0])).astype(BF16)
        part = _dot(act, wd_ref[0])

        @pl.when(f == 0)
        def _():
            acc_ref[...] = part

        @pl.when(f > 0)
        def _():
            acc_ref[...] += part

    last = f == pl.num_programs(1) - 1

    @pl.when(jnp.logical_and(active, last))
    def _():
        o_ref[...] = acc_ref[...] * gate_ref[...]

    @pl.when(jnp.logical_and(jnp.logical_not(active), last))
    def _():
        o_ref[...] = jnp.zeros_like(o_ref)


def _moe_ffn(h, row_token, row_gate, tile_expert, n_active, wg, wu, wd):
    n_tok, d = h.shape
    n_exp, _, ffn = wg.shape
    n_rows = row_token.shape[0]
    n_tiles = n_rows // MOE_TILE
    n_f = ffn // MOE_FCHUNK

    def fcol(i, f, te, na):
        return jnp.where(i < na[0], f, n_f - 1)

    grid_spec = pltpu.PrefetchScalarGridSpec(
        num_scalar_prefetch=2,
        grid=(n_tiles, n_f),
        in_specs=[pl.BlockSpec((1, 1, MOE_TILE), lambda i, f, te, na: (i, 0, 0), memory_space=pltpu.SMEM),
                  pl.BlockSpec(memory_space=pl.ANY),
                  pl.BlockSpec((MOE_TILE, 1), lambda i, f, te, na: (i, 0)),
                  pl.BlockSpec((1, d, MOE_FCHUNK), lambda i, f, te, na: (te[i], 0, fcol(i, f, te, na))),
                  pl.BlockSpec((1, d, MOE_FCHUNK), lambda i, f, te, na: (te[i], 0, fcol(i, f, te, na))),
                  pl.BlockSpec((1, MOE_FCHUNK, d), lambda i, f, te, na: (te[i], fcol(i, f, te, na), 0))],
        out_specs=pl.BlockSpec((MOE_TILE, d), lambda i, f, te, na: (i, 0)),
        scratch_shapes=[pltpu.VMEM((MOE_TILE, d), F32), pltpu.VMEM((MOE_TILE, d), BF16),
                        pltpu.VMEM((MOE_TILE, d), F32), pltpu.SemaphoreType.DMA(())])
    return pl.pallas_call(
        _moe_kernel,
        grid_spec=grid_spec,
        out_shape=jax.ShapeDtypeStruct((n_rows, d), F32),
        compiler_params=_params(("arbitrary", "arbitrary")),
        name="expert_swiglu",
    )(tile_expert, n_active, row_token.reshape(n_tiles, 1, MOE_TILE), h, row_gate.reshape(n_rows, 1), wg, wu, wd)


def _moe_routing(route, n_tok):
    e_flat = jnp.concatenate([route[:, 0], route[:, 1]]).astype(jnp.int32)
    w_flat = jnp.concatenate([route[:, 2], route[:, 3]])
    tok = jnp.arange(n_tok, dtype=jnp.int32)
    tok_flat = jnp.concatenate([tok, tok])
    onehot = (e_flat[:, None] == jnp.arange(N_EXPERTS, dtype=jnp.int32)[None, :]).astype(jnp.int32)
    csum = jnp.cumsum(onehot, axis=0)
    rank = jnp.sum(onehot * (csum - 1), axis=1)
    counts = csum[-1]
    padded = ((counts + MOE_TILE - 1) // MOE_TILE) * MOE_TILE
    ends = jnp.cumsum(padded)
    starts = ends - padded
    pos = jnp.sum(onehot * starts[None, :], axis=1) + rank
    n_rows = 2 * n_tok + N_EXPERTS * MOE_TILE
    row_token = jnp.zeros((n_rows,), jnp.int32).at[pos].set(tok_flat)
    row_gate = jnp.zeros((n_rows,), F32).at[pos].set(w_flat)
    tile_start = jnp.arange(n_rows // MOE_TILE, dtype=jnp.int32) * MOE_TILE
    tile_expert = jnp.minimum(jnp.sum((tile_start[:, None] >= ends[None, :]).astype(jnp.int32), axis=1),
                              N_EXPERTS - 1).astype(jnp.int32)
    n_active = (ends[-1:] // MOE_TILE).astype(jnp.int32)
    return row_token, row_gate, tile_expert, n_active, pos[:n_tok], pos[n_tok:]


def _final_kernel(pos_ref, x_ref, mod_ref, g_ref, y_hbm, o_ref, buf_ref, sem):
    tile = x_ref.shape[0]

    def issue(r, carry):
        for k in range(2):
            pltpu.make_async_copy(y_hbm.at[pl.ds(pos_ref[0, 0, k * tile + r], 1)],
                                  buf_ref.at[k, pl.ds(r, 1)], sem).start()
        return carry
    lax.fori_loop(0, tile, issue, 0)
    for k in range(2):
        pltpu.make_async_copy(y_hbm.at[pl.ds(0, tile)], buf_ref.at[k], sem).wait()
    x = x_ref[...] + mod_ref[0, 0][5:6] * (buf_ref[0] + buf_ref[1])
    o_ref[...] = x * lax.rsqrt(jnp.mean(x * x, axis=-1, keepdims=True) + RMS_EPS) * g_ref[...]


def _final_combine(x_lat, y_sorted, pos0, pos1, mod, layer, g_final, dims):
    n_rows, d = x_lat.shape
    n_tiles = n_rows // TOK_TILE
    tiles_per_batch = dims["S"] // TOK_TILE
    pos = jnp.concatenate([pos0.reshape(n_tiles, 1, TOK_TILE), pos1.reshape(n_tiles, 1, TOK_TILE)], axis=2)
    return pl.pallas_call(
        _final_kernel,
        grid=(n_tiles,),
        in_specs=[pl.BlockSpec((1, 1, 2 * TOK_TILE), lambda i: (i, 0, 0), memory_space=pltpu.SMEM),
                  pl.BlockSpec((TOK_TILE, d), lambda i: (i, 0)),
                  pl.BlockSpec((1, 1, 6, d), lambda i: (layer, i // tiles_per_batch, 0, 0)),
                  pl.BlockSpec((1, d), lambda i: (0, 0)),
                  pl.BlockSpec(memory_space=pl.ANY)],
        out_specs=pl.BlockSpec((TOK_TILE, d), lambda i: (i, 0)),
        out_shape=jax.ShapeDtypeStruct((n_rows, d), F32),
        scratch_shapes=[pltpu.VMEM((2, TOK_TILE, d), F32), pltpu.SemaphoreType.DMA(())],
        compiler_params=_params(("arbitrary",)),
        name="expert_combine_final_norm",
    )(pos, x_lat, mod, g_final, y_sorted)


def _rope_tables(s_len):
    t = np.arange(s_len)
    lane = np.arange(LANES) % HEAD_DIM
    quarter = HEAD_DIM // 4
    inv = 1.0 / (ROPE_BASE ** (jnp.arange(quarter, dtype=F32) / quarter))
    pos = np.where(lane[None, :] < HEAD_DIM // 2, (t // GRID_W)[:, None], (t % GRID_W)[:, None]).astype(np.float32)
    ang = jnp.asarray(pos) * inv[lane % quarter][None, :]
    first = jnp.asarray((lane % (HEAD_DIM // 2)) < quarter)[None, :]
    cos, sin = jnp.cos(ang), jnp.sin(ang)
    ident = jnp.zeros((TOK_TILE, LANES), F32)
    return (jnp.concatenate([cos, ident + 1.0]),
            jnp.concatenate([jnp.where(first, -sin, 0.0), ident]),
            jnp.concatenate([jnp.where(first, 0.0, sin), ident]))


def _rearranged_w_in(w):
    offs = np.concatenate([[0], np.cumsum(IN_SIZES)])
    part = lambda j: w[:, int(offs[j]):int(offs[j + 1])]
    pad = jnp.zeros((w.shape[0], LANES - 2 * GLA_LR), w.dtype)
    order = [0, 1, 2, 3, 6, 7, 8, 9, 10, 11, 4, 5]
    return jnp.concatenate([part(j) for j in order] + [pad], axis=1).astype(BF16)


def _decay_up(w_up, b):
    top = jnp.concatenate([w_up[0], jnp.zeros_like(w_up[0])], axis=1)
    bot = jnp.concatenate([jnp.zeros_like(w_up[1]), w_up[1]], axis=1)
    pad = jnp.zeros((LANES - 2 * GLA_LR, 2 * GLA_QK), w_up.dtype)
    return jnp.concatenate([top, bot, pad], axis=0).astype(BF16), jnp.concatenate([b[0], b[1]])[None, :]


def kernel(x, c, ctx, c_ctx, w_mod, b_mod, g_norm1, g_norm2, w_in, gla_w_dec_up, gla_b_dec, gla_g_norm,
           diff_lambda, diff_g_norm, na_rpb, w_out, w_ffn_gate, w_ffn_up, w_ffn_down, w_router, w_moe_gate,
           w_moe_up, w_moe_down, g_final):
    nb, s_len, d = x.shape
    c_len = ctx.shape[1]
    depth = w_mod.shape[0]
    assert c_len == ATT_TILE and s_len % TOK_TILE == 0 and (nb * c_len) % TOK_TILE == 0 and nb < MOD_ROWS
    n_lat = nb * s_len
    dims = {"B": nb, "S": s_len, "n_lat": n_lat}

    x_all = jnp.concatenate([x.reshape(n_lat, d), ctx.reshape(nb * c_len, d)], axis=0)
    cc = jnp.concatenate([c, c_ctx[None, :], jnp.zeros((MOD_ROWS - nb - 1, d), F32)], axis=0)
    mod = _modulation(cc, w_mod, b_mod).reshape(depth, MOD_ROWS, 6, d)
    rope_tabs = _rope_tables(s_len)

    for l in range(depth):
        last = l == depth - 1
        w_cat = _rearranged_w_in(w_in[l])
        wup2, bdec2 = _decay_up(gla_w_dec_up[l], gla_b_dec[l])
        gla, lg, dq, dk, dv, nq, nk, nv = _input_projection(
            x_all, mod, l, g_norm1[l][None, :], w_cat, wup2, bdec2, rope_tabs, dims)
        o_f, o_b = _gla(gla, lg, dims)
        b_diff = _diff_attention(dq, dk, dv, diff_lambda[l], diff_g_norm[l][None, :], l, not last, dims)
        c_na = _neighbourhood_attention(nq, nk, nv, _na_bias_tables(na_rpb[l], s_len // GRID_W), not last, dims)
        g_gla = jnp.tile(gla_g_norm[l], GLA_HEADS)[None, :]
        n_rows = n_lat if last else x_all.shape[0]
        if l % 2 == 0:
            i = l // 2
            x_mid, h = _output_projection(x_all, o_f, o_b, gla, b_diff, c_na, mod, l, g_gla, g_norm2[l][None, :],
                                          w_out[l].astype(BF16), None, n_rows, dims)
            assert not last, "the dense channel mixer is implemented for non-final layers only"
            x_all = _dense_ffn(h, x_mid, mod, l, w_ffn_gate[i].astype(BF16), w_ffn_up[i].astype(BF16),
                               w_ffn_down[i].astype(BF16), dims)
        else:
            i = l // 2
            w_r = jnp.concatenate([w_router[i], jnp.zeros((d, LANES - N_EXPERTS), F32)], axis=1)
            x_mid, h, route = _output_projection(x_all, o_f, o_b, gla, b_diff, c_na, mod, l, g_gla,
                                                 g_norm2[l][None, :], w_out[l].astype(BF16), w_r, n_rows, dims)
            assert last, "the expert layer is implemented for the final layer only"
            row_token, row_gate, tile_expert, n_active, pos0, pos1 = _moe_routing(route, n_rows)
            y_sorted = _moe_ffn(h, row_token, row_gate, tile_expert, n_active, w_moe_gate[i].astype(BF16),
                                w_moe_up[i].astype(BF16), w_moe_down[i].astype(BF16))
            return _final_combine(x_mid, y_sorted, pos0, pos1, mod, l, g_final[None, :], dims).reshape(nb, s_len, d)
    raise NotImplementedError("the final layer must be an expert layer")
```

```python
import functools
import math

import numpy as np
import jax
import jax.numpy as jnp
from jax import lax
from jax.experimental import pallas as pl
from jax.experimental.pallas import tpu as pltpu

F32 = jnp.float32
BF16 = jnp.bfloat16
HIGHEST = lax.Precision.HIGHEST

GRID_W = 64
HEAD_DIM = 64
ROPE_BASE = 10000.0
RMS_EPS = 1e-6
GLA_HEADS = 4
GLA_DK = 32
GLA_DV = 64
GLA_LR = 16
GLA_NORMALIZER = 16.0
GLA_CHUNK = 64
DIFF_HEADS = 4
DIFF_HD = 64
NA_HEADS = 4
NA_HD = 64
WIN_R = 8
WIN_C = 16
N_EXPERTS = 8
GLA_QK = GLA_HEADS * GLA_DK
GLA_V = GLA_HEADS * GLA_DV
DIFF_W = DIFF_HEADS * 2 * DIFF_HD
NA_W = NA_HEADS * NA_HD
IN_SIZES = (GLA_QK, GLA_QK, GLA_V, GLA_V, GLA_LR, GLA_LR, DIFF_W, DIFF_W, DIFF_W, NA_W, NA_W, NA_W)

LANES = 128
TOK_TILE = 512
ATT_TILE = 256
NA_QROWS = 4
NA_KROWS = 12
FFN_CHUNK = 256
MOE_TILE = 512
MOE_FCHUNK = 1792
MOD_ROWS = 16
NEG_BIG = -1e30
LOG2E = math.log2(math.e)
VMEM_LIMIT = 56 * 1024 * 1024

C_GLA = 0
C_DQ = C_GLA + 2 * GLA_QK + 2 * GLA_V
C_DK = C_DQ + DIFF_W
C_NQ = C_DK + DIFF_W
C_NK = C_NQ + NA_W
C_NV = C_NK + NA_W
C_DEC = C_NV + NA_W
C_END = C_DEC + LANES


def _silu(x):
    return x * (1.0 / (1.0 + jnp.exp(-x)))


def _dot(a, b):
    return jnp.dot(a, b, preferred_element_type=F32)


def _dot_nt(a, b):
    return lax.dot_general(a, b, (((1,), (1,)), ((), ())), preferred_element_type=F32)


def _dot_tn(a, b):
    return lax.dot_general(a, b, (((0,), (0,)), ((), ())), preferred_element_type=F32)


def _params(sem, vmem=VMEM_LIMIT):
    return pltpu.CompilerParams(dimension_semantics=sem, vmem_limit_bytes=vmem)


def _mod_kernel(cc_ref, w_ref, b_ref, o_ref):
    s = _silu(cc_ref[...]).astype(BF16)
    o_ref[0] = _dot(s, w_ref[0].astype(BF16)) + b_ref[0]


def _modulation(cc, w_mod, b_mod):
    depth, d, six_d = w_mod.shape
    n = six_d // d
    return pl.pallas_call(
        _mod_kernel,
        grid=(depth, n),
        in_specs=[pl.BlockSpec((MOD_ROWS, d), lambda l, j: (0, 0)),
                  pl.BlockSpec((1, d, d), lambda l, j: (l, 0, j)),
                  pl.BlockSpec((1, 1, d), lambda l, j: (l, 0, j))],
        out_specs=pl.BlockSpec((1, MOD_ROWS, d), lambda l, j: (l, 0, j)),
        out_shape=jax.ShapeDtypeStruct((depth, MOD_ROWS, six_d), F32),
        compiler_params=_params(("parallel", "parallel")),
        name="modulation",
    )(cc, w_mod, b_mod.reshape(depth, 1, six_d))


def _rope(z, a, bt, ct):
    outs = []
    for s in range(z.shape[1] // LANES):
        zs = z[:, s * LANES:(s + 1) * LANES]
        outs.append(zs * a + pltpu.roll(zs, LANES - 16, 1) * bt + pltpu.roll(zs, 16, 1) * ct)
    return jnp.concatenate(outs, axis=1)


def _inproj_kernel(x_ref, mod_ref, g_ref, w_ref, wvt_ref, wup_ref, bdec_ref, ra_ref, rb_ref, rc_ref,
                   gla_ref, lg_ref, dq_ref, dk_ref, dvt_ref, nq_ref, nk_ref, nv_ref):
    x = x_ref[...]
    m = mod_ref[0, 0]
    h = x * lax.rsqrt(jnp.mean(x * x, axis=-1, keepdims=True) + RMS_EPS) * g_ref[...]
    hb = (h * (1.0 + m[1:2]) + m[0:1]).astype(BF16)

    def proj(c0, c1):
        return _dot(hb, w_ref[:, c0:c1])

    gla_ref[...] = proj(C_GLA, C_DQ).astype(BF16)
    a, bt, ct = ra_ref[...], rb_ref[...], rc_ref[...]
    dq_ref[...] = (_rope(proj(C_DQ, C_DK), a, bt, ct) * (DIFF_HD ** -0.5 * LOG2E)).astype(BF16)
    dk_ref[...] = _rope(proj(C_DK, C_NQ), a, bt, ct).astype(BF16)
    dvt_ref[0] = _dot_nt(wvt_ref[...], hb).astype(BF16)
    nq_ref[...] = (proj(C_NQ, C_NK) * NA_HD ** -0.5).astype(BF16)
    nk_ref[...] = proj(C_NK, C_NV).astype(BF16)
    nv_ref[...] = proj(C_NV, C_DEC).astype(BF16)
    logits = _dot(proj(C_DEC, C_END).astype(BF16), wup_ref[...]) + bdec_ref[...]
    log_sig = jnp.minimum(logits, 0.0) - jnp.log1p(jnp.exp(-jnp.abs(logits)))
    lg_ref[...] = log_sig * (1.0 / GLA_NORMALIZER)


def _input_projection(x_all, mod, layer, g1, w_cat, w_dvt, wup2, bdec2, rope_tabs, dims):
    n_tot, d = x_all.shape
    n_lat_tiles = dims["n_lat"] // TOK_TILE
    tiles_per_batch = dims["S"] // TOK_TILE
    n_tiles = n_tot // TOK_TILE
    nb = dims["B"]

    def mod_map(i):
        return (layer, jnp.where(i < n_lat_tiles, i // tiles_per_batch, nb), 0, 0)

    def rope_map(i):
        return (jnp.where(i < n_lat_tiles, i % tiles_per_batch, tiles_per_batch), 0)

    row = lambda w: pl.BlockSpec((TOK_TILE, w), lambda i: (i, 0))
    const = lambda shape: pl.BlockSpec(shape, lambda i: (0,) * len(shape))
    tab = pl.BlockSpec((TOK_TILE, LANES), rope_map)
    widths = (C_DQ, 2 * GLA_QK, DIFF_W, DIFF_W, None, NA_W, NA_W, NA_W)
    dtypes = (BF16, F32, BF16, BF16, BF16, BF16, BF16, BF16)
    vt_spec = pl.BlockSpec((1, DIFF_W, TOK_TILE), lambda i: (i, 0, 0))
    vt_shape = jax.ShapeDtypeStruct((n_tiles, DIFF_W, TOK_TILE), BF16)
    return pl.pallas_call(
        _inproj_kernel,
        grid=(n_tiles,),
        in_specs=[row(d), pl.BlockSpec((1, 1, 6, d), mod_map), const((1, d)), const((d, C_END)), const((DIFF_W, d)),
                  const((LANES, 2 * GLA_QK)), const((1, 2 * GLA_QK)), tab, tab, tab],
        out_specs=[vt_spec if w is None else row(w) for w in widths],
        out_shape=[vt_shape if w is None else jax.ShapeDtypeStruct((n_tot, w), dt) for w, dt in zip(widths, dtypes)],
        compiler_params=_params(("parallel",)),
        name="input_projection",
    )(x_all, mod, g1, w_cat, w_dvt, wup2, bdec2, *rope_tabs)


def _gla_chunk(g_ref, l_ref, o_ref, s_ref, c, fwd, consts):
    tri, tri4, head_k, head_v, bd = consts
    rows = slice(c * GLA_CHUNK, (c + 1) * GLA_CHUNK)
    q = g_ref[rows, 0:GLA_QK].astype(F32) * GLA_DK ** -0.5
    k = g_ref[rows, GLA_QK:2 * GLA_QK].astype(F32)
    v = g_ref[rows, 2 * GLA_QK:2 * GLA_QK + GLA_V]
    lcol = 0 if fwd else GLA_QK
    lg = l_ref[rows, lcol:lcol + GLA_QK]
    b = jnp.dot(tri, lg, precision=HIGHEST, preferred_element_type=F32)
    b_end = b[GLA_CHUNK - 1:GLA_CHUNK] if fwd else b[0:1]
    q_dec = q * jnp.exp(b)
    k_inv = (k * jnp.exp(-b)).astype(BF16)
    k_end = (k * jnp.exp(b_end - b)).astype(BF16)
    qs = jnp.concatenate([jnp.where(head_k == h, q_dec, 0.0) for h in range(GLA_HEADS)], axis=0)
    att = jnp.where(tri4, _dot_nt(qs.astype(BF16), k_inv), 0.0)
    ov = _dot(att.astype(BF16), v)
    o = _dot(q_dec.astype(BF16), s_ref[...].astype(BF16))
    for h in range(GLA_HEADS):
        o = o + jnp.where(head_v == h, ov[h * GLA_CHUNK:(h + 1) * GLA_CHUNK], 0.0)
    o_ref[rows, :] = o
    total = lax.dot_general(lg, jnp.ones((GLA_CHUNK, GLA_V), F32), (((0,), (0,)), ((), ())),
                            precision=HIGHEST, preferred_element_type=F32)
    s_ref[...] = jnp.where(bd, jnp.exp(total) * s_ref[...] + _dot_tn(k_end, v), 0.0)


def _gla_kernel(gf_ref, lf_ref, gb_ref, lb_ref, of_ref, ob_ref, sf_ref, sb_ref):
    @pl.when(pl.program_id(1) == 0)
    def _():
        sf_ref[...] = jnp.zeros_like(sf_ref)
        sb_ref[...] = jnp.zeros_like(sb_ref)

    r = lax.broadcasted_iota(jnp.int32, (GLA_CHUNK, GLA_CHUNK), 0)
    c = lax.broadcasted_iota(jnp.int32, (GLA_CHUNK, GLA_CHUNK), 1)
    r4 = lax.broadcasted_iota(jnp.int32, (GLA_HEADS * GLA_CHUNK, GLA_CHUNK), 0) % GLA_CHUNK
    c4 = lax.broadcasted_iota(jnp.int32, (GLA_HEADS * GLA_CHUNK, GLA_CHUNK), 1)
    head_k = lax.broadcasted_iota(jnp.int32, (1, GLA_QK), 1) // GLA_DK
    head_v = lax.broadcasted_iota(jnp.int32, (1, GLA_V), 1) // GLA_DV
    bd = (lax.broadcasted_iota(jnp.int32, (GLA_QK, GLA_V), 0) // GLA_DK
          == lax.broadcasted_iota(jnp.int32, (GLA_QK, GLA_V), 1) // GLA_DV)
    lower = ((c <= r).astype(F32), c4 <= r4, head_k, head_v, bd)
    upper = ((c >= r).astype(F32), c4 >= r4, head_k, head_v, bd)
    n_chunks = ATT_TILE // GLA_CHUNK
    for j in range(n_chunks):
        _gla_chunk(gf_ref, lf_ref, of_ref, sf_ref, j, True, lower)
        _gla_chunk(gb_ref, lb_ref, ob_ref, sb_ref, n_chunks - 1 - j, False, upper)


def _gla(gla, lg, dims):
    n_tot = gla.shape[0]
    nb, steps = dims["B"], 1 + dims["S"] // ATT_TILE
    lat_blocks = dims["n_lat"] // ATT_TILE
    per_batch = dims["S"] // ATT_TILE

    def fwd_map(b, i):
        return (jnp.where(i == 0, lat_blocks + b, b * per_batch + i - 1), 0)

    def bwd_map(b, i):
        return (jnp.where(i == 0, lat_blocks + b, b * per_batch + per_batch - i), 0)

    return pl.pallas_call(
        _gla_kernel,
        grid=(nb, steps),
        in_specs=[pl.BlockSpec((ATT_TILE, C_DQ), fwd_map), pl.BlockSpec((ATT_TILE, 2 * GLA_QK), fwd_map),
                  pl.BlockSpec((ATT_TILE, C_DQ), bwd_map), pl.BlockSpec((ATT_TILE, 2 * GLA_QK), bwd_map)],
        out_specs=[pl.BlockSpec((ATT_TILE, GLA_V), fwd_map), pl.BlockSpec((ATT_TILE, GLA_V), bwd_map)],
        out_shape=[jax.ShapeDtypeStruct((n_tot, GLA_V), F32)] * 2,
        scratch_shapes=[pltpu.VMEM((GLA_QK, GLA_V), F32), pltpu.VMEM((GLA_QK, GLA_V), F32)],
        compiler_params=_params(("parallel", "arbitrary")),
        name="gla_scan",
    )(gla, lg, gla, lg)


def _diff_kernel(lam_init, n_lat_q, with_ctx, q_ref, kl_ref, vl_ref, kc_ref, vc_ref, lam_ref, g_ref, o_ref,
                 s_ref, l_ref, acc_ref):
    q = q_ref[...]
    t, hw = q.shape
    c_len = kc_ref.shape[0]
    lane = lax.broadcasted_iota(jnp.int32, q.shape, 1)
    zero = jnp.zeros_like(q)
    qm = (jnp.where(lane < DIFF_HD, q, zero), jnp.where(lane >= DIFF_HD, q, zero))

    ctx_tile = [(lambda: kc_ref[...], lambda: vc_ref[0], 0, c_len)]
    lat_tiles = [(lambda j=j: kl_ref[j * TOK_TILE:(j + 1) * TOK_TILE, :], lambda j=j: vl_ref[j],
                  c_len + j * TOK_TILE, TOK_TILE) for j in range(vl_ref.shape[0])]

    def attend(tiles):
        m = [jnp.full((1, t), -jnp.inf, F32)] * 2
        for load_k, _, off, n in tiles:
            k = load_k()
            for mp in range(2):
                s = _dot_nt(k, qm[mp])
                s_ref[mp, off:off + n, :] = s
                m[mp] = jnp.maximum(m[mp], jnp.max(s, axis=0, keepdims=True))
        l = [jnp.zeros((1, t), F32)] * 2
        acc = [jnp.zeros((hw, t), F32)] * 2
        for _, load_vt, off, n in tiles:
            vt = load_vt()
            for mp in range(2):
                p = jnp.exp2(s_ref[mp, off:off + n, :] - m[mp])
                l[mp] = l[mp] + jnp.sum(p, axis=0, keepdims=True)
                acc[mp] = acc[mp] + _dot(vt, p.astype(BF16))
        for mp in range(2):
            l_ref[mp] = l[mp]
            acc_ref[mp] = acc[mp]

    if with_ctx:
        is_lat = pl.program_id(2) < n_lat_q
        pl.when(is_lat)(lambda: attend(ctx_tile + lat_tiles))
        pl.when(jnp.logical_not(is_lat))(lambda: attend(ctx_tile))
    else:
        attend(ctx_tile + lat_tiles)

    lam = lam_ref[...]
    lam_full = (jnp.exp(jnp.sum(lam[0:1] * lam[1:2], axis=-1, keepdims=True))
                - jnp.exp(jnp.sum(lam[2:3] * lam[3:4], axis=-1, keepdims=True)) + lam_init)
    o = jnp.transpose(acc_ref[0] / l_ref[0] - lam_full * (acc_ref[1] / l_ref[1]))
    o = o * lax.rsqrt(jnp.mean(o * o, axis=-1, keepdims=True) + RMS_EPS) * g_ref[...]
    o_ref[...] = (o * (1.0 - lam_init)).astype(BF16)


def _diff_attention(dq, dk, dvt, lam, g_diff, layer, with_ctx, dims):
    n_tot = dq.shape[0]
    nb, s_len = dims["B"], dims["S"]
    n_lat_q = s_len // ATT_TILE
    lat_blocks = dims["n_lat"] // ATT_TILE
    lat_tiles = dims["n_lat"] // TOK_TILE
    ctx_per_tile = TOK_TILE // ATT_TILE
    lam_init = 0.8 - 0.6 * math.exp(-0.3 * layer)
    hw = 2 * DIFF_HD

    def q_map(b, h, i):
        return (jnp.where(i < n_lat_q, b * n_lat_q + i, lat_blocks + b), h)

    k_lat = pl.BlockSpec((s_len, hw), lambda b, h, i: (b, h))
    k_ctx = pl.BlockSpec((ATT_TILE, hw), lambda b, h, i: (lat_blocks + b, h))
    v_lat = pl.BlockSpec((s_len // TOK_TILE, hw, TOK_TILE), lambda b, h, i: (b, h, 0))
    v_ctx = pl.BlockSpec((1, hw, ATT_TILE), lambda b, h, i: (lat_tiles + b // ctx_per_tile, h, b % ctx_per_tile))
    return pl.pallas_call(
        functools.partial(_diff_kernel, lam_init, n_lat_q, with_ctx),
        grid=(nb, DIFF_HEADS, n_lat_q + (1 if with_ctx else 0)),
        in_specs=[pl.BlockSpec((ATT_TILE, hw), q_map), k_lat, v_lat, k_ctx, v_ctx,
                  pl.BlockSpec((4, DIFF_HD), lambda b, h, i: (0, 0)),
                  pl.BlockSpec((1, hw), lambda b, h, i: (0, 0))],
        out_specs=pl.BlockSpec((ATT_TILE, hw), q_map),
        out_shape=jax.ShapeDtypeStruct((n_tot if with_ctx else dims["n_lat"], DIFF_W), BF16),
        scratch_shapes=[pltpu.VMEM((2, ATT_TILE + s_len, ATT_TILE), F32), pltpu.VMEM((2, 1, ATT_TILE), F32),
                        pltpu.VMEM((2, hw, ATT_TILE), F32)],
        compiler_params=_params(("parallel", "parallel", "arbitrary")),
        name="diff_attention",
    )(dq, dk, dvt, dk, dvt, lam, g_diff)


def _na_bias_tables(rpb, rows):
    wr = min(WIN_R, rows)
    n_groups = rows // NA_QROWS
    qc = np.arange(GRID_W)[:, None]
    kc = np.arange(GRID_W)[None, :]
    cs = np.clip(qc - WIN_C // 2, 0, GRID_W - WIN_C)
    col_valid = (kc >= cs) & (kc < cs + WIN_C)
    onehot = ((kc - qc + WIN_C - 1)[None] == np.arange(2 * WIN_C - 1)[:, None, None]) & col_valid[None]
    toe = jnp.einsum("hrc,cqk->hrqk", rpb.astype(F32), jnp.asarray(onehot, F32), precision=HIGHEST)
    toe = jnp.where(col_valid[None, None], toe, NEG_BIG)
    tabs = []
    for g in (0, 1, n_groups - 1):
        r0 = g * NA_QROWS
        u0 = int(np.clip(r0 - WIN_R // 2, 0, rows - NA_KROWS))
        r = r0 + np.arange(NA_QROWS)[:, None]
        kr = u0 + np.arange(NA_KROWS)[None, :]
        rs = np.clip(r - WIN_R // 2, 0, rows - wr)
        row_valid = (kr >= rs) & (kr < rs + wr)
        rb = np.clip(kr - r + WIN_R - 1, 0, 2 * WIN_R - 2)
        t = jnp.where(row_valid[None, :, :, None, None], toe[:, rb], NEG_BIG)
        tabs.append(t.transpose(0, 1, 3, 2, 4).reshape(NA_HEADS, NA_QROWS * GRID_W, NA_KROWS * GRID_W))
    return jnp.stack(tabs)


def _na_kernel(rows, q_ref, kl_ref, vl_ref, kc_ref, vc_ref, bias_ref, o_ref):
    g = pl.program_id(1)
    n_groups = rows // NA_QROWS
    q = q_ref[...]
    t = q.shape[0]
    head = lax.broadcasted_iota(jnp.int32, q.shape, 1) // NA_HD
    zero = jnp.zeros_like(q)
    qs = jnp.concatenate([jnp.where(head == h, q, zero) for h in range(NA_HEADS)], axis=0)
    kc, vc = kc_ref[...], vc_ref[...]
    s_ctx = _dot_nt(qs, kc)
    m_ctx = jnp.max(s_ctx, axis=-1, keepdims=True)

    def finish(o_stack, l):
        o_stack = o_stack / l
        o = jnp.zeros((t, q.shape[1]), F32)
        for h in range(NA_HEADS):
            o = o + jnp.where(head == h, o_stack[h * t:(h + 1) * t], 0.0)
        o_ref[...] = o.astype(BF16)

    @pl.when(g < n_groups)
    def _():
        u0 = jnp.clip(g * NA_QROWS - WIN_R // 2, 0, rows - NA_KROWS)
        win = pl.ds(pl.multiple_of(u0 * GRID_W, GRID_W), NA_KROWS * GRID_W)
        kw, vw = kl_ref[win, :], vl_ref[win, :]
        s_loc = _dot_nt(qs, kw) + bias_ref[0].reshape(NA_HEADS * t, NA_KROWS * GRID_W)
        m = jnp.maximum(m_ctx, jnp.max(s_loc, axis=-1, keepdims=True))
        p_loc = jnp.exp(s_loc - m)
        p_ctx = jnp.exp(s_ctx - m)
        l = jnp.sum(p_loc, axis=-1, keepdims=True) + jnp.sum(p_ctx, axis=-1, keepdims=True)
        finish(_dot(p_loc.astype(BF16), vw) + _dot(p_ctx.astype(BF16), vc), l)

    @pl.when(g >= n_groups)
    def _():
        p_ctx = jnp.exp(s_ctx - m_ctx)
        finish(_dot(p_ctx.astype(BF16), vc), jnp.sum(p_ctx, axis=-1, keepdims=True))


def _neighbourhood_attention(nq, nk, nv, bias_tabs, with_ctx, dims):
    n_tot = nq.shape[0]
    nb, s_len = dims["B"], dims["S"]
    rows = s_len // GRID_W
    n_groups = rows // NA_QROWS
    lat_blocks = dims["n_lat"] // ATT_TILE

    def q_map(b, g):
        return (jnp.where(g < n_groups, b * n_groups + g, lat_blocks + b), 0)

    def bias_map(b, g):
        return (jnp.where(g == 0, 0, jnp.where(g >= n_groups - 1, 2, 1)), 0, 0, 0)

    lat = pl.BlockSpec((s_len, NA_W), lambda b, g: (b, 0))
    ctx = pl.BlockSpec((ATT_TILE, NA_W), lambda b, g: (lat_blocks + b, 0))
    return pl.pallas_call(
        functools.partial(_na_kernel, rows),
        grid=(nb, n_groups + (1 if with_ctx else 0)),
        in_specs=[pl.BlockSpec((ATT_TILE, NA_W), q_map), lat, lat, ctx, ctx,
                  pl.BlockSpec((1,) + bias_tabs.shape[1:], bias_map)],
        out_specs=pl.BlockSpec((ATT_TILE, NA_W), q_map),
        out_shape=jax.ShapeDtypeStruct((n_tot if with_ctx else dims["n_lat"], NA_W), BF16),
        compiler_params=_params(("parallel", "arbitrary")),
        name="neighbourhood_attention",
    )(nq, nk, nv, nk, nv, bias_tabs)


def _route_top2(logits):
    lane = lax.broadcasted_iota(jnp.int32, logits.shape, 1)
    lane_f = lane.astype(F32)
    lg = jnp.where(lane < N_EXPERTS, logits, -jnp.inf)
    m1 = jnp.max(lg, axis=-1, keepdims=True)
    i1 = jnp.min(jnp.where(lg == m1, lane_f, float(LANES)), axis=-1, keepdims=True)
    lg2 = jnp.where(lane_f == i1, -jnp.inf, lg)
    m2 = jnp.max(lg2, axis=-1, keepdims=True)
    i2 = jnp.min(jnp.where(lg2 == m2, lane_f, float(LANES)), axis=-1, keepdims=True)
    e = jnp.exp(m2 - m1)
    w1 = 1.0 / (1.0 + e)
    w2 = e / (1.0 + e)
    return jnp.where(lane == 0, i1, jnp.where(lane == 1, i2, jnp.where(lane == 2, w1,
                                                                         jnp.where(lane == 3, w2, 0.0))))


def _outproj_kernel(moe, x_ref, of_ref, ob_ref, gate_ref, bd_ref, cn_ref, mod_ref, ggla_ref, g2_ref, w_ref,
                    *rest):
    if moe:
        wr_ref, x_out_ref, h_ref, route_ref = rest
    else:
        x_out_ref, h_ref = rest
    m = mod_ref[0, 0]
    o = of_ref[...] + ob_ref[...]
    avg = (lax.broadcasted_iota(jnp.int32, (GLA_V, GLA_V), 0) // GLA_DV
           == lax.broadcasted_iota(jnp.int32, (GLA_V, GLA_V), 1) // GLA_DV).astype(F32) * (1.0 / GLA_DV)
    ms = jnp.dot(o * o, avg, precision=HIGHEST, preferred_element_type=F32)
    a = o * lax.rsqrt(ms + RMS_EPS) * ggla_ref[...] * _silu(gate_ref[...].astype(F32))
    y = (_dot(a.astype(BF16), w_ref[0:GLA_V]) + _dot(bd_ref[...], w_ref[GLA_V:GLA_V + DIFF_W])
         + _dot(cn_ref[...], w_ref[GLA_V + DIFF_W:]))
    x = x_ref[...] + m[2:3] * y
    x_out_ref[...] = x
    h = x * lax.rsqrt(jnp.mean(x * x, axis=-1, keepdims=True) + RMS_EPS) * g2_ref[...]
    h = h * (1.0 + m[4:5]) + m[3:4]
    h_ref[...] = h.astype(h_ref.dtype)
    if moe:
        route_ref[...] = _route_top2(jnp.dot(h, wr_ref[...], precision=HIGHEST, preferred_element_type=F32))


def _output_projection(x_all, o_f, o_b, gla, b_diff, c_na, mod, layer, g_gla, g2, w_out, w_router, n_rows, dims):
    d = x_all.shape[1]
    n_lat_tiles = dims["n_lat"] // TOK_TILE
    tiles_per_batch = dims["S"] // TOK_TILE
    nb = dims["B"]
    moe = w_router is not None

    def mod_map(i):
        return (layer, jnp.where(i < n_lat_tiles, i // tiles_per_batch, nb), 0, 0)

    row = lambda w, cb=0: pl.BlockSpec((TOK_TILE, w), lambda i: (i, cb))
    const = lambda shape: pl.BlockSpec(shape, lambda i: (0,) * len(shape))
    in_specs = [row(d), row(GLA_V), row(GLA_V), row(GLA_V, (2 * GLA_QK + GLA_V) // GLA_V), row(DIFF_W), row(NA_W),
                pl.BlockSpec((1, 1, 6, d), mod_map), const((1, GLA_V)), const((1, d)), const((d, d))]
    args = [x_all, o_f, o_b, gla, b_diff, c_na, mod, g_gla, g2, w_out]
    out_specs = [row(d), row(d)]
    out_shape = [jax.ShapeDtypeStruct((n_rows, d), F32), jax.ShapeDtypeStruct((n_rows, d), F32 if moe else BF16)]
    if moe:
        in_specs.append(const((d, LANES)))
        args.append(w_router)
        out_specs.append(row(LANES))
        out_shape.append(jax.ShapeDtypeStruct((n_rows, LANES), F32))
    return pl.pallas_call(
        functools.partial(_outproj_kernel, moe),
        grid=(n_rows // TOK_TILE,),
        in_specs=in_specs, out_specs=out_specs, out_shape=out_shape,
        compiler_params=_params(("parallel",)),
        name="output_projection",
    )(*args)


def _ffn_kernel(h_ref, x_ref, mod_ref, wg_ref, wu_ref, wd_ref, o_ref, acc_ref):
    h = h_ref[...]
    n_chunks = wg_ref.shape[1] // FFN_CHUNK
    for j in range(n_chunks):
        cols = slice(j * FFN_CHUNK, (j + 1) * FFN_CHUNK)
        act = (_silu(_dot(h, wg_ref[:, cols])) * _dot(h, wu_ref[:, cols])).astype(BF16)
        part = _dot(act, wd_ref[cols, :])
        if j == 0:
            acc_ref[...] = part
        else:
            acc_ref[...] += part
    o_ref[...] = x_ref[...] + mod_ref[0, 0][5:6] * acc_ref[...]


def _dense_ffn(h, x_all, mod, layer, wg, wu, wd, dims):
    n_rows, d = x_all.shape
    f = wg.shape[1]
    n_lat_tiles = dims["n_lat"] // TOK_TILE
    tiles_per_batch = dims["S"] // TOK_TILE
    nb = dims["B"]

    def mod_map(i):
        return (layer, jnp.where(i < n_lat_tiles, i // tiles_per_batch, nb), 0, 0)

    row = lambda: pl.BlockSpec((TOK_TILE, d), lambda i: (i, 0))
    const = lambda shape: pl.BlockSpec(shape, lambda i: (0, 0))
    return pl.pallas_call(
        _ffn_kernel,
        grid=(n_rows // TOK_TILE,),
        in_specs=[row(), row(), pl.BlockSpec((1, 1, 6, d), mod_map), const((d, f)), const((d, f)), const((f, d))],
        out_specs=row(),
        out_shape=jax.ShapeDtypeStruct((n_rows, d), F32),
        scratch_shapes=[pltpu.VMEM((TOK_TILE, d), F32)],
        compiler_params=_params(("parallel",)),
        name="dense_swiglu",
    )(h, x_all, mod, wg, wu, wd)


def _moe_routing(route, n_tok):
    e_flat = jnp.concatenate([route[:, 0], route[:, 1]]).astype(jnp.int32)
    onehot = (e_flat[:, None] == jnp.arange(N_EXPERTS, dtype=jnp.int32)[None, :]).astype(jnp.int32)
    csum = jnp.cumsum(onehot, axis=0)
    counts = csum[-1]
    padded = ((counts + MOE_TILE - 1) // MOE_TILE) * MOE_TILE
    ends = jnp.cumsum(padded)
    pos = jnp.sum(onehot * (csum - 1 + (ends - padded)[None, :]), axis=1)
    n_rows = 2 * n_tok + N_EXPERTS * MOE_TILE
    tile_start = jnp.arange(n_rows // MOE_TILE, dtype=jnp.int32) * MOE_TILE
    tile_expert = jnp.minimum(jnp.sum((tile_start[:, None] >= ends[None, :]).astype(jnp.int32), axis=1),
                              N_EXPERTS - 1).astype(jnp.int32)
    n_active = (ends[-1:] // MOE_TILE).astype(jnp.int32)
    n_tiles = n_tok // TOK_TILE
    pos = jnp.concatenate([pos[:n_tok].reshape(n_tiles, 1, TOK_TILE), pos[n_tok:].reshape(n_tiles, 1, TOK_TILE)],
                          axis=2)
    return pos, n_rows, tile_expert, n_active


def _dispatch_kernel(pos_ref, h_ref, xs_in, xs_out, sem):
    del xs_in
    tile = h_ref.shape[0]

    def issue(r, carry):
        for k in range(2):
            pltpu.make_async_copy(h_ref.at[pl.ds(r, 1)], xs_out.at[pl.ds(pos_ref[0, 0, k * tile + r], 1)],
                                  sem).start()
        return carry
    lax.fori_loop(0, tile, issue, 0)
    for k in range(2):
        pltpu.make_async_copy(h_ref, xs_out.at[pl.ds(0, tile)], sem).wait()


def _moe_dispatch(h, pos, n_rows):
    n_tok, d = h.shape
    return pl.pallas_call(
        _dispatch_kernel,
        grid=(n_tok // TOK_TILE,),
        in_specs=[pl.BlockSpec((1, 1, 2 * TOK_TILE), lambda i: (i, 0, 0), memory_space=pltpu.SMEM),
                  pl.BlockSpec((TOK_TILE, d), lambda i: (i, 0)),
                  pl.BlockSpec(memory_space=pl.ANY)],
        out_specs=pl.BlockSpec(memory_space=pl.ANY),
        out_shape=jax.ShapeDtypeStruct((n_rows, d), h.dtype),
        scratch_shapes=[pltpu.SemaphoreType.DMA(())],
        input_output_aliases={2: 0},
        compiler_params=_params(("arbitrary",)),
        name="expert_dispatch",
    )(pos, h, jnp.zeros((n_rows, d), h.dtype))


def _moe_kernel(te_ref, nact_ref, x_ref, wg_ref, wu_ref, wd_ref, o_ref, xb_ref, acc_ref):
    i, f = pl.program_id(0), pl.program_id(1)
    active = i < nact_ref[0]
    last = f == pl.num_programs(1) - 1

    @pl.when(f == 0)
    def _():
        xb_ref[...] = x_ref[...].astype(BF16)
        acc_ref[...] = jnp.zeros_like(acc_ref)

    @pl.when(active)
    def _():
        x = xb_ref[...]
        for j in range(wg_ref.shape[2] // FFN_CHUNK):
            cols = slice(j * FFN_CHUNK, (j + 1) * FFN_CHUNK)
            act = (_silu(_dot(x, wg_ref[0, :, cols])) * _dot(x, wu_ref[0, :, cols])).astype(BF16)
            acc_ref[...] += _dot(act, wd_ref[0, cols, :])

    @pl.when(last)
    def _():
        o_ref[...] = acc_ref[...]


def _moe_ffn(x_sorted, tile_expert, n_active, wg, wu, wd):
    n_rows, d = x_sorted.shape
    ffn = wg.shape[2]
    n_tiles = n_rows // MOE_TILE
    n_f = ffn // MOE_FCHUNK

    def row(i, f, te, na):
        return jnp.minimum(i, na[0] - 1)

    def fcol(i, f, te, na):
        return jnp.where(i < na[0], f, n_f - 1)

    grid_spec = pltpu.PrefetchScalarGridSpec(
        num_scalar_prefetch=2,
        grid=(n_tiles, n_f),
        in_specs=[pl.BlockSpec((MOE_TILE, d), lambda i, f, te, na: (row(i, f, te, na), 0)),
                  pl.BlockSpec((1, d, MOE_FCHUNK), lambda i, f, te, na: (te[i], 0, fcol(i, f, te, na))),
                  pl.BlockSpec((1, d, MOE_FCHUNK), lambda i, f, te, na: (te[i], 0, fcol(i, f, te, na))),
                  pl.BlockSpec((1, MOE_FCHUNK, d), lambda i, f, te, na: (te[i], fcol(i, f, te, na), 0))],
        out_specs=pl.BlockSpec((MOE_TILE, d), lambda i, f, te, na: (i, 0)),
        scratch_shapes=[pltpu.VMEM((MOE_TILE, d), BF16), pltpu.VMEM((MOE_TILE, d), F32)])
    return pl.pallas_call(
        _moe_kernel,
        grid_spec=grid_spec,
        out_shape=jax.ShapeDtypeStruct((n_rows, d), F32),
        compiler_params=_params(("arbitrary", "arbitrary")),
        name="expert_swiglu",
    )(tile_expert, n_active, x_sorted, wg, wu, wd)


def _final_kernel(pos_ref, x_ref, route_ref, mod_ref, g_ref, y_hbm, o_ref, buf_ref, sem):
    tile = x_ref.shape[0]

    def issue(r, carry):
        for k in range(2):
            pltpu.make_async_copy(y_hbm.at[pl.ds(pos_ref[0, 0, k * tile + r], 1)],
                                  buf_ref.at[k, pl.ds(r, 1)], sem).start()
        return carry
    lax.fori_loop(0, tile, issue, 0)
    for k in range(2):
        pltpu.make_async_copy(y_hbm.at[pl.ds(0, tile)], buf_ref.at[k], sem).wait()
    route = route_ref[...]
    y = route[:, 2:3] * buf_ref[0] + route[:, 3:4] * buf_ref[1]
    x = x_ref[...] + mod_ref[0, 0][5:6] * y
    o_ref[...] = x * lax.rsqrt(jnp.mean(x * x, axis=-1, keepdims=True) + RMS_EPS) * g_ref[...]


def _final_combine(x_lat, y_sorted, pos, route, mod, layer, g_final, dims):
    n_rows, d = x_lat.shape
    n_tiles = n_rows // TOK_TILE
    tiles_per_batch = dims["S"] // TOK_TILE
    return pl.pallas_call(
        _final_kernel,
        grid=(n_tiles,),
        in_specs=[pl.BlockSpec((1, 1, 2 * TOK_TILE), lambda i: (i, 0, 0), memory_space=pltpu.SMEM),
                  pl.BlockSpec((TOK_TILE, d), lambda i: (i, 0)),
                  pl.BlockSpec((TOK_TILE, LANES), lambda i: (i, 0)),
                  pl.BlockSpec((1, 1, 6, d), lambda i: (layer, i // tiles_per_batch, 0, 0)),
                  pl.BlockSpec((1, d), lambda i: (0, 0)),
                  pl.BlockSpec(memory_space=pl.ANY)],
        out_specs=pl.BlockSpec((TOK_TILE, d), lambda i: (i, 0)),
        out_shape=jax.ShapeDtypeStruct((n_rows, d), F32),
        scratch_shapes=[pltpu.VMEM((2, TOK_TILE, d), F32), pltpu.SemaphoreType.DMA(())],
        compiler_params=_params(("arbitrary",)),
        name="expert_combine_final_norm",
    )(pos, x_lat, route, mod, g_final, y_sorted)


def _rope_tables(s_len):
    t = np.arange(s_len)
    lane = np.arange(LANES) % HEAD_DIM
    quarter = HEAD_DIM // 4
    inv = 1.0 / (ROPE_BASE ** (jnp.arange(quarter, dtype=F32) / quarter))
    pos = np.where(lane[None, :] < HEAD_DIM // 2, (t // GRID_W)[:, None], (t % GRID_W)[:, None]).astype(np.float32)
    ang = jnp.asarray(pos) * inv[lane % quarter][None, :]
    first = jnp.asarray((lane % (HEAD_DIM // 2)) < quarter)[None, :]
    cos, sin = jnp.cos(ang), jnp.sin(ang)
    ident = jnp.zeros((TOK_TILE, LANES), F32)
    return (jnp.concatenate([cos, ident + 1.0]),
            jnp.concatenate([jnp.where(first, -sin, 0.0), ident]),
            jnp.concatenate([jnp.where(first, 0.0, sin), ident]))


def _rearranged_w_in(w):
    offs = np.concatenate([[0], np.cumsum(IN_SIZES)])
    part = lambda j: w[:, int(offs[j]):int(offs[j + 1])]
    pad = jnp.zeros((w.shape[0], LANES - 2 * GLA_LR), w.dtype)
    order = [0, 1, 2, 3, 6, 7, 9, 10, 11, 4, 5]
    w_cat = jnp.concatenate([part(j) for j in order] + [pad], axis=1).astype(BF16)
    return w_cat, jnp.transpose(part(8)).astype(BF16)


def _decay_up(w_up, b):
    top = jnp.concatenate([w_up[0], jnp.zeros_like(w_up[0])], axis=1)
    bot = jnp.concatenate([jnp.zeros_like(w_up[1]), w_up[1]], axis=1)
    pad = jnp.zeros((LANES - 2 * GLA_LR, 2 * GLA_QK), w_up.dtype)
    return jnp.concatenate([top, bot, pad], axis=0).astype(BF16), jnp.concatenate([b[0], b[1]])[None, :]


def kernel(x, c, ctx, c_ctx, w_mod, b_mod, g_norm1, g_norm2, w_in, gla_w_dec_up, gla_b_dec, gla_g_norm,
           diff_lambda, diff_g_norm, na_rpb, w_out, w_ffn_gate, w_ffn_up, w_ffn_down, w_router, w_moe_gate,
           w_moe_up, w_moe_down, g_final):
    nb, s_len, d = x.shape
    c_len = ctx.shape[1]
    depth = w_mod.shape[0]
    assert c_len == ATT_TILE and s_len % TOK_TILE == 0 and (nb * c_len) % TOK_TILE == 0 and nb < MOD_ROWS
    n_lat = nb * s_len
    dims = {"B": nb, "S": s_len, "n_lat": n_lat}

    x_all = jnp.concatenate([x.reshape(n_lat, d), ctx.reshape(nb * c_len, d)], axis=0)
    cc = jnp.concatenate([c, c_ctx[None, :], jnp.zeros((MOD_ROWS - nb - 1, d), F32)], axis=0)
    mod = _modulation(cc, w_mod, b_mod).reshape(depth, MOD_ROWS, 6, d)
    rope_tabs = _rope_tables(s_len)

    for l in range(depth):
        last = l == depth - 1
        w_cat, w_dvt = _rearranged_w_in(w_in[l])
        wup2, bdec2 = _decay_up(gla_w_dec_up[l], gla_b_dec[l])
        gla, lg, dq, dk, dvt, nq, nk, nv = _input_projection(
            x_all, mod, l, g_norm1[l][None, :], w_cat, w_dvt, wup2, bdec2, rope_tabs, dims)
        o_f, o_b = _gla(gla, lg, dims)
        b_diff = _diff_attention(dq, dk, dvt, diff_lambda[l], diff_g_norm[l][None, :], l, not last, dims)
        c_na = _neighbourhood_attention(nq, nk, nv, _na_bias_tables(na_rpb[l], s_len // GRID_W), not last, dims)
        g_gla = jnp.tile(gla_g_norm[l], GLA_HEADS)[None, :]
        n_rows = n_lat if last else x_all.shape[0]
        if l % 2 == 0:
            i = l // 2
            x_mid, h = _output_projection(x_all, o_f, o_b, gla, b_diff, c_na, mod, l, g_gla, g_norm2[l][None, :],
                                          w_out[l].astype(BF16), None, n_rows, dims)
            assert not last, "the dense channel mixer is implemented for non-final layers only"
            x_all = _dense_ffn(h, x_mid, mod, l, w_ffn_gate[i].astype(BF16), w_ffn_up[i].astype(BF16),
                               w_ffn_down[i].astype(BF16), dims)
        else:
            i = l // 2
            w_r = jnp.concatenate([w_router[i], jnp.zeros((d, LANES - N_EXPERTS), F32)], axis=1)
            x_mid, h, route = _output_projection(x_all, o_f, o_b, gla, b_diff, c_na, mod, l, g_gla,
                                                 g_norm2[l][None, :], w_out[l].astype(BF16), w_r, n_rows, dims)
            assert last, "the expert layer is implemented for the final layer only"
            pos, n_sorted, tile_expert, n_active = _moe_routing(route, n_rows)
            y_sorted = _moe_ffn(_moe_dispatch(h, pos, n_sorted), tile_expert, n_active, w_moe_gate[i].astype(BF16),
                                w_moe_up[i].astype(BF16), w_moe_down[i].astype(BF16))
            return _final_combine(x_mid, y_sorted, pos, route, mod, l, g_final[None, :], dims).reshape(nb, s_len, d)
    raise NotImplementedError("the final layer must be an expert layer")
```

```python
import functools
import math

import numpy as np
import jax
import jax.numpy as jnp
from jax import lax
from jax.experimental import pallas as pl
from jax.experimental.pallas import tpu as pltpu

F32 = jnp.float32
BF16 = jnp.bfloat16
HIGHEST = lax.Precision.HIGHEST

GRID_W = 64
HEAD_DIM = 64
ROPE_BASE = 10000.0
RMS_EPS = 1e-6
GLA_HEADS = 4
GLA_DK = 32
GLA_DV = 64
GLA_LR = 16
GLA_NORMALIZER = 16.0
GLA_CHUNK = 64
DIFF_HEADS = 4
DIFF_HD = 64
NA_HEADS = 4
NA_HD = 64
WIN_R = 8
WIN_C = 16
N_EXPERTS = 8
GLA_QK = GLA_HEADS * GLA_DK
GLA_V = GLA_HEADS * GLA_DV
DIFF_W = DIFF_HEADS * 2 * DIFF_HD
NA_W = NA_HEADS * NA_HD
IN_SIZES = (GLA_QK, GLA_QK, GLA_V, GLA_V, GLA_LR, GLA_LR, DIFF_W, DIFF_W, DIFF_W, NA_W, NA_W, NA_W)

LANES = 128
TOK_TILE = 512
ATT_TILE = 256
DIFF_QT = 512
NA_QROWS = 4
NA_KROWS = 12
FFN_CHUNK = 256
MOE_TILE = 512
MOE_FCHUNK = 1792
MOD_ROWS = 16
NEG_BIG = -1e30
LOG2E = math.log2(math.e)
VMEM_LIMIT = 56 * 1024 * 1024

C_GLA = 0
C_DQ = C_GLA + 2 * GLA_QK + 2 * GLA_V
C_DK = C_DQ + DIFF_W
C_NQ = C_DK + DIFF_W
C_NK = C_NQ + NA_W
C_NV = C_NK + NA_W
C_DEC = C_NV + NA_W
C_END = C_DEC + LANES


def _silu(x):
    return x * (1.0 / (1.0 + jnp.exp(-x)))


def _dot(a, b):
    return jnp.dot(a, b, preferred_element_type=F32)


def _dot_nt(a, b):
    return lax.dot_general(a, b, (((1,), (1,)), ((), ())), preferred_element_type=F32)


def _dot_tn(a, b):
    return lax.dot_general(a, b, (((0,), (0,)), ((), ())), preferred_element_type=F32)


def _params(sem, vmem=VMEM_LIMIT):
    return pltpu.CompilerParams(dimension_semantics=sem, vmem_limit_bytes=vmem)


def _mod_kernel(cc_ref, w_ref, b_ref, o_ref):
    s = _silu(cc_ref[...]).astype(BF16)
    o_ref[0] = _dot(s, w_ref[0].astype(BF16)) + b_ref[0]


def _modulation(cc, w_mod, b_mod):
    depth, d, six_d = w_mod.shape
    n = six_d // d
    return pl.pallas_call(
        _mod_kernel,
        grid=(depth, n),
        in_specs=[pl.BlockSpec((MOD_ROWS, d), lambda l, j: (0, 0)),
                  pl.BlockSpec((1, d, d), lambda l, j: (l, 0, j)),
                  pl.BlockSpec((1, 1, d), lambda l, j: (l, 0, j))],
        out_specs=pl.BlockSpec((1, MOD_ROWS, d), lambda l, j: (l, 0, j)),
        out_shape=jax.ShapeDtypeStruct((depth, MOD_ROWS, six_d), F32),
        compiler_params=_params(("parallel", "parallel")),
        name="modulation",
    )(cc, w_mod, b_mod.reshape(depth, 1, six_d))


def _rope(z, a, bt, ct):
    outs = []
    for s in range(z.shape[1] // LANES):
        zs = z[:, s * LANES:(s + 1) * LANES]
        outs.append(zs * a + pltpu.roll(zs, LANES - 16, 1) * bt + pltpu.roll(zs, 16, 1) * ct)
    return jnp.concatenate(outs, axis=1)


def _inproj_kernel(x_ref, mod_ref, g_ref, w_ref, wvt_ref, wup_ref, bdec_ref, ra_ref, rb_ref, rc_ref,
                   gla_ref, lg_ref, dq_ref, dk_ref, dvt_ref, nq_ref, nk_ref, nv_ref):
    x = x_ref[...]
    m = mod_ref[0, 0]
    h = x * lax.rsqrt(jnp.mean(x * x, axis=-1, keepdims=True) + RMS_EPS) * g_ref[...]
    hb = (h * (1.0 + m[1:2]) + m[0:1]).astype(BF16)

    def proj(c0, c1):
        return _dot(hb, w_ref[:, c0:c1])

    gla_ref[...] = proj(C_GLA, C_DQ).astype(BF16)
    a, bt, ct = ra_ref[...], rb_ref[...], rc_ref[...]
    dq_ref[...] = (_rope(proj(C_DQ, C_DK), a, bt, ct) * (DIFF_HD ** -0.5 * LOG2E)).astype(BF16)
    dk_ref[...] = _rope(proj(C_DK, C_NQ), a, bt, ct).astype(BF16)
    dvt_ref[0] = _dot_nt(wvt_ref[...], hb).astype(BF16)
    nq_ref[...] = (proj(C_NQ, C_NK) * NA_HD ** -0.5).astype(BF16)
    nk_ref[...] = proj(C_NK, C_NV).astype(BF16)
    nv_ref[...] = proj(C_NV, C_DEC).astype(BF16)
    logits = _dot(proj(C_DEC, C_END).astype(BF16), wup_ref[...]) + bdec_ref[...]
    log_sig = jnp.minimum(logits, 0.0) - jnp.log1p(jnp.exp(-jnp.abs(logits)))
    lg_ref[...] = log_sig * (1.0 / GLA_NORMALIZER)


def _input_projection(x_all, mod, layer, g1, w_cat, w_dvt, wup2, bdec2, rope_tabs, dims):
    n_tot, d = x_all.shape
    n_lat_tiles = dims["n_lat"] // TOK_TILE
    tiles_per_batch = dims["S"] // TOK_TILE
    n_tiles = n_tot // TOK_TILE
    nb = dims["B"]

    def mod_map(i):
        return (layer, jnp.where(i < n_lat_tiles, i // tiles_per_batch, nb), 0, 0)

    def rope_map(i):
        return (jnp.where(i < n_lat_tiles, i % tiles_per_batch, tiles_per_batch), 0)

    row = lambda w: pl.BlockSpec((TOK_TILE, w), lambda i: (i, 0))
    const = lambda shape: pl.BlockSpec(shape, lambda i: (0,) * len(shape))
    tab = pl.BlockSpec((TOK_TILE, LANES), rope_map)
    widths = (C_DQ, 2 * GLA_QK, DIFF_W, DIFF_W, None, NA_W, NA_W, NA_W)
    dtypes = (BF16, F32, BF16, BF16, BF16, BF16, BF16, BF16)
    vt_spec = pl.BlockSpec((1, DIFF_W, TOK_TILE), lambda i: (i, 0, 0))
    vt_shape = jax.ShapeDtypeStruct((n_tiles, DIFF_W, TOK_TILE), BF16)
    return pl.pallas_call(
        _inproj_kernel,
        grid=(n_tiles,),
        in_specs=[row(d), pl.BlockSpec((1, 1, 6, d), mod_map), const((1, d)), const((d, C_END)), const((DIFF_W, d)),
                  const((LANES, 2 * GLA_QK)), const((1, 2 * GLA_QK)), tab, tab, tab],
        out_specs=[vt_spec if w is None else row(w) for w in widths],
        out_shape=[vt_shape if w is None else jax.ShapeDtypeStruct((n_tot, w), dt) for w, dt in zip(widths, dtypes)],
        compiler_params=_params(("parallel",)),
        name="input_projection",
    )(x_all, mod, g1, w_cat, w_dvt, wup2, bdec2, *rope_tabs)


def _gla_chunk(g_ref, l_ref, o_ref, s_ref, c, fwd, consts):
    tri, tri4, head_k, head_v, bd = consts
    rows = slice(c * GLA_CHUNK, (c + 1) * GLA_CHUNK)
    q = g_ref[rows, 0:GLA_QK].astype(F32) * GLA_DK ** -0.5
    k = g_ref[rows, GLA_QK:2 * GLA_QK].astype(F32)
    v = g_ref[rows, 2 * GLA_QK:2 * GLA_QK + GLA_V]
    lcol = 0 if fwd else GLA_QK
    lg = l_ref[rows, lcol:lcol + GLA_QK]
    lg_hi = lg.astype(BF16)
    lg_lo = (lg - lg_hi.astype(F32)).astype(BF16)
    b = _dot(tri, lg_hi) + _dot(tri, lg_lo)
    b_end = b[GLA_CHUNK - 1:GLA_CHUNK] if fwd else b[0:1]
    q_dec = q * jnp.exp(b)
    k_inv = (k * jnp.exp(-b)).astype(BF16)
    k_end = (k * jnp.exp(b_end - b)).astype(BF16)
    qs = jnp.concatenate([jnp.where(head_k == h, q_dec, 0.0) for h in range(GLA_HEADS)], axis=0)
    att = jnp.where(tri4, _dot_nt(qs.astype(BF16), k_inv), 0.0)
    ov = _dot(att.astype(BF16), v)
    o = _dot_nt(q_dec.astype(BF16), s_ref[...].astype(BF16))
    for h in range(GLA_HEADS):
        o = o + jnp.where(head_v == h, ov[h * GLA_CHUNK:(h + 1) * GLA_CHUNK], 0.0)
    o_ref[rows, :] = o
    s_ref[...] = jnp.where(bd, jnp.exp(b_end) * s_ref[...] + _dot_tn(v, k_end), 0.0)


def _gla_kernel(gf_ref, lf_ref, gb_ref, lb_ref, of_ref, ob_ref, sf_ref, sb_ref):
    @pl.when(pl.program_id(1) == 0)
    def _():
        sf_ref[...] = jnp.zeros_like(sf_ref)
        sb_ref[...] = jnp.zeros_like(sb_ref)

    r = lax.broadcasted_iota(jnp.int32, (GLA_CHUNK, GLA_CHUNK), 0)
    c = lax.broadcasted_iota(jnp.int32, (GLA_CHUNK, GLA_CHUNK), 1)
    r4 = lax.broadcasted_iota(jnp.int32, (GLA_HEADS * GLA_CHUNK, GLA_CHUNK), 0) % GLA_CHUNK
    c4 = lax.broadcasted_iota(jnp.int32, (GLA_HEADS * GLA_CHUNK, GLA_CHUNK), 1)
    head_k = lax.broadcasted_iota(jnp.int32, (1, GLA_QK), 1) // GLA_DK
    head_v = lax.broadcasted_iota(jnp.int32, (1, GLA_V), 1) // GLA_DV
    bd = (lax.broadcasted_iota(jnp.int32, (GLA_V, GLA_QK), 0) // GLA_DV
          == lax.broadcasted_iota(jnp.int32, (GLA_V, GLA_QK), 1) // GLA_DK)
    lower = ((c <= r).astype(BF16), c4 <= r4, head_k, head_v, bd)
    upper = ((c >= r).astype(BF16), c4 >= r4, head_k, head_v, bd)
    n_chunks = ATT_TILE // GLA_CHUNK
    for j in range(n_chunks):
        _gla_chunk(gf_ref, lf_ref, of_ref, sf_ref, j, True, lower)
        _gla_chunk(gb_ref, lb_ref, ob_ref, sb_ref, n_chunks - 1 - j, False, upper)


def _gla(gla, lg, dims):
    n_tot = gla.shape[0]
    nb, steps = dims["B"], 1 + dims["S"] // ATT_TILE
    lat_blocks = dims["n_lat"] // ATT_TILE
    per_batch = dims["S"] // ATT_TILE

    def fwd_map(b, i):
        return (jnp.where(i == 0, lat_blocks + b, b * per_batch + i - 1), 0)

    def bwd_map(b, i):
        return (jnp.where(i == 0, lat_blocks + b, b * per_batch + per_batch - i), 0)

    return pl.pallas_call(
        _gla_kernel,
        grid=(nb, steps),
        in_specs=[pl.BlockSpec((ATT_TILE, C_DQ), fwd_map), pl.BlockSpec((ATT_TILE, 2 * GLA_QK), fwd_map),
                  pl.BlockSpec((ATT_TILE, C_DQ), bwd_map), pl.BlockSpec((ATT_TILE, 2 * GLA_QK), bwd_map)],
        out_specs=[pl.BlockSpec((ATT_TILE, GLA_V), fwd_map), pl.BlockSpec((ATT_TILE, GLA_V), bwd_map)],
        out_shape=[jax.ShapeDtypeStruct((n_tot, GLA_V), F32)] * 2,
        scratch_shapes=[pltpu.VMEM((GLA_V, GLA_QK), F32), pltpu.VMEM((GLA_V, GLA_QK), F32)],
        compiler_params=_params(("parallel", "arbitrary")),
        name="gla_scan",
    )(gla, lg, gla, lg)


def _diff_kernel(lam_init, latent, *refs):
    if latent:
        q_ref, kl_ref, vl_ref, kc_ref, vc_ref, lam_ref, g_ref, o_ref, *s_refs = refs
    else:
        q_ref, kc_ref, vc_ref, lam_ref, g_ref, _, o_ref, *s_refs = refs
    q = q_ref[...]
    t, hw = q.shape
    c_len = kc_ref.shape[0]
    lane = lax.broadcasted_iota(jnp.int32, q.shape, 1)
    zero = jnp.zeros_like(q)
    qm = (jnp.where(lane < DIFF_HD, q, zero), jnp.where(lane >= DIFF_HD, q, zero))

    tiles = [(lambda: kc_ref[...], lambda: vc_ref[0], 0, c_len)]
    if latent:
        tiles += [(lambda j=j: kl_ref[j * TOK_TILE:(j + 1) * TOK_TILE, :], lambda j=j: vl_ref[j],
                   c_len + j * TOK_TILE, TOK_TILE) for j in range(vl_ref.shape[0])]

    gw = min(t, ATT_TILE)
    groups = [(mp, c0) for mp in range(2) for c0 in range(0, t, gw)]

    def pass1(g, tile, m):
        mp, c0 = groups[g]
        load_k, _, off, n = tile
        s = _dot_nt(load_k(), qm[mp][c0:c0 + gw])
        s_refs[g][off:off + n, :] = s
        return jnp.maximum(m, jnp.max(s, axis=0, keepdims=True))

    def pass2(g, tile, m, l, acc):
        _, load_vt, off, n = tile
        p = jnp.exp2(s_refs[g][off:off + n, :] - m)
        return l + jnp.sum(p, axis=0, keepdims=True), acc + _dot(load_vt(), p.astype(BF16))

    m_prev, outs = None, []
    for g in range(len(groups) + 1):
        m = jnp.full((1, gw), -jnp.inf, F32)
        l, acc = jnp.zeros((1, gw), F32), jnp.zeros((hw, gw), F32)
        for tile in tiles:
            if g < len(groups):
                m = pass1(g, tile, m)
            if g > 0:
                l, acc = pass2(g - 1, tile, m_prev, l, acc)
        if g > 0:
            outs.append(acc / l)
        m_prev = m
    n_half = len(groups) // 2
    on = [jnp.concatenate(outs[i * n_half:(i + 1) * n_half], axis=1) for i in range(2)]

    lam = lam_ref[...]
    lam_full = (jnp.exp(jnp.sum(lam[0:1] * lam[1:2], axis=-1, keepdims=True))
                - jnp.exp(jnp.sum(lam[2:3] * lam[3:4], axis=-1, keepdims=True)) + lam_init)
    o = jnp.transpose(on[0] - lam_full * on[1])
    o = o * lax.rsqrt(jnp.mean(o * o, axis=-1, keepdims=True) + RMS_EPS) * g_ref[...]
    o_ref[...] = (o * (1.0 - lam_init)).astype(BF16)


def _diff_attention(dq, dk, dvt, lam, g_diff, layer, with_ctx, dims):
    n_tot = dq.shape[0]
    nb, s_len = dims["B"], dims["S"]
    n_lat_q = s_len // DIFF_QT
    lat_blocks = dims["n_lat"] // ATT_TILE
    lat_tiles = dims["n_lat"] // TOK_TILE
    ctx_per_tile = TOK_TILE // ATT_TILE
    lam_init = 0.8 - 0.6 * math.exp(-0.3 * layer)
    hw = 2 * DIFF_HD
    out_shape = jax.ShapeDtypeStruct((n_tot if with_ctx else dims["n_lat"], DIFF_W), BF16)

    k_ctx = pl.BlockSpec((ATT_TILE, hw), lambda b, h, *_: (lat_blocks + b, h))
    v_ctx = pl.BlockSpec((1, hw, ATT_TILE), lambda b, h, *_: (lat_tiles + b // ctx_per_tile, h, b % ctx_per_tile))
    small = [pl.BlockSpec((4, DIFF_HD), lambda *_: (0, 0)), pl.BlockSpec((1, hw), lambda *_: (0, 0))]
    q_lat = pl.BlockSpec((DIFF_QT, hw), lambda b, h, i: (b * n_lat_q + i, h))
    out = pl.pallas_call(
        functools.partial(_diff_kernel, lam_init, True),
        grid=(nb, DIFF_HEADS, n_lat_q),
        in_specs=[q_lat, pl.BlockSpec((s_len, hw), lambda b, h, i: (b, h)),
                  pl.BlockSpec((s_len // TOK_TILE, hw, TOK_TILE), lambda b, h, i: (b, h, 0)), k_ctx, v_ctx] + small,
        out_specs=q_lat,
        out_shape=out_shape,
        scratch_shapes=[pltpu.VMEM((ATT_TILE + s_len, ATT_TILE), F32)] * (2 * DIFF_QT // ATT_TILE),
        compiler_params=_params(("parallel", "parallel", "arbitrary")),
        name="diff_attention",
    )(dq, dk, dvt, dk, dvt, lam, g_diff)
    if not with_ctx:
        return out
    q_ctx = pl.BlockSpec((ATT_TILE, hw), lambda b, h: (lat_blocks + b, h))
    return pl.pallas_call(
        functools.partial(_diff_kernel, lam_init, False),
        grid=(nb, DIFF_HEADS),
        in_specs=[q_ctx, k_ctx, v_ctx] + small + [pl.BlockSpec(memory_space=pl.ANY)],
        out_specs=q_ctx,
        out_shape=out_shape,
        scratch_shapes=[pltpu.VMEM((ATT_TILE, ATT_TILE), F32)] * 2,
        input_output_aliases={5: 0},
        compiler_params=_params(("parallel", "parallel")),
        name="diff_attention_context",
    )(dq, dk, dvt, lam, g_diff, out)


def _na_bias_tables(rpb, rows):
    wr = min(WIN_R, rows)
    n_groups = rows // NA_QROWS
    qc = np.arange(GRID_W)[:, None]
    kc = np.arange(GRID_W)[None, :]
    cs = np.clip(qc - WIN_C // 2, 0, GRID_W - WIN_C)
    col_valid = (kc >= cs) & (kc < cs + WIN_C)
    onehot = ((kc - qc + WIN_C - 1)[None] == np.arange(2 * WIN_C - 1)[:, None, None]) & col_valid[None]
    toe = jnp.einsum("hrc,cqk->hrqk", rpb.astype(F32), jnp.asarray(onehot, F32), precision=HIGHEST)
    toe = jnp.where(col_valid[None, None], toe, NEG_BIG)
    tabs = []
    for g in (0, 1, n_groups - 1):
        r0 = g * NA_QROWS
        u0 = int(np.clip(r0 - WIN_R // 2, 0, rows - NA_KROWS))
        r = r0 + np.arange(NA_QROWS)[:, None]
        kr = u0 + np.arange(NA_KROWS)[None, :]
        rs = np.clip(r - WIN_R // 2, 0, rows - wr)
        row_valid = (kr >= rs) & (kr < rs + wr)
        rb = np.clip(kr - r + WIN_R - 1, 0, 2 * WIN_R - 2)
        t = jnp.where(row_valid[None, :, :, None, None], toe[:, rb], NEG_BIG)
        tabs.append(t.transpose(0, 1, 3, 2, 4).reshape(NA_HEADS, NA_QROWS * GRID_W, NA_KROWS * GRID_W))
    return jnp.stack(tabs)


def _na_kernel(rows, q_ref, kl_ref, vl_ref, kc_ref, vc_ref, bias_ref, o_ref):
    g = pl.program_id(1)
    n_groups = rows // NA_QROWS
    q = q_ref[...]
    t = q.shape[0]
    head = lax.broadcasted_iota(jnp.int32, q.shape, 1) // NA_HD
    zero = jnp.zeros_like(q)
    qs = jnp.concatenate([jnp.where(head == h, q, zero) for h in range(NA_HEADS)], axis=0)
    kc, vc = kc_ref[...], vc_ref[...]
    s_ctx = _dot_nt(qs, kc)
    m_ctx = jnp.max(s_ctx, axis=-1, keepdims=True)

    def finish(o_stack, l):
        o_stack = o_stack / l
        o = jnp.zeros((t, q.shape[1]), F32)
        for h in range(NA_HEADS):
            o = o + jnp.where(head == h, o_stack[h * t:(h + 1) * t], 0.0)
        o_ref[...] = o.astype(BF16)

    @pl.when(g < n_groups)
    def _():
        u0 = jnp.clip(g * NA_QROWS - WIN_R // 2, 0, rows - NA_KROWS)
        win = pl.ds(pl.multiple_of(u0 * GRID_W, GRID_W), NA_KROWS * GRID_W)
        kw, vw = kl_ref[win, :], vl_ref[win, :]
        s_loc = _dot_nt(qs, kw) + bias_ref[0].reshape(NA_HEADS * t, NA_KROWS * GRID_W)
        m = jnp.maximum(m_ctx, jnp.max(s_loc, axis=-1, keepdims=True))
        p_loc = jnp.exp(s_loc - m)
        p_ctx = jnp.exp(s_ctx - m)
        l = jnp.sum(p_loc, axis=-1, keepdims=True) + jnp.sum(p_ctx, axis=-1, keepdims=True)
        finish(_dot(p_loc.astype(BF16), vw) + _dot(p_ctx.astype(BF16), vc), l)

    @pl.when(g >= n_groups)
    def _():
        p_ctx = jnp.exp(s_ctx - m_ctx)
        finish(_dot(p_ctx.astype(BF16), vc), jnp.sum(p_ctx, axis=-1, keepdims=True))


def _neighbourhood_attention(nq, nk, nv, bias_tabs, with_ctx, dims):
    n_tot = nq.shape[0]
    nb, s_len = dims["B"], dims["S"]
    rows = s_len // GRID_W
    n_groups = rows // NA_QROWS
    lat_blocks = dims["n_lat"] // ATT_TILE

    def q_map(b, g):
        return (jnp.where(g < n_groups, b * n_groups + g, lat_blocks + b), 0)

    def bias_map(b, g):
        return (jnp.where(g == 0, 0, jnp.where(g >= n_groups - 1, 2, 1)), 0, 0, 0)

    lat = pl.BlockSpec((s_len, NA_W), lambda b, g: (b, 0))
    ctx = pl.BlockSpec((ATT_TILE, NA_W), lambda b, g: (lat_blocks + b, 0))
    return pl.pallas_call(
        functools.partial(_na_kernel, rows),
        grid=(nb, n_groups + (1 if with_ctx else 0)),
        in_specs=[pl.BlockSpec((ATT_TILE, NA_W), q_map), lat, lat, ctx, ctx,
                  pl.BlockSpec((1,) + bias_tabs.shape[1:], bias_map)],
        out_specs=pl.BlockSpec((ATT_TILE, NA_W), q_map),
        out_shape=jax.ShapeDtypeStruct((n_tot if with_ctx else dims["n_lat"], NA_W), BF16),
        compiler_params=_params(("parallel", "arbitrary")),
        name="neighbourhood_attention",
    )(nq, nk, nv, nk, nv, bias_tabs)


def _route_top2(logits):
    lane = lax.broadcasted_iota(jnp.int32, logits.shape, 1)
    lane_f = lane.astype(F32)
    lg = jnp.where(lane < N_EXPERTS, logits, -jnp.inf)
    m1 = jnp.max(lg, axis=-1, keepdims=True)
    i1 = jnp.min(jnp.where(lg == m1, lane_f, float(LANES)), axis=-1, keepdims=True)
    lg2 = jnp.where(lane_f == i1, -jnp.inf, lg)
    m2 = jnp.max(lg2, axis=-1, keepdims=True)
    i2 = jnp.min(jnp.where(lg2 == m2, lane_f, float(LANES)), axis=-1, keepdims=True)
    e = jnp.exp(m2 - m1)
    w1 = 1.0 / (1.0 + e)
    w2 = e / (1.0 + e)
    return jnp.where(lane == 0, i1, jnp.where(lane == 1, i2, jnp.where(lane == 2, w1,
                                                                         jnp.where(lane == 3, w2, 0.0))))


def _outproj_kernel(moe, x_ref, of_ref, ob_ref, gate_ref, bd_ref, cn_ref, mod_ref, ggla_ref, g2_ref, w_ref,
                    *rest):
    if moe:
        wr_ref, x_out_ref, h_ref, route_ref = rest
    else:
        x_out_ref, h_ref = rest
    m = mod_ref[0, 0]
    o = of_ref[...] + ob_ref[...]
    avg = (lax.broadcasted_iota(jnp.int32, (GLA_V, GLA_V), 0) // GLA_DV
           == lax.broadcasted_iota(jnp.int32, (GLA_V, GLA_V), 1) // GLA_DV).astype(F32) * (1.0 / GLA_DV)
    ms = jnp.dot(o * o, avg, precision=HIGHEST, preferred_element_type=F32)
    a = o * lax.rsqrt(ms + RMS_EPS) * ggla_ref[...] * _silu(gate_ref[...].astype(F32))
    y = (_dot(a.astype(BF16), w_ref[0:GLA_V]) + _dot(bd_ref[...], w_ref[GLA_V:GLA_V + DIFF_W])
         + _dot(cn_ref[...], w_ref[GLA_V + DIFF_W:]))
    x = x_ref[...] + m[2:3] * y
    x_out_ref[...] = x
    h = x * lax.rsqrt(jnp.mean(x * x, axis=-1, keepdims=True) + RMS_EPS) * g2_ref[...]
    h = h * (1.0 + m[4:5]) + m[3:4]
    h_ref[...] = h.astype(h_ref.dtype)
    if moe:
        h_hi, w = h.astype(BF16), wr_ref[...]
        h_lo = (h - h_hi.astype(F32)).astype(BF16)
        w_hi = w.astype(BF16)
        w_lo = (w - w_hi.astype(F32)).astype(BF16)
        route_ref[...] = _route_top2(_dot(h_hi, w_hi) + (_dot(h_hi, w_lo) + _dot(h_lo, w_hi)))


def _output_projection(x_all, o_f, o_b, gla, b_diff, c_na, mod, layer, g_gla, g2, w_out, w_router, n_rows, dims):
    d = x_all.shape[1]
    n_lat_tiles = dims["n_lat"] // TOK_TILE
    tiles_per_batch = dims["S"] // TOK_TILE
    nb = dims["B"]
    moe = w_router is not None

    def mod_map(i):
        return (layer, jnp.where(i < n_lat_tiles, i // tiles_per_batch, nb), 0, 0)

    row = lambda w, cb=0: pl.BlockSpec((TOK_TILE, w), lambda i: (i, cb))
    const = lambda shape: pl.BlockSpec(shape, lambda i: (0,) * len(shape))
    in_specs = [row(d), row(GLA_V), row(GLA_V), row(GLA_V, (2 * GLA_QK + GLA_V) // GLA_V), row(DIFF_W), row(NA_W),
                pl.BlockSpec((1, 1, 6, d), mod_map), const((1, GLA_V)), const((1, d)), const((d, d))]
    args = [x_all, o_f, o_b, gla, b_diff, c_na, mod, g_gla, g2, w_out]
    out_specs = [row(d), row(d)]
    out_shape = [jax.ShapeDtypeStruct((n_rows, d), F32), jax.ShapeDtypeStruct((n_rows, d), F32 if moe else BF16)]
    if moe:
        in_specs.append(const((d, LANES)))
        args.append(w_router)
        out_specs.append(row(LANES))
        out_shape.append(jax.ShapeDtypeStruct((n_rows, LANES), F32))
    return pl.pallas_call(
        functools.partial(_outproj_kernel, moe),
        grid=(n_rows // TOK_TILE,),
        in_specs=in_specs, out_specs=out_specs, out_shape=out_shape,
        compiler_params=_params(("parallel",)),
        name="output_projection",
    )(*args)


def _ffn_kernel(h_ref, x_ref, mod_ref, wg_ref, wu_ref, wd_ref, o_ref, acc_ref):
    h = h_ref[...]
    n_chunks = wg_ref.shape[1] // FFN_CHUNK
    for j in range(n_chunks):
        cols = slice(j * FFN_CHUNK, (j + 1) * FFN_CHUNK)
        act = (_silu(_dot(h, wg_ref[:, cols])) * _dot(h, wu_ref[:, cols])).astype(BF16)
        part = _dot(act, wd_ref[cols, :])
        if j == 0:
            acc_ref[...] = part
        else:
            acc_ref[...] += part
    o_ref[...] = x_ref[...] + mod_ref[0, 0][5:6] * acc_ref[...]


def _dense_ffn(h, x_all, mod, layer, wg, wu, wd, dims):
    n_rows, d = x_all.shape
    f = wg.shape[1]
    n_lat_tiles = dims["n_lat"] // TOK_TILE
    tiles_per_batch = dims["S"] // TOK_TILE
    nb = dims["B"]

    def mod_map(i):
        return (layer, jnp.where(i < n_lat_tiles, i // tiles_per_batch, nb), 0, 0)

    row = lambda: pl.BlockSpec((TOK_TILE, d), lambda i: (i, 0))
    const = lambda shape: pl.BlockSpec(shape, lambda i: (0, 0))
    return pl.pallas_call(
        _ffn_kernel,
        grid=(n_rows // TOK_TILE,),
        in_specs=[row(), row(), pl.BlockSpec((1, 1, 6, d), mod_map), const((d, f)), const((d, f)), const((f, d))],
        out_specs=row(),
        out_shape=jax.ShapeDtypeStruct((n_rows, d), F32),
        scratch_shapes=[pltpu.VMEM((TOK_TILE, d), F32)],
        compiler_params=_params(("parallel",)),
        name="dense_swiglu",
    )(h, x_all, mod, wg, wu, wd)


def _moe_routing(route, n_tok):
    e_flat = jnp.concatenate([route[:, 0], route[:, 1]]).astype(jnp.int32)
    onehot = (e_flat[:, None] == jnp.arange(N_EXPERTS, dtype=jnp.int32)[None, :]).astype(jnp.int32)
    csum = jnp.cumsum(onehot, axis=0)
    counts = csum[-1]
    padded = ((counts + MOE_TILE - 1) // MOE_TILE) * MOE_TILE
    ends = jnp.cumsum(padded)
    pos = jnp.sum(onehot * (csum - 1 + (ends - padded)[None, :]), axis=1)
    n_rows = 2 * n_tok + N_EXPERTS * MOE_TILE
    tile_start = jnp.arange(n_rows // MOE_TILE, dtype=jnp.int32) * MOE_TILE
    tile_expert = jnp.minimum(jnp.sum((tile_start[:, None] >= ends[None, :]).astype(jnp.int32), axis=1),
                              N_EXPERTS - 1).astype(jnp.int32)
    n_active = (ends[-1:] // MOE_TILE).astype(jnp.int32)
    n_tiles = n_tok // TOK_TILE
    pos = jnp.concatenate([pos[:n_tok].reshape(n_tiles, 1, TOK_TILE), pos[n_tok:].reshape(n_tiles, 1, TOK_TILE)],
                          axis=2)
    return pos, n_rows, tile_expert, n_active


def _dispatch_kernel(pos_ref, h_ref, xs_in, xs_out, sem):
    del xs_in
    tile = h_ref.shape[0]

    def issue(r, carry):
        for k in range(2):
            pltpu.make_async_copy(h_ref.at[pl.ds(r, 1)], xs_out.at[pl.ds(pos_ref[0, 0, k * tile + r], 1)],
                                  sem).start()
        return carry
    lax.fori_loop(0, tile, issue, 0, unroll=8)
    for k in range(2):
        pltpu.make_async_copy(h_ref, xs_out.at[pl.ds(0, tile)], sem).wait()


def _moe_dispatch(h, pos, n_rows):
    n_tok, d = h.shape
    return pl.pallas_call(
        _dispatch_kernel,
        grid=(n_tok // TOK_TILE,),
        in_specs=[pl.BlockSpec((1, 1, 2 * TOK_TILE), lambda i: (i, 0, 0), memory_space=pltpu.SMEM),
                  pl.BlockSpec((TOK_TILE, d), lambda i: (i, 0)),
                  pl.BlockSpec(memory_space=pl.ANY)],
        out_specs=pl.BlockSpec(memory_space=pl.ANY),
        out_shape=jax.ShapeDtypeStruct((n_rows, d), h.dtype),
        scratch_shapes=[pltpu.SemaphoreType.DMA(())],
        input_output_aliases={2: 0},
        compiler_params=_params(("arbitrary",)),
        name="expert_dispatch",
    )(pos, h, jnp.zeros((n_rows, d), h.dtype))


def _moe_kernel(te_ref, nact_ref, x_ref, wg_ref, wu_ref, wd_ref, o_ref, xb_ref, acc_ref):
    i, f = pl.program_id(0), pl.program_id(1)
    active = i < nact_ref[0]
    last = f == pl.num_programs(1) - 1

    @pl.when(f == 0)
    def _():
        xb_ref[...] = x_ref[...].astype(BF16)
        acc_ref[...] = jnp.zeros_like(acc_ref)

    @pl.when(active)
    def _():
        x = xb_ref[...]
        for j in range(wg_ref.shape[2] // FFN_CHUNK):
            cols = slice(j * FFN_CHUNK, (j + 1) * FFN_CHUNK)
            act = (_silu(_dot(x, wg_ref[0, :, cols])) * _dot(x, wu_ref[0, :, cols])).astype(BF16)
            acc_ref[...] += _dot(act, wd_ref[0, cols, :])

    @pl.when(last)
    def _():
        o_ref[...] = acc_ref[...]


def _moe_ffn(x_sorted, tile_expert, n_active, wg, wu, wd):
    n_rows, d = x_sorted.shape
    ffn = wg.shape[2]
    n_tiles = n_rows // MOE_TILE
    n_f = ffn // MOE_FCHUNK

    def row(i, f, te, na):
        return jnp.minimum(i, na[0] - 1)

    def fcol(i, f, te, na):
        return jnp.where(i < na[0], f, n_f - 1)

    grid_spec = pltpu.PrefetchScalarGridSpec(
        num_scalar_prefetch=2,
        grid=(n_tiles, n_f),
        in_specs=[pl.BlockSpec((MOE_TILE, d), lambda i, f, te, na: (row(i, f, te, na), 0)),
                  pl.BlockSpec((1, d, MOE_FCHUNK), lambda i, f, te, na: (te[i], 0, fcol(i, f, te, na))),
                  pl.BlockSpec((1, d, MOE_FCHUNK), lambda i, f, te, na: (te[i], 0, fcol(i, f, te, na))),
                  pl.BlockSpec((1, MOE_FCHUNK, d), lambda i, f, te, na: (te[i], fcol(i, f, te, na), 0))],
        out_specs=pl.BlockSpec((MOE_TILE, d), lambda i, f, te, na: (i, 0)),
        scratch_shapes=[pltpu.VMEM((MOE_TILE, d), BF16), pltpu.VMEM((MOE_TILE, d), F32)])
    return pl.pallas_call(
        _moe_kernel,
        grid_spec=grid_spec,
        out_shape=jax.ShapeDtypeStruct((n_rows, d), F32),
        compiler_params=_params(("arbitrary", "arbitrary")),
        name="expert_swiglu",
    )(tile_expert, n_active, x_sorted, wg, wu, wd)


def _final_kernel(pos_ref, x_ref, route_ref, mod_ref, g_ref, y_hbm, o_ref, buf_ref, sem):
    tile = x_ref.shape[0]

    def issue(r, carry):
        for k in range(2):
            pltpu.make_async_copy(y_hbm.at[pl.ds(pos_ref[0, 0, k * tile + r], 1)],
                                  buf_ref.at[k, pl.ds(r, 1)], sem).start()
        return carry
    lax.fori_loop(0, tile, issue, 0, unroll=8)
    for k in range(2):
        pltpu.make_async_copy(y_hbm.at[pl.ds(0, tile)], buf_ref.at[k], sem).wait()
    route = route_ref[...]
    y = route[:, 2:3] * buf_ref[0] + route[:, 3:4] * buf_ref[1]
    x = x_ref[...] + mod_ref[0, 0][5:6] * y
    o_ref[...] = x * lax.rsqrt(jnp.mean(x * x, axis=-1, keepdims=True) + RMS_EPS) * g_ref[...]


def _final_combine(x_lat, y_sorted, pos, route, mod, layer, g_final, dims):
    n_rows, d = x_lat.shape
    n_tiles = n_rows // TOK_TILE
    tiles_per_batch = dims["S"] // TOK_TILE
    return pl.pallas_call(
        _final_kernel,
        grid=(n_tiles,),
        in_specs=[pl.BlockSpec((1, 1, 2 * TOK_TILE), lambda i: (i, 0, 0), memory_space=pltpu.SMEM),
                  pl.BlockSpec((TOK_TILE, d), lambda i: (i, 0)),
                  pl.BlockSpec((TOK_TILE, LANES), lambda i: (i, 0)),
                  pl.BlockSpec((1, 1, 6, d), lambda i: (layer, i // tiles_per_batch, 0, 0)),
                  pl.BlockSpec((1, d), lambda i: (0, 0)),
                  pl.BlockSpec(memory_space=pl.ANY)],
        out_specs=pl.BlockSpec((TOK_TILE, d), lambda i: (i, 0)),
        out_shape=jax.ShapeDtypeStruct((n_rows, d), F32),
        scratch_shapes=[pltpu.VMEM((2, TOK_TILE, d), F32), pltpu.SemaphoreType.DMA(())],
        compiler_params=_params(("arbitrary",)),
        name="expert_combine_final_norm",
    )(pos, x_lat, route, mod, g_final, y_sorted)


def _rope_tables(s_len):
    t = np.arange(s_len)
    lane = np.arange(LANES) % HEAD_DIM
    quarter = HEAD_DIM // 4
    inv = 1.0 / (ROPE_BASE ** (jnp.arange(quarter, dtype=F32) / quarter))
    pos = np.where(lane[None, :] < HEAD_DIM // 2, (t // GRID_W)[:, None], (t % GRID_W)[:, None]).astype(np.float32)
    ang = jnp.asarray(pos) * inv[lane % quarter][None, :]
    first = jnp.asarray((lane % (HEAD_DIM // 2)) < quarter)[None, :]
    cos, sin = jnp.cos(ang), jnp.sin(ang)
    ident = jnp.zeros((TOK_TILE, LANES), F32)
    return (jnp.concatenate([cos, ident + 1.0]),
            jnp.concatenate([jnp.where(first, -sin, 0.0), ident]),
            jnp.concatenate([jnp.where(first, 0.0, sin), ident]))


def _rearranged_w_in(w):
    offs = np.concatenate([[0], np.cumsum(IN_SIZES)])
    part = lambda j: w[:, int(offs[j]):int(offs[j + 1])]
    pad = jnp.zeros((w.shape[0], LANES - 2 * GLA_LR), w.dtype)
    order = [0, 1, 2, 3, 6, 7, 9, 10, 11, 4, 5]
    w_cat = jnp.concatenate([part(j) for j in order] + [pad], axis=1).astype(BF16)
    return w_cat, jnp.transpose(part(8)).astype(BF16)


def _decay_up(w_up, b):
    top = jnp.concatenate([w_up[0], jnp.zeros_like(w_up[0])], axis=1)
    bot = jnp.concatenate([jnp.zeros_like(w_up[1]), w_up[1]], axis=1)
    pad = jnp.zeros((LANES - 2 * GLA_LR, 2 * GLA_QK), w_up.dtype)
    return jnp.concatenate([top, bot, pad], axis=0).astype(BF16), jnp.concatenate([b[0], b[1]])[None, :]


def kernel(x, c, ctx, c_ctx, w_mod, b_mod, g_norm1, g_norm2, w_in, gla_w_dec_up, gla_b_dec, gla_g_norm,
           diff_lambda, diff_g_norm, na_rpb, w_out, w_ffn_gate, w_ffn_up, w_ffn_down, w_router, w_moe_gate,
           w_moe_up, w_moe_down, g_final):
    nb, s_len, d = x.shape
    c_len = ctx.shape[1]
    depth = w_mod.shape[0]
    assert c_len == ATT_TILE and s_len % TOK_TILE == 0 and (nb * c_len) % TOK_TILE == 0 and nb < MOD_ROWS
    n_lat = nb * s_len
    dims = {"B": nb, "S": s_len, "n_lat": n_lat}

    x_all = jnp.concatenate([x.reshape(n_lat, d), ctx.reshape(nb * c_len, d)], axis=0)
    cc = jnp.concatenate([c, c_ctx[None, :], jnp.zeros((MOD_ROWS - nb - 1, d), F32)], axis=0)
    mod = _modulation(cc, w_mod, b_mod).reshape(depth, MOD_ROWS, 6, d)
    rope_tabs = _rope_tables(s_len)

    for l in range(depth):
        last = l == depth - 1
        w_cat, w_dvt = _rearranged_w_in(w_in[l])
        wup2, bdec2 = _decay_up(gla_w_dec_up[l], gla_b_dec[l])
        gla, lg, dq, dk, dvt, nq, nk, nv = _input_projection(
            x_all, mod, l, g_norm1[l][None, :], w_cat, w_dvt, wup2, bdec2, rope_tabs, dims)
        o_f, o_b = _gla(gla, lg, dims)
        b_diff = _diff_attention(dq, dk, dvt, diff_lambda[l], diff_g_norm[l][None, :], l, not last, dims)
        c_na = _neighbourhood_attention(nq, nk, nv, _na_bias_tables(na_rpb[l], s_len // GRID_W), not last, dims)
        g_gla = jnp.tile(gla_g_norm[l], GLA_HEADS)[None, :]
        n_rows = n_lat if last else x_all.shape[0]
        if l % 2 == 0:
            i = l // 2
            x_mid, h = _output_projection(x_all, o_f, o_b, gla, b_diff, c_na, mod, l, g_gla, g_norm2[l][None, :],
                                          w_out[l].astype(BF16), None, n_rows, dims)
            assert not last, "the dense channel mixer is implemented for non-final layers only"
            x_all = _dense_ffn(h, x_mid, mod, l, w_ffn_gate[i].astype(BF16), w_ffn_up[i].astype(BF16),
                               w_ffn_down[i].astype(BF16), dims)
        else:
            i = l // 2
            w_r = jnp.concatenate([w_router[i], jnp.zeros((d, LANES - N_EXPERTS), F32)], axis=1)
            x_mid, h, route = _output_projection(x_all, o_f, o_b, gla, b_diff, c_na, mod, l, g_gla,
                                                 g_norm2[l][None, :], w_out[l].astype(BF16), w_r, n_rows, dims)
            assert last, "the expert layer is implemented for the final layer only"
            pos, n_sorted, tile_expert, n_active = _moe_routing(route, n_rows)
            y_sorted = _moe_ffn(_moe_dispatch(h, pos, n_sorted), tile_expert, n_active, w_moe_gate[i].astype(BF16),
                                w_moe_up[i].astype(BF16), w_moe_down[i].astype(BF16))
            return _final_combine(x_mid, y_sorted, pos, route, mod, l, g_final[None, :], dims).reshape(nb, s_len, d)
    raise NotImplementedError("the final layer must be an expert layer")
```

```python
import functools
import math

import numpy as np
import jax
import jax.numpy as jnp
from jax import lax
from jax.experimental import pallas as pl
from jax.experimental.pallas import tpu as pltpu

F32 = jnp.float32
BF16 = jnp.bfloat16
HIGHEST = lax.Precision.HIGHEST

GRID_W = 64
HEAD_DIM = 64
ROPE_BASE = 10000.0
RMS_EPS = 1e-6
GLA_HEADS = 4
GLA_DK = 32
GLA_DV = 64
GLA_LR = 16
GLA_NORMALIZER = 16.0
GLA_CHUNK = 64
DIFF_HEADS = 4
DIFF_HD = 64
NA_HEADS = 4
NA_HD = 64
WIN_R = 8
WIN_C = 16
N_EXPERTS = 8
GLA_QK = GLA_HEADS * GLA_DK
GLA_V = GLA_HEADS * GLA_DV
DIFF_W = DIFF_HEADS * 2 * DIFF_HD
NA_W = NA_HEADS * NA_HD
IN_SIZES = (GLA_QK, GLA_QK, GLA_V, GLA_V, GLA_LR, GLA_LR, DIFF_W, DIFF_W, DIFF_W, NA_W, NA_W, NA_W)

LANES = 128
TOK_TILE = 512
ATT_TILE = 256
DIFF_QT = 512
NA_QROWS = 4
NA_KROWS = 12
FFN_CHUNK = 256
MOE_TILE = 512
MOE_FCHUNK = 1792
MOD_ROWS = 16
NEG_BIG = -1e30
LOG2E = math.log2(math.e)
VMEM_LIMIT = 56 * 1024 * 1024

C_GLA = 0
C_DQ = C_GLA + 2 * GLA_QK + 2 * GLA_V
C_DK = C_DQ + DIFF_W
C_NQ = C_DK + DIFF_W
C_NK = C_NQ + NA_W
C_DEC = C_NK + NA_W
C_END = C_DEC + LANES


def _silu(x):
    return x * (1.0 / (1.0 + jnp.exp(-x)))


def _dot(a, b):
    return jnp.dot(a, b, preferred_element_type=F32)


def _dot_nt(a, b):
    return lax.dot_general(a, b, (((1,), (1,)), ((), ())), preferred_element_type=F32)


def _dot_tn(a, b):
    return lax.dot_general(a, b, (((0,), (0,)), ((), ())), preferred_element_type=F32)


def _params(sem, vmem=VMEM_LIMIT):
    return pltpu.CompilerParams(dimension_semantics=sem, vmem_limit_bytes=vmem)


def _mod_kernel(cc_ref, w_ref, b_ref, o_ref):
    s = _silu(cc_ref[...]).astype(BF16)
    o_ref[0] = _dot(s, w_ref[0].astype(BF16)) + b_ref[0]


def _modulation(cc, w_mod, b_mod):
    depth, d, six_d = w_mod.shape
    n = six_d // d
    return pl.pallas_call(
        _mod_kernel,
        grid=(depth, n),
        in_specs=[pl.BlockSpec((MOD_ROWS, d), lambda l, j: (0, 0)),
                  pl.BlockSpec((1, d, d), lambda l, j: (l, 0, j)),
                  pl.BlockSpec((1, 1, d), lambda l, j: (l, 0, j))],
        out_specs=pl.BlockSpec((1, MOD_ROWS, d), lambda l, j: (l, 0, j)),
        out_shape=jax.ShapeDtypeStruct((depth, MOD_ROWS, six_d), F32),
        compiler_params=_params(("parallel", "parallel")),
        name="modulation",
    )(cc, w_mod, b_mod.reshape(depth, 1, six_d))


def _rope(z, a, bt, ct):
    outs = []
    for s in range(z.shape[1] // LANES):
        zs = z[:, s * LANES:(s + 1) * LANES]
        outs.append(zs * a + pltpu.roll(zs, LANES - 16, 1) * bt + pltpu.roll(zs, 16, 1) * ct)
    return jnp.concatenate(outs, axis=1)


def _inproj_kernel(x_ref, mod_ref, g_ref, w_ref, wvt_ref, wnt_ref, wup_ref, bdec_ref, ra_ref, rb_ref, rc_ref,
                   gla_ref, lg_ref, dq_ref, dk_ref, dvt_ref, nq_ref, nk_ref, nvt_ref):
    x = x_ref[...]
    m = mod_ref[0, 0]
    h = x * lax.rsqrt(jnp.mean(x * x, axis=-1, keepdims=True) + RMS_EPS) * g_ref[...]
    hb = (h * (1.0 + m[1:2]) + m[0:1]).astype(BF16)

    def proj(c0, c1):
        return _dot(hb, w_ref[:, c0:c1])

    gla_ref[...] = proj(C_GLA, C_DQ).astype(BF16)
    a, bt, ct = ra_ref[...], rb_ref[...], rc_ref[...]
    dq_ref[...] = (_rope(proj(C_DQ, C_DK), a, bt, ct) * (DIFF_HD ** -0.5 * LOG2E)).astype(BF16)
    dk_ref[...] = _rope(proj(C_DK, C_NQ), a, bt, ct).astype(BF16)
    dvt_ref[0] = _dot_nt(wvt_ref[...], hb).astype(BF16)
    nq_ref[...] = (proj(C_NQ, C_NK) * (NA_HD ** -0.5 * LOG2E)).astype(BF16)
    nk_ref[...] = proj(C_NK, C_DEC).astype(BF16)
    for j in range(nvt_ref.shape[0]):
        nvt_ref[j] = _dot_nt(wnt_ref[...], hb[j * ATT_TILE:(j + 1) * ATT_TILE]).astype(BF16)
    logits = _dot(proj(C_DEC, C_END).astype(BF16), wup_ref[...]) + bdec_ref[...]
    log_sig = jnp.minimum(logits, 0.0) - jnp.log1p(jnp.exp(-jnp.abs(logits)))
    lg_ref[...] = log_sig * (1.0 / GLA_NORMALIZER)


def _input_projection(x_all, mod, layer, g1, w_cat, w_dvt, w_nvt, wup2, bdec2, rope_tabs, dims):
    n_tot, d = x_all.shape
    n_lat_tiles = dims["n_lat"] // TOK_TILE
    tiles_per_batch = dims["S"] // TOK_TILE
    n_tiles = n_tot // TOK_TILE
    nb = dims["B"]

    def mod_map(i):
        return (layer, jnp.where(i < n_lat_tiles, i // tiles_per_batch, nb), 0, 0)

    def rope_map(i):
        return (jnp.where(i < n_lat_tiles, i % tiles_per_batch, tiles_per_batch), 0)

    row = lambda w: pl.BlockSpec((TOK_TILE, w), lambda i: (i, 0))
    const = lambda shape: pl.BlockSpec(shape, lambda i: (0,) * len(shape))
    tab = pl.BlockSpec((TOK_TILE, LANES), rope_map)
    sub = TOK_TILE // ATT_TILE
    out_specs = [row(C_DQ), row(2 * GLA_QK), row(DIFF_W), row(DIFF_W),
                 pl.BlockSpec((1, DIFF_W, TOK_TILE), lambda i: (i, 0, 0)), row(NA_W), row(NA_W),
                 pl.BlockSpec((sub, NA_W, ATT_TILE), lambda i: (i, 0, 0))]
    shape = lambda w, dt=BF16: jax.ShapeDtypeStruct((n_tot, w), dt)
    out_shape = [shape(C_DQ), shape(2 * GLA_QK, F32), shape(DIFF_W), shape(DIFF_W),
                 jax.ShapeDtypeStruct((n_tiles, DIFF_W, TOK_TILE), BF16), shape(NA_W), shape(NA_W),
                 jax.ShapeDtypeStruct((n_tiles * sub, NA_W, ATT_TILE), BF16)]
    return pl.pallas_call(
        _inproj_kernel,
        grid=(n_tiles,),
        in_specs=[row(d), pl.BlockSpec((1, 1, 6, d), mod_map), const((1, d)), const((d, C_END)), const((DIFF_W, d)),
                  const((NA_W, d)), const((LANES, 2 * GLA_QK)), const((1, 2 * GLA_QK)), tab, tab, tab],
        out_specs=out_specs,
        out_shape=out_shape,
        compiler_params=_params(("parallel",)),
        name="input_projection",
    )(x_all, mod, g1, w_cat, w_dvt, w_nvt, wup2, bdec2, *rope_tabs)


def _gla_chunk_local(g_ref, l_ref, c, fwd, consts):
    tri, tri4, head_k, head_v, bd = consts
    rows = slice(c * GLA_CHUNK, (c + 1) * GLA_CHUNK)
    q = g_ref[rows, 0:GLA_QK].astype(F32) * GLA_DK ** -0.5
    k = g_ref[rows, GLA_QK:2 * GLA_QK].astype(F32)
    v = g_ref[rows, 2 * GLA_QK:2 * GLA_QK + GLA_V]
    lcol = 0 if fwd else GLA_QK
    lg = l_ref[rows, lcol:lcol + GLA_QK]
    lg_hi = lg.astype(BF16)
    lg_lo = (lg - lg_hi.astype(F32)).astype(BF16)
    b = _dot(tri, lg_hi) + _dot(tri, lg_lo)
    b_end = b[GLA_CHUNK - 1:GLA_CHUNK] if fwd else b[0:1]
    q_dec = q * jnp.exp(b)
    k_inv = (k * jnp.exp(-b)).astype(BF16)
    k_end = (k * jnp.exp(b_end - b)).astype(BF16)
    qs = jnp.concatenate([jnp.where(head_k == h, q_dec, 0.0) for h in range(GLA_HEADS)], axis=0)
    att = jnp.where(tri4, _dot_nt(qs.astype(BF16), k_inv), 0.0)
    ov = _dot(att.astype(BF16), v)
    o = jnp.where(head_v == 0, ov[0:GLA_CHUNK], 0.0)
    for h in range(1, GLA_HEADS):
        o = o + jnp.where(head_v == h, ov[h * GLA_CHUNK:(h + 1) * GLA_CHUNK], 0.0)
    return o, q_dec.astype(BF16), jnp.where(bd, _dot_tn(v, k_end), 0.0), jnp.exp(b_end)


def _gla_kernel(gf_ref, lf_ref, gb_ref, lb_ref, of_ref, ob_ref, sf_ref, sb_ref):
    @pl.when(pl.program_id(1) == 0)
    def _():
        sf_ref[...] = jnp.zeros_like(sf_ref)
        sb_ref[...] = jnp.zeros_like(sb_ref)

    r = lax.broadcasted_iota(jnp.int32, (GLA_CHUNK, GLA_CHUNK), 0)
    c = lax.broadcasted_iota(jnp.int32, (GLA_CHUNK, GLA_CHUNK), 1)
    r4 = lax.broadcasted_iota(jnp.int32, (GLA_HEADS * GLA_CHUNK, GLA_CHUNK), 0) % GLA_CHUNK
    c4 = lax.broadcasted_iota(jnp.int32, (GLA_HEADS * GLA_CHUNK, GLA_CHUNK), 1)
    head_k = lax.broadcasted_iota(jnp.int32, (1, GLA_QK), 1) // GLA_DK
    head_v = lax.broadcasted_iota(jnp.int32, (1, GLA_V), 1) // GLA_DV
    bd = (lax.broadcasted_iota(jnp.int32, (GLA_V, GLA_QK), 0) // GLA_DV
          == lax.broadcasted_iota(jnp.int32, (GLA_V, GLA_QK), 1) // GLA_DK)
    lower = ((c <= r).astype(BF16), c4 <= r4, head_k, head_v, bd)
    upper = ((c >= r).astype(BF16), c4 >= r4, head_k, head_v, bd)
    n_chunks = ATT_TILE // GLA_CHUNK
    scans = ((gf_ref, lf_ref, of_ref, sf_ref, True, lower, list(range(n_chunks))),
             (gb_ref, lb_ref, ob_ref, sb_ref, False, upper, list(range(n_chunks - 1, -1, -1))))
    local = [[_gla_chunk_local(g_ref, l_ref, cidx, fwd, consts) for cidx in order]
             for g_ref, l_ref, _, _, fwd, consts, order in scans]
    for (_, _, o_ref, s_ref, _, _, order), chunks in zip(scans, local):
        s = s_ref[...]
        for cidx, (o_intra, q_dec, inc, decay) in zip(order, chunks):
            o_ref[cidx * GLA_CHUNK:(cidx + 1) * GLA_CHUNK, :] = o_intra + _dot_nt(q_dec, s.astype(BF16))
            s = decay * s + inc
        s_ref[...] = s


def _gla(gla, lg, dims):
    n_tot = gla.shape[0]
    nb, steps = dims["B"], 1 + dims["S"] // ATT_TILE
    lat_blocks = dims["n_lat"] // ATT_TILE
    per_batch = dims["S"] // ATT_TILE

    def fwd_map(b, i):
        return (jnp.where(i == 0, lat_blocks + b, b * per_batch + i - 1), 0)

    def bwd_map(b, i):
        return (jnp.where(i == 0, lat_blocks + b, b * per_batch + per_batch - i), 0)

    return pl.pallas_call(
        _gla_kernel,
        grid=(nb, steps),
        in_specs=[pl.BlockSpec((ATT_TILE, C_DQ), fwd_map), pl.BlockSpec((ATT_TILE, 2 * GLA_QK), fwd_map),
                  pl.BlockSpec((ATT_TILE, C_DQ), bwd_map), pl.BlockSpec((ATT_TILE, 2 * GLA_QK), bwd_map)],
        out_specs=[pl.BlockSpec((ATT_TILE, GLA_V), fwd_map), pl.BlockSpec((ATT_TILE, GLA_V), bwd_map)],
        out_shape=[jax.ShapeDtypeStruct((n_tot, GLA_V), F32)] * 2,
        scratch_shapes=[pltpu.VMEM((GLA_V, GLA_QK), F32), pltpu.VMEM((GLA_V, GLA_QK), F32)],
        compiler_params=_params(("parallel", "arbitrary")),
        name="gla_scan",
    )(gla, lg, gla, lg)


def _diff_kernel(lam_init, latent, *refs):
    if latent:
        q_ref, kl_ref, vl_ref, kc_ref, vc_ref, lam_ref, g_ref, o_ref, *s_refs = refs
    else:
        q_ref, kc_ref, vc_ref, lam_ref, g_ref, o_ref, *s_refs = refs
    q = q_ref[...]
    t, hw = q.shape
    c_len = kc_ref.shape[0]
    lane = lax.broadcasted_iota(jnp.int32, q.shape, 1)
    zero = jnp.zeros_like(q)
    qm = (jnp.where(lane < DIFF_HD, q, zero), jnp.where(lane >= DIFF_HD, q, zero))

    tiles = [(lambda: kc_ref[...], lambda: vc_ref[0], 0, c_len)]
    if latent:
        tiles += [(lambda j=j: kl_ref[j * TOK_TILE:(j + 1) * TOK_TILE, :], lambda j=j: vl_ref[j],
                   c_len + j * TOK_TILE, TOK_TILE) for j in range(vl_ref.shape[0])]

    gw = min(t, ATT_TILE)
    groups = [(mp, c0) for mp in range(2) for c0 in range(0, t, gw)]

    def pass1(g, tile, m):
        mp, c0 = groups[g]
        load_k, _, off, n = tile
        s = _dot_nt(load_k(), qm[mp][c0:c0 + gw])
        s_refs[g][off:off + n, :] = s
        return jnp.maximum(m, jnp.max(s, axis=0, keepdims=True))

    def pass2(g, tile, m, l, acc):
        _, load_vt, off, n = tile
        p = jnp.exp2(s_refs[g][off:off + n, :] - m)
        return l + jnp.sum(p, axis=0, keepdims=True), acc + _dot(load_vt(), p.astype(BF16))

    m_prev, outs = None, []
    for g in range(len(groups) + 1):
        m = jnp.full((1, gw), -jnp.inf, F32)
        l, acc = jnp.zeros((1, gw), F32), jnp.zeros((hw, gw), F32)
        for tile in tiles:
            if g < len(groups):
                m = pass1(g, tile, m)
            if g > 0:
                l, acc = pass2(g - 1, tile, m_prev, l, acc)
        if g > 0:
            outs.append(acc / l)
        m_prev = m
    n_half = len(groups) // 2
    on = [jnp.concatenate(outs[i * n_half:(i + 1) * n_half], axis=1) for i in range(2)]

    lam = lam_ref[...]
    lam_full = (jnp.exp(jnp.sum(lam[0:1] * lam[1:2], axis=-1, keepdims=True))
                - jnp.exp(jnp.sum(lam[2:3] * lam[3:4], axis=-1, keepdims=True)) + lam_init)
    o = jnp.transpose(on[0] - lam_full * on[1])
    o = o * lax.rsqrt(jnp.mean(o * o, axis=-1, keepdims=True) + RMS_EPS) * g_ref[...]
    o_ref[...] = (o * (1.0 - lam_init)).astype(BF16)


def _diff_attention(dq, dk, dvt, lam, g_diff, layer, with_ctx, dims):
    n_tot = dq.shape[0]
    nb, s_len = dims["B"], dims["S"]
    n_lat_q = s_len // DIFF_QT
    lat_blocks = dims["n_lat"] // ATT_TILE
    lat_tiles = dims["n_lat"] // TOK_TILE
    ctx_per_tile = TOK_TILE // ATT_TILE
    lam_init = 0.8 - 0.6 * math.exp(-0.3 * layer)
    hw = 2 * DIFF_HD
    out_shape = jax.ShapeDtypeStruct((dims["n_lat"], DIFF_W), BF16)

    k_ctx = pl.BlockSpec((ATT_TILE, hw), lambda b, h, *_: (lat_blocks + b, h))
    v_ctx = pl.BlockSpec((1, hw, ATT_TILE), lambda b, h, *_: (lat_tiles + b // ctx_per_tile, h, b % ctx_per_tile))
    small = [pl.BlockSpec((4, DIFF_HD), lambda *_: (0, 0)), pl.BlockSpec((1, hw), lambda *_: (0, 0))]
    q_lat = pl.BlockSpec((DIFF_QT, hw), lambda b, h, i: (b * n_lat_q + i, h))
    out = pl.pallas_call(
        functools.partial(_diff_kernel, lam_init, True),
        grid=(nb, DIFF_HEADS, n_lat_q),
        in_specs=[q_lat, pl.BlockSpec((s_len, hw), lambda b, h, i: (b, h)),
                  pl.BlockSpec((s_len // TOK_TILE, hw, TOK_TILE), lambda b, h, i: (b, h, 0)), k_ctx, v_ctx] + small,
        out_specs=q_lat,
        out_shape=out_shape,
        scratch_shapes=[pltpu.VMEM((ATT_TILE + s_len, ATT_TILE), F32)] * (2 * DIFF_QT // ATT_TILE),
        compiler_params=_params(("parallel", "parallel", "arbitrary")),
        name="diff_attention",
    )(dq, dk, dvt, dk, dvt, lam, g_diff)
    if not with_ctx:
        return out, None
    out_ctx = pl.pallas_call(
        functools.partial(_diff_kernel, lam_init, False),
        grid=(nb, DIFF_HEADS),
        in_specs=[pl.BlockSpec((ATT_TILE, hw), lambda b, h: (lat_blocks + b, h)), k_ctx, v_ctx] + small,
        out_specs=pl.BlockSpec((ATT_TILE, hw), lambda b, h: (b, h)),
        out_shape=jax.ShapeDtypeStruct((nb * ATT_TILE, DIFF_W), BF16),
        scratch_shapes=[pltpu.VMEM((ATT_TILE, ATT_TILE), F32)] * 2,
        compiler_params=_params(("parallel", "parallel")),
        name="diff_attention_context",
    )(dq, dk, dvt, lam, g_diff)
    return out, out_ctx


def _na_bias_tables(rpb, rows):
    wr = min(WIN_R, rows)
    n_groups = rows // NA_QROWS
    qc = np.arange(GRID_W)[:, None]
    kc = np.arange(GRID_W)[None, :]
    cs = np.clip(qc - WIN_C // 2, 0, GRID_W - WIN_C)
    col_valid = (kc >= cs) & (kc < cs + WIN_C)
    onehot = ((kc - qc + WIN_C - 1)[None] == np.arange(2 * WIN_C - 1)[:, None, None]) & col_valid[None]
    toe = jnp.einsum("hrc,cqk->hrqk", rpb.astype(F32), jnp.asarray(onehot, F32), precision=HIGHEST)
    toe = jnp.where(col_valid[None, None], toe, NEG_BIG)
    tabs = []
    for g in (0, 1, n_groups - 1):
        r0 = g * NA_QROWS
        u0 = int(np.clip(r0 - WIN_R // 2, 0, rows - NA_KROWS))
        r = r0 + np.arange(NA_QROWS)[:, None]
        kr = u0 + np.arange(NA_KROWS)[None, :]
        rs = np.clip(r - WIN_R // 2, 0, rows - wr)
        row_valid = (kr >= rs) & (kr < rs + wr)
        rb = np.clip(kr - r + WIN_R - 1, 0, 2 * WIN_R - 2)
        t = jnp.where(row_valid[None, :, :, None, None], toe[:, rb], NEG_BIG)
        t = t.reshape(NA_HEADS, NA_QROWS, NA_KROWS // NA_QROWS, NA_QROWS, GRID_W, GRID_W)
        t = t.transpose(2, 3, 5, 0, 1, 4).reshape(NA_KROWS // NA_QROWS, ATT_TILE, NA_HEADS * ATT_TILE)
        tabs.append(t * LOG2E)
    return jnp.stack(tabs)


def _na_kernel(rows, latent, *refs):
    if latent:
        q_ref, kl_ref, vl_ref, kc_ref, vc_ref, bias_ref, o_ref, *s_refs = refs
    else:
        q_ref, kc_ref, vc_ref, o_ref, *s_refs = refs
    q = q_ref[...]
    t = q.shape[0]
    head = lax.broadcasted_iota(jnp.int32, q.shape, 1) // NA_HD
    zero = jnp.zeros_like(q)
    qs = jnp.concatenate([jnp.where(head == h, q, zero) for h in range(NA_HEADS)], axis=0)

    tiles = [(lambda: kc_ref[...], lambda: vc_ref[0], None)]
    if latent:
        t0 = jnp.clip(pl.program_id(1) - 1, 0, rows // NA_QROWS - NA_KROWS // NA_QROWS)
        for j in range(NA_KROWS // NA_QROWS):
            tiles.append((lambda j=j: kl_ref[pl.ds(pl.multiple_of((t0 + j) * t, t), t), :],
                          lambda j=j: vl_ref[t0 + j], lambda j=j: bias_ref[0, j]))

    m = jnp.full((1, NA_HEADS * t), -jnp.inf, F32)
    for i, (load_k, _, load_bias) in enumerate(tiles):
        s = _dot_nt(load_k(), qs)
        if load_bias is not None:
            s = s + load_bias()
        s_refs[i][...] = s
        m = jnp.maximum(m, jnp.max(s, axis=0, keepdims=True))
    l = jnp.zeros((1, NA_HEADS * t), F32)
    acc = [jnp.zeros((NA_HD, t), F32)] * NA_HEADS
    for i, (_, load_vt, _) in enumerate(tiles):
        p = jnp.exp2(s_refs[i][...] - m)
        l = l + jnp.sum(p, axis=0, keepdims=True)
        pb, vt = p.astype(BF16), load_vt()
        for h in range(NA_HEADS):
            acc[h] = acc[h] + _dot(vt[h * NA_HD:(h + 1) * NA_HD, :], pb[:, h * t:(h + 1) * t])
    o = jnp.concatenate([acc[h] / l[:, h * t:(h + 1) * t] for h in range(NA_HEADS)], axis=0)
    o_ref[...] = jnp.transpose(o).astype(BF16)


def _neighbourhood_attention(nq, nk, nvt, bias_tabs, with_ctx, dims):
    nb, s_len = dims["B"], dims["S"]
    rows = s_len // GRID_W
    n_groups = rows // NA_QROWS
    lat_blocks = dims["n_lat"] // ATT_TILE
    n_key_tiles = 1 + NA_KROWS // NA_QROWS

    def bias_map(b, g):
        return (jnp.where(g == 0, 0, jnp.where(g >= n_groups - 1, 2, 1)), 0, 0, 0)

    k_ctx = pl.BlockSpec((ATT_TILE, NA_W), lambda b, *_: (lat_blocks + b, 0))
    v_ctx = pl.BlockSpec((1, NA_W, ATT_TILE), lambda b, *_: (lat_blocks + b, 0, 0))
    q_lat = pl.BlockSpec((ATT_TILE, NA_W), lambda b, g: (b * n_groups + g, 0))
    scratch = lambda n: [pltpu.VMEM((ATT_TILE, NA_HEADS * ATT_TILE), F32)] * n
    lat = pl.pallas_call(
        functools.partial(_na_kernel, rows, True),
        grid=(nb, n_groups),
        in_specs=[q_lat, pl.BlockSpec((s_len, NA_W), lambda b, g: (b, 0)),
                  pl.BlockSpec((s_len // ATT_TILE, NA_W, ATT_TILE), lambda b, g: (b, 0, 0)), k_ctx, v_ctx,
                  pl.BlockSpec((1,) + bias_tabs.shape[1:], bias_map)],
        out_specs=q_lat,
        out_shape=jax.ShapeDtypeStruct((dims["n_lat"], NA_W), BF16),
        scratch_shapes=scratch(n_key_tiles),
        compiler_params=_params(("parallel", "arbitrary")),
        name="neighbourhood_attention",
    )(nq, nk, nvt, nk, nvt, bias_tabs)
    if not with_ctx:
        return lat, None
    ctx = pl.pallas_call(
        functools.partial(_na_kernel, rows, False),
        grid=(nb,),
        in_specs=[pl.BlockSpec((ATT_TILE, NA_W), lambda b: (lat_blocks + b, 0)), k_ctx, v_ctx],
        out_specs=pl.BlockSpec((ATT_TILE, NA_W), lambda b: (b, 0)),
        out_shape=jax.ShapeDtypeStruct((nb * ATT_TILE, NA_W), BF16),
        scratch_shapes=scratch(1),
        compiler_params=_params(("parallel",)),
        name="context_attention",
    )(nq, nk, nvt)
    return lat, ctx


def _route_top2(logits):
    lane = lax.broadcasted_iota(jnp.int32, logits.shape, 1)
    lane_f = lane.astype(F32)
    lg = jnp.where(lane < N_EXPERTS, logits, -jnp.inf)
    m1 = jnp.max(lg, axis=-1, keepdims=True)
    i1 = jnp.min(jnp.where(lg == m1, lane_f, float(LANES)), axis=-1, keepdims=True)
    lg2 = jnp.where(lane_f == i1, -jnp.inf, lg)
    m2 = jnp.max(lg2, axis=-1, keepdims=True)
    i2 = jnp.min(jnp.where(lg2 == m2, lane_f, float(LANES)), axis=-1, keepdims=True)
    e = jnp.exp(m2 - m1)
    w1 = 1.0 / (1.0 + e)
    w2 = e / (1.0 + e)
    return jnp.where(lane == 0, i1, jnp.where(lane == 1, i2, jnp.where(lane == 2, w1,
                                                                         jnp.where(lane == 3, w2, 0.0))))


def _outproj_kernel(moe, n_lat_tiles, x_ref, of_ref, ob_ref, gate_ref, bd_ref, cn_ref, *rest):
    if moe:
        mod_ref, ggla_ref, g2_ref, w_ref, wr_ref, x_out_ref, h_ref, route_ref = rest
        b_diff, c_na = bd_ref[...], cn_ref[...]
    else:
        bdc_ref, cnc_ref, mod_ref, ggla_ref, g2_ref, w_ref, x_out_ref, h_ref = rest
        is_lat = pl.program_id(0) < n_lat_tiles
        b_diff = jnp.where(is_lat, bd_ref[...], bdc_ref[...])
        c_na = jnp.where(is_lat, cn_ref[...], cnc_ref[...])
    m = mod_ref[0, 0]
    o = of_ref[...] + ob_ref[...]
    avg = (lax.broadcasted_iota(jnp.int32, (GLA_V, GLA_V), 0) // GLA_DV
           == lax.broadcasted_iota(jnp.int32, (GLA_V, GLA_V), 1) // GLA_DV).astype(F32) * (1.0 / GLA_DV)
    ms = jnp.dot(o * o, avg, precision=HIGHEST, preferred_element_type=F32)
    a = o * lax.rsqrt(ms + RMS_EPS) * ggla_ref[...] * _silu(gate_ref[...].astype(F32))
    y = (_dot(a.astype(BF16), w_ref[0:GLA_V]) + _dot(b_diff, w_ref[GLA_V:GLA_V + DIFF_W])
         + _dot(c_na, w_ref[GLA_V + DIFF_W:]))
    x = x_ref[...] + m[2:3] * y
    x_out_ref[...] = x
    h = x * lax.rsqrt(jnp.mean(x * x, axis=-1, keepdims=True) + RMS_EPS) * g2_ref[...]
    h = h * (1.0 + m[4:5]) + m[3:4]
    h_ref[...] = h.astype(h_ref.dtype)
    if moe:
        h_hi, w = h.astype(BF16), wr_ref[...]
        h_lo = (h - h_hi.astype(F32)).astype(BF16)
        w_hi = w.astype(BF16)
        w_lo = (w - w_hi.astype(F32)).astype(BF16)
        route_ref[...] = _route_top2(_dot(h_hi, w_hi) + (_dot(h_hi, w_lo) + _dot(h_lo, w_hi)))


def _output_projection(x_all, o_f, o_b, gla, b_diff, c_na, mod, layer, g_gla, g2, w_out, w_router, n_rows, dims):
    d = x_all.shape[1]
    n_lat_tiles = dims["n_lat"] // TOK_TILE
    tiles_per_batch = dims["S"] // TOK_TILE
    nb = dims["B"]
    moe = w_router is not None
    assert moe == (b_diff[1] is None) and n_rows == (dims["n_lat"] if moe else x_all.shape[0])

    def mod_map(i):
        return (layer, jnp.where(i < n_lat_tiles, i // tiles_per_batch, nb), 0, 0)

    row = lambda w, cb=0: pl.BlockSpec((TOK_TILE, w), lambda i: (i, cb))
    lat_row = lambda w: pl.BlockSpec((TOK_TILE, w), lambda i: (jnp.minimum(i, n_lat_tiles - 1), 0))
    ctx_row = lambda w: pl.BlockSpec((TOK_TILE, w), lambda i: (jnp.maximum(i - n_lat_tiles, 0), 0))
    const = lambda shape: pl.BlockSpec(shape, lambda i: (0,) * len(shape))
    in_specs = [row(d), row(GLA_V), row(GLA_V), row(GLA_V, (2 * GLA_QK + GLA_V) // GLA_V),
                lat_row(DIFF_W), lat_row(NA_W)]
    args = [x_all, o_f, o_b, gla, b_diff[0], c_na[0]]
    if not moe:
        in_specs += [ctx_row(DIFF_W), ctx_row(NA_W)]
        args += [b_diff[1], c_na[1]]
    in_specs += [pl.BlockSpec((1, 1, 6, d), mod_map), const((1, GLA_V)), const((1, d)), const((d, d))]
    args += [mod, g_gla, g2, w_out]
    out_specs = [row(d), row(d)]
    out_shape = [jax.ShapeDtypeStruct((n_rows, d), F32), jax.ShapeDtypeStruct((n_rows, d), F32 if moe else BF16)]
    if moe:
        in_specs.append(const((d, LANES)))
        args.append(w_router)
        out_specs.append(row(LANES))
        out_shape.append(jax.ShapeDtypeStruct((n_rows, LANES), F32))
    return pl.pallas_call(
        functools.partial(_outproj_kernel, moe, n_lat_tiles),
        grid=(n_rows // TOK_TILE,),
        in_specs=in_specs, out_specs=out_specs, out_shape=out_shape,
        compiler_params=_params(("parallel",)),
        name="output_projection",
    )(*args)


def _ffn_kernel(h_ref, x_ref, mod_ref, wg_ref, wu_ref, wd_ref, o_ref, acc_ref):
    h = h_ref[...]
    n_chunks = wg_ref.shape[1] // FFN_CHUNK
    for j in range(n_chunks):
        cols = slice(j * FFN_CHUNK, (j + 1) * FFN_CHUNK)
        act = (_silu(_dot(h, wg_ref[:, cols])) * _dot(h, wu_ref[:, cols])).astype(BF16)
        part = _dot(act, wd_ref[cols, :])
        if j == 0:
            acc_ref[...] = part
        else:
            acc_ref[...] += part
    o_ref[...] = x_ref[...] + mod_ref[0, 0][5:6] * acc_ref[...]


def _dense_ffn(h, x_all, mod, layer, wg, wu, wd, dims):
    n_rows, d = x_all.shape
    f = wg.shape[1]
    n_lat_tiles = dims["n_lat"] // TOK_TILE
    tiles_per_batch = dims["S"] // TOK_TILE
    nb = dims["B"]

    def mod_map(i):
        return (layer, jnp.where(i < n_lat_tiles, i // tiles_per_batch, nb), 0, 0)

    row = lambda: pl.BlockSpec((TOK_TILE, d), lambda i: (i, 0))
    const = lambda shape: pl.BlockSpec(shape, lambda i: (0, 0))
    return pl.pallas_call(
        _ffn_kernel,
        grid=(n_rows // TOK_TILE,),
        in_specs=[row(), row(), pl.BlockSpec((1, 1, 6, d), mod_map), const((d, f)), const((d, f)), const((f, d))],
        out_specs=row(),
        out_shape=jax.ShapeDtypeStruct((n_rows, d), F32),
        scratch_shapes=[pltpu.VMEM((TOK_TILE, d), F32)],
        compiler_params=_params(("parallel",)),
        name="dense_swiglu",
    )(h, x_all, mod, wg, wu, wd)


def _moe_routing(route, n_tok):
    e_flat = jnp.concatenate([route[:, 0], route[:, 1]]).astype(jnp.int32)
    onehot = (e_flat[:, None] == jnp.arange(N_EXPERTS, dtype=jnp.int32)[None, :]).astype(jnp.int32)
    csum = jnp.cumsum(onehot, axis=0)
    counts = csum[-1]
    padded = ((counts + MOE_TILE - 1) // MOE_TILE) * MOE_TILE
    ends = jnp.cumsum(padded)
    pos = jnp.sum(onehot * (csum - 1 + (ends - padded)[None, :]), axis=1)
    n_rows = 2 * n_tok + N_EXPERTS * MOE_TILE
    tile_start = jnp.arange(n_rows // MOE_TILE, dtype=jnp.int32) * MOE_TILE
    tile_expert = jnp.minimum(jnp.sum((tile_start[:, None] >= ends[None, :]).astype(jnp.int32), axis=1),
                              N_EXPERTS - 1).astype(jnp.int32)
    n_active = (ends[-1:] // MOE_TILE).astype(jnp.int32)
    unused = ends[-1] + jnp.arange(N_EXPERTS, dtype=jnp.int32) * MOE_TILE
    tails = jnp.concatenate([jnp.where(padded > 0, ends - MOE_TILE, -1),
                             jnp.where(unused < n_rows, unused, -1)]).astype(jnp.int32)
    n_tiles = n_tok // TOK_TILE
    pos = jnp.concatenate([pos[:n_tok].reshape(n_tiles, 1, TOK_TILE), pos[n_tok:].reshape(n_tiles, 1, TOK_TILE)],
                          axis=2)
    return pos, tails, n_rows, tile_expert, n_active


def _dispatch_kernel(tail_ref, pos_ref, h_ref, xs_out, zero_ref, sem):
    tile = h_ref.shape[0]

    @pl.when(pl.program_id(0) == 0)
    def _():
        zero_ref[...] = jnp.zeros_like(zero_ref)
        for e in range(2 * N_EXPERTS):
            @pl.when(tail_ref[e] >= 0)
            def _():
                rows = pl.ds(pl.multiple_of(tail_ref[e], MOE_TILE), MOE_TILE)
                tail = pltpu.make_async_copy(zero_ref, xs_out.at[rows], sem)
                tail.start()
                tail.wait()

    def issue(r, carry):
        for k in range(2):
            pltpu.make_async_copy(h_ref.at[pl.ds(r, 1)], xs_out.at[pl.ds(pos_ref[0, 0, k * tile + r], 1)],
                                  sem).start()
        return carry
    lax.fori_loop(0, tile, issue, 0, unroll=8)
    for k in range(2):
        pltpu.make_async_copy(h_ref, xs_out.at[pl.ds(0, tile)], sem).wait()


def _moe_dispatch(h, pos, tails, n_rows):
    n_tok, d = h.shape
    grid_spec = pltpu.PrefetchScalarGridSpec(
        num_scalar_prefetch=1,
        grid=(n_tok // TOK_TILE,),
        in_specs=[pl.BlockSpec((1, 1, 2 * TOK_TILE), lambda i, tl: (i, 0, 0), memory_space=pltpu.SMEM),
                  pl.BlockSpec((TOK_TILE, d), lambda i, tl: (i, 0))],
        out_specs=pl.BlockSpec(memory_space=pl.ANY),
        scratch_shapes=[pltpu.VMEM((MOE_TILE, d), h.dtype), pltpu.SemaphoreType.DMA(())])
    return pl.pallas_call(
        _dispatch_kernel,
        grid_spec=grid_spec,
        out_shape=jax.ShapeDtypeStruct((n_rows, d), h.dtype),
        compiler_params=_params(("arbitrary",)),
        name="expert_dispatch",
    )(tails, pos, h)


def _moe_kernel(te_ref, nact_ref, x_ref, wg_ref, wu_ref, wd_ref, o_ref, xb_ref, acc_ref):
    i, f = pl.program_id(0), pl.program_id(1)
    active = i < nact_ref[0]
    last = f == pl.num_programs(1) - 1

    @pl.when(f == 0)
    def _():
        xb_ref[...] = x_ref[...].astype(BF16)
        acc_ref[...] = jnp.zeros_like(acc_ref)

    @pl.when(active)
    def _():
        x = xb_ref[...]
        for j in range(wg_ref.shape[2] // FFN_CHUNK):
            cols = slice(j * FFN_CHUNK, (j + 1) * FFN_CHUNK)
            act = (_silu(_dot(x, wg_ref[0, :, cols])) * _dot(x, wu_ref[0, :, cols])).astype(BF16)
            acc_ref[...] += _dot(act, wd_ref[0, cols, :])

    @pl.when(last)
    def _():
        o_ref[...] = acc_ref[...]


def _moe_ffn(x_sorted, tile_expert, n_active, wg, wu, wd):
    n_rows, d = x_sorted.shape
    ffn = wg.shape[2]
    n_tiles = n_rows // MOE_TILE
    n_f = ffn // MOE_FCHUNK

    def row(i, f, te, na):
        return jnp.minimum(i, na[0] - 1)

    def fcol(i, f, te, na):
        return jnp.where(i < na[0], f, n_f - 1)

    grid_spec = pltpu.PrefetchScalarGridSpec(
        num_scalar_prefetch=2,
        grid=(n_tiles, n_f),
        in_specs=[pl.BlockSpec((MOE_TILE, d), lambda i, f, te, na: (row(i, f, te, na), 0)),
                  pl.BlockSpec((1, d, MOE_FCHUNK), lambda i, f, te, na: (te[i], 0, fcol(i, f, te, na))),
                  pl.BlockSpec((1, d, MOE_FCHUNK), lambda i, f, te, na: (te[i], 0, fcol(i, f, te, na))),
                  pl.BlockSpec((1, MOE_FCHUNK, d), lambda i, f, te, na: (te[i], fcol(i, f, te, na), 0))],
        out_specs=pl.BlockSpec((MOE_TILE, d), lambda i, f, te, na: (i, 0)),
        scratch_shapes=[pltpu.VMEM((MOE_TILE, d), BF16), pltpu.VMEM((MOE_TILE, d), F32)])
    return pl.pallas_call(
        _moe_kernel,
        grid_spec=grid_spec,
        out_shape=jax.ShapeDtypeStruct((n_rows, d), F32),
        compiler_params=_params(("arbitrary", "arbitrary")),
        name="expert_swiglu",
    )(tile_expert, n_active, x_sorted, wg, wu, wd)


def _final_kernel(pos_ref, x_ref, route_ref, mod_ref, g_ref, y_hbm, o_ref, buf_ref, sem):
    tile = x_ref.shape[0]

    def issue(r, carry):
        for k in range(2):
            pltpu.make_async_copy(y_hbm.at[pl.ds(pos_ref[0, 0, k * tile + r], 1)],
                                  buf_ref.at[k, pl.ds(r, 1)], sem).start()
        return carry
    lax.fori_loop(0, tile, issue, 0, unroll=8)
    for k in range(2):
        pltpu.make_async_copy(y_hbm.at[pl.ds(0, tile)], buf_ref.at[k], sem).wait()
    route = route_ref[...]
    y = route[:, 2:3] * buf_ref[0] + route[:, 3:4] * buf_ref[1]
    x = x_ref[...] + mod_ref[0, 0][5:6] * y
    o_ref[...] = x * lax.rsqrt(jnp.mean(x * x, axis=-1, keepdims=True) + RMS_EPS) * g_ref[...]


def _final_combine(x_lat, y_sorted, pos, route, mod, layer, g_final, dims):
    n_rows, d = x_lat.shape
    n_tiles = n_rows // TOK_TILE
    tiles_per_batch = dims["S"] // TOK_TILE
    return pl.pallas_call(
        _final_kernel,
        grid=(n_tiles,),
        in_specs=[pl.BlockSpec((1, 1, 2 * TOK_TILE), lambda i: (i, 0, 0), memory_space=pltpu.SMEM),
                  pl.BlockSpec((TOK_TILE, d), lambda i: (i, 0)),
                  pl.BlockSpec((TOK_TILE, LANES), lambda i: (i, 0)),
                  pl.BlockSpec((1, 1, 6, d), lambda i: (layer, i // tiles_per_batch, 0, 0)),
                  pl.BlockSpec((1, d), lambda i: (0, 0)),
                  pl.BlockSpec(memory_space=pl.ANY)],
        out_specs=pl.BlockSpec((TOK_TILE, d), lambda i: (i, 0)),
        out_shape=jax.ShapeDtypeStruct((n_rows, d), F32),
        scratch_shapes=[pltpu.VMEM((2, TOK_TILE, d), F32), pltpu.SemaphoreType.DMA(())],
        compiler_params=_params(("arbitrary",)),
        name="expert_combine_final_norm",
    )(pos, x_lat, route, mod, g_final, y_sorted)


def _rope_tables(s_len):
    t = np.arange(s_len)
    lane = np.arange(LANES) % HEAD_DIM
    quarter = HEAD_DIM // 4
    inv = 1.0 / (ROPE_BASE ** (jnp.arange(quarter, dtype=F32) / quarter))
    pos = np.where(lane[None, :] < HEAD_DIM // 2, (t // GRID_W)[:, None], (t % GRID_W)[:, None]).astype(np.float32)
    ang = jnp.asarray(pos) * inv[lane % quarter][None, :]
    first = jnp.asarray((lane % (HEAD_DIM // 2)) < quarter)[None, :]
    cos, sin = jnp.cos(ang), jnp.sin(ang)
    ident = jnp.zeros((TOK_TILE, LANES), F32)
    return (jnp.concatenate([cos, ident + 1.0]),
            jnp.concatenate([jnp.where(first, -sin, 0.0), ident]),
            jnp.concatenate([jnp.where(first, 0.0, sin), ident]))


def _rearranged_w_in(w):
    offs = np.concatenate([[0], np.cumsum(IN_SIZES)])
    part = lambda j: w[:, int(offs[j]):int(offs[j + 1])]
    pad = jnp.zeros((w.shape[0], LANES - 2 * GLA_LR), w.dtype)
    order = [0, 1, 2, 3, 6, 7, 9, 10, 4, 5]
    w_cat = jnp.concatenate([part(j) for j in order] + [pad], axis=1).astype(BF16)
    return w_cat, jnp.transpose(part(8)).astype(BF16), jnp.transpose(part(11)).astype(BF16)


def _decay_up(w_up, b):
    top = jnp.concatenate([w_up[0], jnp.zeros_like(w_up[0])], axis=1)
    bot = jnp.concatenate([jnp.zeros_like(w_up[1]), w_up[1]], axis=1)
    pad = jnp.zeros((LANES - 2 * GLA_LR, 2 * GLA_QK), w_up.dtype)
    return jnp.concatenate([top, bot, pad], axis=0).astype(BF16), jnp.concatenate([b[0], b[1]])[None, :]


def kernel(x, c, ctx, c_ctx, w_mod, b_mod, g_norm1, g_norm2, w_in, gla_w_dec_up, gla_b_dec, gla_g_norm,
           diff_lambda, diff_g_norm, na_rpb, w_out, w_ffn_gate, w_ffn_up, w_ffn_down, w_router, w_moe_gate,
           w_moe_up, w_moe_down, g_final):
    nb, s_len, d = x.shape
    c_len = ctx.shape[1]
    depth = w_mod.shape[0]
    assert c_len == ATT_TILE and s_len % TOK_TILE == 0 and (nb * c_len) % TOK_TILE == 0 and nb < MOD_ROWS
    n_lat = nb * s_len
    dims = {"B": nb, "S": s_len, "n_lat": n_lat}

    x_all = jnp.concatenate([x.reshape(n_lat, d), ctx.reshape(nb * c_len, d)], axis=0)
    cc = jnp.concatenate([c, c_ctx[None, :], jnp.zeros((MOD_ROWS - nb - 1, d), F32)], axis=0)
    mod = _modulation(cc, w_mod, b_mod).reshape(depth, MOD_ROWS, 6, d)
    rope_tabs = _rope_tables(s_len)

    for l in range(depth):
        last = l == depth - 1
        w_cat, w_dvt, w_nvt = _rearranged_w_in(w_in[l])
        wup2, bdec2 = _decay_up(gla_w_dec_up[l], gla_b_dec[l])
        gla, lg, dq, dk, dvt, nq, nk, nvt = _input_projection(
            x_all, mod, l, g_norm1[l][None, :], w_cat, w_dvt, w_nvt, wup2, bdec2, rope_tabs, dims)
        o_f, o_b = _gla(gla, lg, dims)
        b_diff = _diff_attention(dq, dk, dvt, diff_lambda[l], diff_g_norm[l][None, :], l, not last, dims)
        c_na = _neighbourhood_attention(nq, nk, nvt, _na_bias_tables(na_rpb[l], s_len // GRID_W), not last, dims)
        g_gla = jnp.tile(gla_g_norm[l], GLA_HEADS)[None, :]
        n_rows = n_lat if last else x_all.shape[0]
        if l % 2 == 0:
            i = l // 2
            x_mid, h = _output_projection(x_all, o_f, o_b, gla, b_diff, c_na, mod, l, g_gla, g_norm2[l][None, :],
                                          w_out[l].astype(BF16), None, n_rows, dims)
            assert not last, "the dense channel mixer is implemented for non-final layers only"
            x_all = _dense_ffn(h, x_mid, mod, l, w_ffn_gate[i].astype(BF16), w_ffn_up[i].astype(BF16),
                               w_ffn_down[i].astype(BF16), dims)
        else:
            i = l // 2
            w_r = jnp.concatenate([w_router[i], jnp.zeros((d, LANES - N_EXPERTS), F32)], axis=1)
            x_mid, h, route = _output_projection(x_all, o_f, o_b, gla, b_diff, c_na, mod, l, g_gla,
                                                 g_norm2[l][None, :], w_out[l].astype(BF16), w_r, n_rows, dims)
            assert last, "the expert layer is implemented for the final layer only"
            pos, tails, n_sorted, tile_expert, n_active = _moe_routing(route, n_rows)
            x_sorted = _moe_dispatch(h, pos, tails, n_sorted)
            y_sorted = _moe_ffn(x_sorted, tile_expert, n_active, w_moe_gate[i].astype(BF16),
                                w_moe_up[i].astype(BF16), w_moe_down[i].astype(BF16))
            return _final_combine(x_mid, y_sorted, pos, route, mod, l, g_final[None, :], dims).reshape(nb, s_len, d)
    raise NotImplementedError("the final layer must be an expert layer")
```

```python
import functools
import math

import numpy as np
import jax
import jax.numpy as jnp
from jax import lax
from jax.experimental import pallas as pl
from jax.experimental.pallas import tpu as pltpu

F32 = jnp.float32
BF16 = jnp.bfloat16
HIGHEST = lax.Precision.HIGHEST

GRID_W = 64
HEAD_DIM = 64
ROPE_BASE = 10000.0
RMS_EPS = 1e-6
GLA_HEADS = 4
GLA_DK = 32
GLA_DV = 64
GLA_LR = 16
GLA_NORMALIZER = 16.0
GLA_CHUNK = 64
DIFF_HEADS = 4
DIFF_HD = 64
NA_HEADS = 4
NA_HD = 64
WIN_R = 8
WIN_C = 16
N_EXPERTS = 8
GLA_QK = GLA_HEADS * GLA_DK
GLA_V = GLA_HEADS * GLA_DV
DIFF_W = DIFF_HEADS * 2 * DIFF_HD
NA_W = NA_HEADS * NA_HD
IN_SIZES = (GLA_QK, GLA_QK, GLA_V, GLA_V, GLA_LR, GLA_LR, DIFF_W, DIFF_W, DIFF_W, NA_W, NA_W, NA_W)

LANES = 128
TOK_TILE = 512
ATT_TILE = 256
DIFF_QT = 512
NA_QROWS = 4
NA_KROWS = 12
FFN_CHUNK = 256
MOE_TILE = 512
MOE_FCHUNK = 1792
MOD_ROWS = 16
NEG_BIG = -1e30
LOG2E = math.log2(math.e)
VMEM_LIMIT = 56 * 1024 * 1024

C_GLA = 0
C_DQ = C_GLA + 2 * GLA_QK + 2 * GLA_V
C_DK = C_DQ + DIFF_W
C_NQ = C_DK + DIFF_W
C_NK = C_NQ + NA_W
C_DEC = C_NK + NA_W
C_END = C_DEC + LANES


def _silu(x):
    return x * (1.0 / (1.0 + jnp.exp(-x)))


def _dot(a, b):
    return jnp.dot(a, b, preferred_element_type=F32)


def _dot_nt(a, b):
    return lax.dot_general(a, b, (((1,), (1,)), ((), ())), preferred_element_type=F32)


def _dot_tn(a, b):
    return lax.dot_general(a, b, (((0,), (0,)), ((), ())), preferred_element_type=F32)


def _params(sem, vmem=VMEM_LIMIT):
    return pltpu.CompilerParams(dimension_semantics=sem, vmem_limit_bytes=vmem)


def _mod_kernel(cc_ref, w_ref, b_ref, o_ref):
    s = _silu(cc_ref[...]).astype(BF16)
    o_ref[0] = _dot(s, w_ref[0].astype(BF16)) + b_ref[0]


def _modulation(cc, w_mod, b_mod):
    depth, d, six_d = w_mod.shape
    n = six_d // d
    return pl.pallas_call(
        _mod_kernel,
        grid=(depth, n),
        in_specs=[pl.BlockSpec((MOD_ROWS, d), lambda l, j: (0, 0)),
                  pl.BlockSpec((1, d, d), lambda l, j: (l, 0, j)),
                  pl.BlockSpec((1, 1, d), lambda l, j: (l, 0, j))],
        out_specs=pl.BlockSpec((1, MOD_ROWS, d), lambda l, j: (l, 0, j)),
        out_shape=jax.ShapeDtypeStruct((depth, MOD_ROWS, six_d), F32),
        compiler_params=_params(("parallel", "parallel")),
        name="modulation",
    )(cc, w_mod, b_mod.reshape(depth, 1, six_d))


def _rope(z, a, bt, ct):
    outs = []
    for s in range(z.shape[1] // LANES):
        zs = z[:, s * LANES:(s + 1) * LANES]
        outs.append(zs * a + pltpu.roll(zs, LANES - 16, 1) * bt + pltpu.roll(zs, 16, 1) * ct)
    return jnp.concatenate(outs, axis=1)


def _split_rows(x, n_lat_tiles, width):
    if not isinstance(x, tuple):
        return [x], [pl.BlockSpec((TOK_TILE, width), lambda i: (i, 0))]
    return list(x), [pl.BlockSpec((TOK_TILE, width), lambda i: (jnp.minimum(i, n_lat_tiles - 1), 0)),
                     pl.BlockSpec((TOK_TILE, width), lambda i: (jnp.maximum(i - n_lat_tiles, 0), 0))]


def _joined_rows(n_lat_tiles, refs):
    if len(refs) == 1:
        return refs[0][...]
    return jnp.where(pl.program_id(0) < n_lat_tiles, refs[0][...], refs[1][...])


def _inproj_kernel(n_lat_tiles, n_x, *refs):
    (mod_ref, g_ref, w_ref, wvt_ref, wnt_ref, wup_ref, bdec_ref, ra_ref, rb_ref, rc_ref,
     gla_ref, lg_ref, dq_ref, dk_ref, dvt_ref, nq_ref, nk_ref, nvt_ref) = refs[n_x:]
    x = _joined_rows(n_lat_tiles, refs[:n_x])
    m = mod_ref[0, 0]
    h = x * lax.rsqrt(jnp.mean(x * x, axis=-1, keepdims=True) + RMS_EPS) * g_ref[...]
    hb = (h * (1.0 + m[1:2]) + m[0:1]).astype(BF16)

    def proj(c0, c1):
        return _dot(hb, w_ref[:, c0:c1])

    gla_ref[...] = proj(C_GLA, C_DQ).astype(BF16)
    a, bt, ct = ra_ref[...], rb_ref[...], rc_ref[...]
    dq_ref[...] = (_rope(proj(C_DQ, C_DK), a, bt, ct) * (DIFF_HD ** -0.5 * LOG2E)).astype(BF16)
    dk_ref[...] = _rope(proj(C_DK, C_NQ), a, bt, ct).astype(BF16)
    dvt_ref[0] = _dot_nt(wvt_ref[...], hb).astype(BF16)
    nq_ref[...] = (proj(C_NQ, C_NK) * (NA_HD ** -0.5 * LOG2E)).astype(BF16)
    nk_ref[...] = proj(C_NK, C_DEC).astype(BF16)
    for j in range(nvt_ref.shape[0]):
        nvt_ref[j] = _dot_nt(wnt_ref[...], hb[j * ATT_TILE:(j + 1) * ATT_TILE]).astype(BF16)
    logits = _dot(proj(C_DEC, C_END).astype(BF16), wup_ref[...]) + bdec_ref[...]
    log_sig = jnp.minimum(logits, 0.0) - jnp.log1p(jnp.exp(-jnp.abs(logits)))
    lg_ref[...] = log_sig * (1.0 / GLA_NORMALIZER)


def _input_projection(x_all, mod, layer, g1, w_cat, w_dvt, w_nvt, wup2, bdec2, rope_tabs, dims):
    n_tot, d = dims["n_tot"], g1.shape[1]
    n_lat_tiles = dims["n_lat"] // TOK_TILE
    x_args, x_specs = _split_rows(x_all, n_lat_tiles, d)
    tiles_per_batch = dims["S"] // TOK_TILE
    n_tiles = n_tot // TOK_TILE
    nb = dims["B"]

    def mod_map(i):
        return (layer, jnp.where(i < n_lat_tiles, i // tiles_per_batch, nb), 0, 0)

    def rope_map(i):
        return (jnp.where(i < n_lat_tiles, i % tiles_per_batch, tiles_per_batch), 0)

    row = lambda w: pl.BlockSpec((TOK_TILE, w), lambda i: (i, 0))
    const = lambda shape: pl.BlockSpec(shape, lambda i: (0,) * len(shape))
    tab = pl.BlockSpec((TOK_TILE, LANES), rope_map)
    sub = TOK_TILE // ATT_TILE
    out_specs = [row(C_DQ), row(2 * GLA_QK), row(DIFF_W), row(DIFF_W),
                 pl.BlockSpec((1, DIFF_W, TOK_TILE), lambda i: (i, 0, 0)), row(NA_W), row(NA_W),
                 pl.BlockSpec((sub, NA_W, ATT_TILE), lambda i: (i, 0, 0))]
    shape = lambda w, dt=BF16: jax.ShapeDtypeStruct((n_tot, w), dt)
    out_shape = [shape(C_DQ), shape(2 * GLA_QK, F32), shape(DIFF_W), shape(DIFF_W),
                 jax.ShapeDtypeStruct((n_tiles, DIFF_W, TOK_TILE), BF16), shape(NA_W), shape(NA_W),
                 jax.ShapeDtypeStruct((n_tiles * sub, NA_W, ATT_TILE), BF16)]
    return pl.pallas_call(
        functools.partial(_inproj_kernel, n_lat_tiles, len(x_args)),
        grid=(n_tiles,),
        in_specs=x_specs + [pl.BlockSpec((1, 1, 6, d), mod_map), const((1, d)), const((d, C_END)),
                            const((DIFF_W, d)), const((NA_W, d)), const((LANES, 2 * GLA_QK)),
                            const((1, 2 * GLA_QK)), tab, tab, tab],
        out_specs=out_specs,
        out_shape=out_shape,
        compiler_params=_params(("parallel",)),
        name="input_projection",
    )(*x_args, mod, g1, w_cat, w_dvt, w_nvt, wup2, bdec2, *rope_tabs)


def _gla_block_local(g_ref, l_ref, fwd, consts):
    tri, same, tri4, head_k, head_v, bd = consts
    t = g_ref.shape[0]
    q = g_ref[:, 0:GLA_QK].astype(F32) * GLA_DK ** -0.5
    k = g_ref[:, GLA_QK:2 * GLA_QK].astype(F32)
    v = g_ref[:, 2 * GLA_QK:2 * GLA_QK + GLA_V]
    lcol = 0 if fwd else GLA_QK
    lg = l_ref[:, lcol:lcol + GLA_QK]
    lg_hi = lg.astype(BF16)
    lg_lo = (lg - lg_hi.astype(F32)).astype(BF16)
    b = _dot(tri, lg_hi) + _dot(tri, lg_lo)
    total = _dot(same, lg_hi) + _dot(same, lg_lo)
    q_dec = q * jnp.exp(b)
    k_inv = (k * jnp.exp(-b)).astype(BF16)
    k_end = (k * jnp.exp(total - b)).astype(BF16)
    qs = jnp.concatenate([jnp.where(head_k == h, q_dec, 0.0) for h in range(GLA_HEADS)], axis=0)
    att = jnp.where(tri4, _dot_nt(qs.astype(BF16), k_inv), 0.0)
    ov = _dot(att.astype(BF16), v)
    o = jnp.where(head_v == 0, ov[0:t], 0.0)
    for h in range(1, GLA_HEADS):
        o = o + jnp.where(head_v == h, ov[h * t:(h + 1) * t], 0.0)
    q_dec = q_dec.astype(BF16)
    chunks = []
    for c in range(t // GLA_CHUNK):
        rows = slice(c * GLA_CHUNK, (c + 1) * GLA_CHUNK)
        chunks.append((o[rows], q_dec[rows], jnp.where(bd, _dot_tn(v[rows], k_end[rows]), 0.0),
                       jnp.exp(total[c * GLA_CHUNK:c * GLA_CHUNK + 1])))
    return chunks


def _gla_kernel(gf_ref, lf_ref, gb_ref, lb_ref, of_ref, ob_ref, sf_ref, sb_ref):
    @pl.when(pl.program_id(1) == 0)
    def _():
        sf_ref[...] = jnp.zeros_like(sf_ref)
        sb_ref[...] = jnp.zeros_like(sb_ref)

    t = gf_ref.shape[0]
    r = lax.broadcasted_iota(jnp.int32, (t, t), 0)
    c = lax.broadcasted_iota(jnp.int32, (t, t), 1)
    r4 = lax.broadcasted_iota(jnp.int32, (GLA_HEADS * t, t), 0) % t
    c4 = lax.broadcasted_iota(jnp.int32, (GLA_HEADS * t, t), 1)
    same = r // GLA_CHUNK == c // GLA_CHUNK
    same4 = r4 // GLA_CHUNK == c4 // GLA_CHUNK
    head_k = lax.broadcasted_iota(jnp.int32, (1, GLA_QK), 1) // GLA_DK
    head_v = lax.broadcasted_iota(jnp.int32, (1, GLA_V), 1) // GLA_DV
    bd = (lax.broadcasted_iota(jnp.int32, (GLA_V, GLA_QK), 0) // GLA_DV
          == lax.broadcasted_iota(jnp.int32, (GLA_V, GLA_QK), 1) // GLA_DK)
    same_b = same.astype(BF16)
    lower = (jnp.logical_and(same, c <= r).astype(BF16), same_b, jnp.logical_and(same4, c4 <= r4), head_k, head_v, bd)
    upper = (jnp.logical_and(same, c >= r).astype(BF16), same_b, jnp.logical_and(same4, c4 >= r4), head_k, head_v, bd)
    n_chunks = t // GLA_CHUNK
    scans = ((gf_ref, lf_ref, of_ref, sf_ref, True, lower, list(range(n_chunks))),
             (gb_ref, lb_ref, ob_ref, sb_ref, False, upper, list(range(n_chunks - 1, -1, -1))))
    local = [_gla_block_local(g_ref, l_ref, fwd, consts) for g_ref, l_ref, _, _, fwd, consts, _ in scans]
    for (_, _, o_ref, s_ref, _, _, order), chunks in zip(scans, local):
        s = s_ref[...]
        for cidx, (o_intra, q_dec, inc, decay) in ((cidx, chunks[cidx]) for cidx in order):
            o = o_intra + _dot_nt(q_dec, s.astype(BF16))
            o_ref[cidx * GLA_CHUNK:(cidx + 1) * GLA_CHUNK, :] = o.astype(o_ref.dtype)
            s = decay * s + inc
        s_ref[...] = s


def _gla(gla, lg, dims):
    n_tot = gla.shape[0]
    nb, steps = dims["B"], 1 + dims["S"] // ATT_TILE
    lat_blocks = dims["n_lat"] // ATT_TILE
    per_batch = dims["S"] // ATT_TILE

    def fwd_map(b, i):
        return (jnp.where(i == 0, lat_blocks + b, b * per_batch + i - 1), 0)

    def bwd_map(b, i):
        return (jnp.where(i == 0, lat_blocks + b, b * per_batch + per_batch - i), 0)

    return pl.pallas_call(
        _gla_kernel,
        grid=(nb, steps),
        in_specs=[pl.BlockSpec((ATT_TILE, C_DQ), fwd_map), pl.BlockSpec((ATT_TILE, 2 * GLA_QK), fwd_map),
                  pl.BlockSpec((ATT_TILE, C_DQ), bwd_map), pl.BlockSpec((ATT_TILE, 2 * GLA_QK), bwd_map)],
        out_specs=[pl.BlockSpec((ATT_TILE, GLA_V), fwd_map), pl.BlockSpec((ATT_TILE, GLA_V), bwd_map)],
        out_shape=[jax.ShapeDtypeStruct((n_tot, GLA_V), BF16)] * 2,
        scratch_shapes=[pltpu.VMEM((GLA_V, GLA_QK), F32), pltpu.VMEM((GLA_V, GLA_QK), F32)],
        compiler_params=_params(("parallel", "arbitrary")),
        name="gla_scan",
    )(gla, lg, gla, lg)


def _diff_kernel(lam_init, latent, *refs):
    if latent:
        q_ref, kl_ref, vl_ref, kc_ref, vc_ref, lam_ref, g_ref, o_ref, *s_refs = refs
    else:
        q_ref, kc_ref, vc_ref, lam_ref, g_ref, o_ref, *s_refs = refs
    q = q_ref[...]
    t, hw = q.shape
    c_len = kc_ref.shape[0]
    lane = lax.broadcasted_iota(jnp.int32, q.shape, 1)
    zero = jnp.zeros_like(q)
    qm = (jnp.where(lane < DIFF_HD, q, zero), jnp.where(lane >= DIFF_HD, q, zero))

    tiles = [(lambda: kc_ref[...], lambda: vc_ref[0], 0, c_len)]
    if latent:
        tiles += [(lambda j=j: kl_ref[j * TOK_TILE:(j + 1) * TOK_TILE, :], lambda j=j: vl_ref[j],
                   c_len + j * TOK_TILE, TOK_TILE) for j in range(vl_ref.shape[0])]

    gw = min(t, ATT_TILE)
    groups = [(mp, c0) for mp in range(2) for c0 in range(0, t, gw)]

    def pass1(g, tile, m):
        mp, c0 = groups[g]
        load_k, _, off, n = tile
        s = _dot_nt(load_k(), qm[mp][c0:c0 + gw])
        s_refs[g][off:off + n, :] = s
        return jnp.maximum(m, jnp.max(s, axis=0, keepdims=True))

    def pass2(g, tile, m, l, acc):
        _, load_vt, off, n = tile
        p = jnp.exp2(s_refs[g][off:off + n, :] - m)
        return l + jnp.sum(p, axis=0, keepdims=True), acc + _dot(load_vt(), p.astype(BF16))

    m_prev, outs = None, []
    for g in range(len(groups) + 1):
        m = jnp.full((1, gw), -jnp.inf, F32)
        l, acc = jnp.zeros((1, gw), F32), jnp.zeros((hw, gw), F32)
        for tile in tiles:
            if g < len(groups):
                m = pass1(g, tile, m)
            if g > 0:
                l, acc = pass2(g - 1, tile, m_prev, l, acc)
        if g > 0:
            outs.append(acc / l)
        m_prev = m
    n_half = len(groups) // 2
    on = [jnp.concatenate(outs[i * n_half:(i + 1) * n_half], axis=1) for i in range(2)]

    lam = lam_ref[...]
    lam_full = (jnp.exp(jnp.sum(lam[0:1] * lam[1:2], axis=-1, keepdims=True))
                - jnp.exp(jnp.sum(lam[2:3] * lam[3:4], axis=-1, keepdims=True)) + lam_init)
    o = jnp.transpose(on[0] - lam_full * on[1])
    o = o * lax.rsqrt(jnp.mean(o * o, axis=-1, keepdims=True) + RMS_EPS) * g_ref[...]
    o_ref[...] = (o * (1.0 - lam_init)).astype(BF16)


def _diff_attention(dq, dk, dvt, lam, g_diff, layer, with_ctx, dims):
    n_tot = dq.shape[0]
    nb, s_len = dims["B"], dims["S"]
    n_lat_q = s_len // DIFF_QT
    lat_blocks = dims["n_lat"] // ATT_TILE
    lat_tiles = dims["n_lat"] // TOK_TILE
    ctx_per_tile = TOK_TILE // ATT_TILE
    lam_init = 0.8 - 0.6 * math.exp(-0.3 * layer)
    hw = 2 * DIFF_HD
    out_shape = jax.ShapeDtypeStruct((dims["n_lat"], DIFF_W), BF16)

    k_ctx = pl.BlockSpec((ATT_TILE, hw), lambda b, h, *_: (lat_blocks + b, h))
    v_ctx = pl.BlockSpec((1, hw, ATT_TILE), lambda b, h, *_: (lat_tiles + b // ctx_per_tile, h, b % ctx_per_tile))
    small = [pl.BlockSpec((4, DIFF_HD), lambda *_: (0, 0)), pl.BlockSpec((1, hw), lambda *_: (0, 0))]
    q_lat = pl.BlockSpec((DIFF_QT, hw), lambda b, h, i: (b * n_lat_q + i, h))
    out = pl.pallas_call(
        functools.partial(_diff_kernel, lam_init, True),
        grid=(nb, DIFF_HEADS, n_lat_q),
        in_specs=[q_lat, pl.BlockSpec((s_len, hw), lambda b, h, i: (b, h)),
                  pl.BlockSpec((s_len // TOK_TILE, hw, TOK_TILE), lambda b, h, i: (b, h, 0)), k_ctx, v_ctx] + small,
        out_specs=q_lat,
        out_shape=out_shape,
        scratch_shapes=[pltpu.VMEM((ATT_TILE + s_len, ATT_TILE), F32)] * (2 * DIFF_QT // ATT_TILE),
        compiler_params=_params(("parallel", "parallel", "arbitrary")),
        name="diff_attention",
    )(dq, dk, dvt, dk, dvt, lam, g_diff)
    if not with_ctx:
        return out, None
    out_ctx = pl.pallas_call(
        functools.partial(_diff_kernel, lam_init, False),
        grid=(nb, DIFF_HEADS),
        in_specs=[pl.BlockSpec((ATT_TILE, hw), lambda b, h: (lat_blocks + b, h)), k_ctx, v_ctx] + small,
        out_specs=pl.BlockSpec((ATT_TILE, hw), lambda b, h: (b, h)),
        out_shape=jax.ShapeDtypeStruct((nb * ATT_TILE, DIFF_W), BF16),
        scratch_shapes=[pltpu.VMEM((ATT_TILE, ATT_TILE), F32)] * 2,
        compiler_params=_params(("parallel", "parallel")),
        name="diff_attention_context",
    )(dq, dk, dvt, lam, g_diff)
    return out, out_ctx


def _na_bias_tables(rpb, rows):
    wr = min(WIN_R, rows)
    n_groups = rows // NA_QROWS
    qc = np.arange(GRID_W)[:, None]
    kc = np.arange(GRID_W)[None, :]
    cs = np.clip(qc - WIN_C // 2, 0, GRID_W - WIN_C)
    col_valid = (kc >= cs) & (kc < cs + WIN_C)
    onehot = ((kc - qc + WIN_C - 1)[None] == np.arange(2 * WIN_C - 1)[:, None, None]) & col_valid[None]
    toe = jnp.einsum("hrc,cqk->hrqk", rpb.astype(F32), jnp.asarray(onehot, F32), precision=HIGHEST)
    toe = jnp.where(col_valid[None, None], toe, NEG_BIG)
    tabs = []
    for g in (0, 1, n_groups - 1):
        r0 = g * NA_QROWS
        u0 = int(np.clip(r0 - WIN_R // 2, 0, rows - NA_KROWS))
        r = r0 + np.arange(NA_QROWS)[:, None]
        kr = u0 + np.arange(NA_KROWS)[None, :]
        rs = np.clip(r - WIN_R // 2, 0, rows - wr)
        row_valid = (kr >= rs) & (kr < rs + wr)
        rb = np.clip(kr - r + WIN_R - 1, 0, 2 * WIN_R - 2)
        t = jnp.where(row_valid[None, :, :, None, None], toe[:, rb], NEG_BIG)
        t = t.reshape(NA_HEADS, NA_QROWS, NA_KROWS // NA_QROWS, NA_QROWS, GRID_W, GRID_W)
        t = t.transpose(2, 3, 5, 0, 1, 4).reshape(NA_KROWS // NA_QROWS, ATT_TILE, NA_HEADS * ATT_TILE)
        tabs.append(t * LOG2E)
    return jnp.stack(tabs)


def _na_kernel(rows, latent, *refs):
    if latent:
        q_ref, kl_ref, vl_ref, kc_ref, vc_ref, bias_ref, o_ref, *s_refs = refs
    else:
        q_ref, kc_ref, vc_ref, o_ref, *s_refs = refs
    q = q_ref[...]
    t = q.shape[0]
    head = lax.broadcasted_iota(jnp.int32, q.shape, 1) // NA_HD
    zero = jnp.zeros_like(q)
    qs = jnp.concatenate([jnp.where(head == h, q, zero) for h in range(NA_HEADS)], axis=0)

    tiles = [(lambda: kc_ref[...], lambda: vc_ref[0], None)]
    if latent:
        t0 = jnp.clip(pl.program_id(1) - 1, 0, rows // NA_QROWS - NA_KROWS // NA_QROWS)
        for j in range(NA_KROWS // NA_QROWS):
            tiles.append((lambda j=j: kl_ref[pl.ds(pl.multiple_of((t0 + j) * t, t), t), :],
                          lambda j=j: vl_ref[t0 + j], lambda j=j: bias_ref[0, j]))

    m = jnp.full((1, NA_HEADS * t), -jnp.inf, F32)
    for i, (load_k, _, load_bias) in enumerate(tiles):
        s = _dot_nt(load_k(), qs)
        if load_bias is not None:
            s = s + load_bias()
        s_refs[i][...] = s
        m = jnp.maximum(m, jnp.max(s, axis=0, keepdims=True))
    l = jnp.zeros((1, NA_HEADS * t), F32)
    acc = [jnp.zeros((NA_HD, t), F32)] * NA_HEADS
    for i, (_, load_vt, _) in enumerate(tiles):
        p = jnp.exp2(s_refs[i][...] - m)
        l = l + jnp.sum(p, axis=0, keepdims=True)
        pb, vt = p.astype(BF16), load_vt()
        for h in range(NA_HEADS):
            acc[h] = acc[h] + _dot(vt[h * NA_HD:(h + 1) * NA_HD, :], pb[:, h * t:(h + 1) * t])
    o = jnp.concatenate([acc[h] / l[:, h * t:(h + 1) * t] for h in range(NA_HEADS)], axis=0)
    o_ref[...] = jnp.transpose(o).astype(BF16)


def _neighbourhood_attention(nq, nk, nvt, bias_tabs, with_ctx, dims):
    nb, s_len = dims["B"], dims["S"]
    rows = s_len // GRID_W
    n_groups = rows // NA_QROWS
    lat_blocks = dims["n_lat"] // ATT_TILE
    n_key_tiles = 1 + NA_KROWS // NA_QROWS

    def bias_map(b, g):
        return (jnp.where(g == 0, 0, jnp.where(g >= n_groups - 1, 2, 1)), 0, 0, 0)

    k_ctx = pl.BlockSpec((ATT_TILE, NA_W), lambda b, *_: (lat_blocks + b, 0))
    v_ctx = pl.BlockSpec((1, NA_W, ATT_TILE), lambda b, *_: (lat_blocks + b, 0, 0))
    q_lat = pl.BlockSpec((ATT_TILE, NA_W), lambda b, g: (b * n_groups + g, 0))
    scratch = lambda n: [pltpu.VMEM((ATT_TILE, NA_HEADS * ATT_TILE), F32)] * n
    lat = pl.pallas_call(
        functools.partial(_na_kernel, rows, True),
        grid=(nb, n_groups),
        in_specs=[q_lat, pl.BlockSpec((s_len, NA_W), lambda b, g: (b, 0)),
                  pl.BlockSpec((s_len // ATT_TILE, NA_W, ATT_TILE), lambda b, g: (b, 0, 0)), k_ctx, v_ctx,
                  pl.BlockSpec((1,) + bias_tabs.shape[1:], bias_map)],
        out_specs=q_lat,
        out_shape=jax.ShapeDtypeStruct((dims["n_lat"], NA_W), BF16),
        scratch_shapes=scratch(n_key_tiles),
        compiler_params=_params(("parallel", "arbitrary")),
        name="neighbourhood_attention",
    )(nq, nk, nvt, nk, nvt, bias_tabs)
    if not with_ctx:
        return lat, None
    ctx = pl.pallas_call(
        functools.partial(_na_kernel, rows, False),
        grid=(nb,),
        in_specs=[pl.BlockSpec((ATT_TILE, NA_W), lambda b: (lat_blocks + b, 0)), k_ctx, v_ctx],
        out_specs=pl.BlockSpec((ATT_TILE, NA_W), lambda b: (b, 0)),
        out_shape=jax.ShapeDtypeStruct((nb * ATT_TILE, NA_W), BF16),
        scratch_shapes=scratch(1),
        compiler_params=_params(("parallel",)),
        name="context_attention",
    )(nq, nk, nvt)
    return lat, ctx


def _route_top2(logits):
    lane = lax.broadcasted_iota(jnp.int32, logits.shape, 1)
    lane_f = lane.astype(F32)
    lg = jnp.where(lane < N_EXPERTS, logits, -jnp.inf)
    m1 = jnp.max(lg, axis=-1, keepdims=True)
    i1 = jnp.min(jnp.where(lg == m1, lane_f, float(LANES)), axis=-1, keepdims=True)
    lg2 = jnp.where(lane_f == i1, -jnp.inf, lg)
    m2 = jnp.max(lg2, axis=-1, keepdims=True)
    i2 = jnp.min(jnp.where(lg2 == m2, lane_f, float(LANES)), axis=-1, keepdims=True)
    e = jnp.exp(m2 - m1)
    w1 = 1.0 / (1.0 + e)
    w2 = e / (1.0 + e)
    return jnp.where(lane == 0, i1, jnp.where(lane == 1, i2, jnp.where(lane == 2, w1,
                                                                         jnp.where(lane == 3, w2, 0.0))))


def _outproj_kernel(moe, n_lat_tiles, n_x, *refs):
    x_in = _joined_rows(n_lat_tiles, refs[:n_x])
    of_ref, ob_ref, gate_ref, bd_ref, cn_ref, *rest = refs[n_x:]
    if moe:
        mod_ref, ggla_ref, g2_ref, w_ref, wr_ref, x_out_ref, h_ref, route_ref = rest
        b_diff, c_na = bd_ref[...], cn_ref[...]
    else:
        bdc_ref, cnc_ref, mod_ref, ggla_ref, g2_ref, w_ref, x_out_ref, h_ref = rest
        is_lat = pl.program_id(0) < n_lat_tiles
        b_diff = jnp.where(is_lat, bd_ref[...], bdc_ref[...])
        c_na = jnp.where(is_lat, cn_ref[...], cnc_ref[...])
    m = mod_ref[0, 0]
    o = of_ref[...].astype(F32) + ob_ref[...].astype(F32)
    avg = (lax.broadcasted_iota(jnp.int32, (GLA_V, GLA_V), 0) // GLA_DV
           == lax.broadcasted_iota(jnp.int32, (GLA_V, GLA_V), 1) // GLA_DV).astype(F32) * (1.0 / GLA_DV)
    ms = jnp.dot(o * o, avg, precision=HIGHEST, preferred_element_type=F32)
    a = o * lax.rsqrt(ms + RMS_EPS) * ggla_ref[...] * _silu(gate_ref[...].astype(F32))
    y = (_dot(a.astype(BF16), w_ref[0:GLA_V]) + _dot(b_diff, w_ref[GLA_V:GLA_V + DIFF_W])
         + _dot(c_na, w_ref[GLA_V + DIFF_W:]))
    x = x_in + m[2:3] * y
    x_out_ref[...] = x
    h = x * lax.rsqrt(jnp.mean(x * x, axis=-1, keepdims=True) + RMS_EPS) * g2_ref[...]
    h = h * (1.0 + m[4:5]) + m[3:4]
    h_ref[...] = h.astype(h_ref.dtype)
    if moe:
        h_hi, w = h.astype(BF16), wr_ref[...]
        h_lo = (h - h_hi.astype(F32)).astype(BF16)
        w_hi = w.astype(BF16)
        w_lo = (w - w_hi.astype(F32)).astype(BF16)
        route_ref[...] = _route_top2(_dot(h_hi, w_hi) + (_dot(h_hi, w_lo) + _dot(h_lo, w_hi)))


def _output_projection(x_all, o_f, o_b, gla, b_diff, c_na, mod, layer, g_gla, g2, w_out, w_router, n_rows, dims):
    d = w_out.shape[1]
    n_lat_tiles = dims["n_lat"] // TOK_TILE
    tiles_per_batch = dims["S"] // TOK_TILE
    nb = dims["B"]
    moe = w_router is not None
    assert moe == (b_diff[1] is None) and n_rows == (dims["n_lat"] if moe else dims["n_tot"])
    x_args, x_specs = _split_rows(x_all, n_lat_tiles, d)

    def mod_map(i):
        return (layer, jnp.where(i < n_lat_tiles, i // tiles_per_batch, nb), 0, 0)

    row = lambda w, cb=0: pl.BlockSpec((TOK_TILE, w), lambda i: (i, cb))
    lat_row = lambda w: pl.BlockSpec((TOK_TILE, w), lambda i: (jnp.minimum(i, n_lat_tiles - 1), 0))
    ctx_row = lambda w: pl.BlockSpec((TOK_TILE, w), lambda i: (jnp.maximum(i - n_lat_tiles, 0), 0))
    const = lambda shape: pl.BlockSpec(shape, lambda i: (0,) * len(shape))
    in_specs = x_specs + [row(GLA_V), row(GLA_V), row(GLA_V, (2 * GLA_QK + GLA_V) // GLA_V),
                          lat_row(DIFF_W), lat_row(NA_W)]
    args = x_args + [o_f, o_b, gla, b_diff[0], c_na[0]]
    if not moe:
        in_specs += [ctx_row(DIFF_W), ctx_row(NA_W)]
        args += [b_diff[1], c_na[1]]
    in_specs += [pl.BlockSpec((1, 1, 6, d), mod_map), const((1, GLA_V)), const((1, d)), const((d, d))]
    args += [mod, g_gla, g2, w_out]
    out_specs = [row(d), row(d)]
    out_shape = [jax.ShapeDtypeStruct((n_rows, d), F32), jax.ShapeDtypeStruct((n_rows, d), F32 if moe else BF16)]
    if moe:
        in_specs.append(const((d, LANES)))
        args.append(w_router)
        out_specs.append(row(LANES))
        out_shape.append(jax.ShapeDtypeStruct((n_rows, LANES), F32))
    return pl.pallas_call(
        functools.partial(_outproj_kernel, moe, n_lat_tiles, len(x_args)),
        grid=(n_rows // TOK_TILE,),
        in_specs=in_specs, out_specs=out_specs, out_shape=out_shape,
        compiler_params=_params(("parallel",)),
        name="output_projection",
    )(*args)


def _ffn_kernel(h_ref, x_ref, mod_ref, wg_ref, wu_ref, wd_ref, o_ref, acc_ref):
    h = h_ref[...]
    n_chunks = wg_ref.shape[1] // FFN_CHUNK
    for j in range(n_chunks):
        cols = slice(j * FFN_CHUNK, (j + 1) * FFN_CHUNK)
        act = (_silu(_dot(h, wg_ref[:, cols])) * _dot(h, wu_ref[:, cols])).astype(BF16)
        part = _dot(act, wd_ref[cols, :])
        if j == 0:
            acc_ref[...] = part
        else:
            acc_ref[...] += part
    o_ref[...] = x_ref[...] + mod_ref[0, 0][5:6] * acc_ref[...]


def _dense_ffn(h, x_all, mod, layer, wg, wu, wd, dims):
    n_rows, d = x_all.shape
    f = wg.shape[1]
    n_lat_tiles = dims["n_lat"] // TOK_TILE
    tiles_per_batch = dims["S"] // TOK_TILE
    nb = dims["B"]

    def mod_map(i):
        return (layer, jnp.where(i < n_lat_tiles, i // tiles_per_batch, nb), 0, 0)

    row = lambda: pl.BlockSpec((TOK_TILE, d), lambda i: (i, 0))
    const = lambda shape: pl.BlockSpec(shape, lambda i: (0, 0))
    return pl.pallas_call(
        _ffn_kernel,
        grid=(n_rows // TOK_TILE,),
        in_specs=[row(), row(), pl.BlockSpec((1, 1, 6, d), mod_map), const((d, f)), const((d, f)), const((f, d))],
        out_specs=row(),
        out_shape=jax.ShapeDtypeStruct((n_rows, d), F32),
        scratch_shapes=[pltpu.VMEM((TOK_TILE, d), F32)],
        compiler_params=_params(("parallel",)),
        name="dense_swiglu",
    )(h, x_all, mod, wg, wu, wd)


def _moe_routing(route, n_tok):
    e_flat = jnp.concatenate([route[:, 0], route[:, 1]]).astype(jnp.int32)
    onehot = (e_flat[:, None] == jnp.arange(N_EXPERTS, dtype=jnp.int32)[None, :]).astype(jnp.int32)
    csum = jnp.cumsum(onehot, axis=0)
    counts = csum[-1]
    padded = ((counts + MOE_TILE - 1) // MOE_TILE) * MOE_TILE
    ends = jnp.cumsum(padded)
    pos = jnp.sum(onehot * (csum - 1 + (ends - padded)[None, :]), axis=1)
    n_rows = 2 * n_tok + N_EXPERTS * MOE_TILE
    tile_start = jnp.arange(n_rows // MOE_TILE, dtype=jnp.int32) * MOE_TILE
    tile_expert = jnp.minimum(jnp.sum((tile_start[:, None] >= ends[None, :]).astype(jnp.int32), axis=1),
                              N_EXPERTS - 1).astype(jnp.int32)
    n_active = (ends[-1:] // MOE_TILE).astype(jnp.int32)
    unused = ends[-1] + jnp.arange(N_EXPERTS, dtype=jnp.int32) * MOE_TILE
    tails = jnp.concatenate([jnp.where(padded > 0, ends - MOE_TILE, -1),
                             jnp.where(unused < n_rows, unused, -1)]).astype(jnp.int32)
    n_tiles = n_tok // TOK_TILE
    pos = jnp.concatenate([pos[:n_tok].reshape(n_tiles, 1, TOK_TILE), pos[n_tok:].reshape(n_tiles, 1, TOK_TILE)],
                          axis=2)
    return pos, tails, n_rows, tile_expert, n_active


def _dispatch_kernel(tail_ref, pos_ref, h_ref, xs_out, zero_ref, sem):
    tile = h_ref.shape[0]

    @pl.when(pl.program_id(0) == 0)
    def _():
        zero_ref[...] = jnp.zeros_like(zero_ref)
        for e in range(2 * N_EXPERTS):
            @pl.when(tail_ref[e] >= 0)
            def _():
                rows = pl.ds(pl.multiple_of(tail_ref[e], MOE_TILE), MOE_TILE)
                tail = pltpu.make_async_copy(zero_ref, xs_out.at[rows], sem)
                tail.start()
                tail.wait()

    def issue(r, carry):
        for k in range(2):
            pltpu.make_async_copy(h_ref.at[pl.ds(r, 1)], xs_out.at[pl.ds(pos_ref[0, 0, k * tile + r], 1)],
                                  sem).start()
        return carry
    lax.fori_loop(0, tile, issue, 0, unroll=8)
    for k in range(2):
        pltpu.make_async_copy(h_ref, xs_out.at[pl.ds(0, tile)], sem).wait()


def _moe_dispatch(h, pos, tails, n_rows):
    n_tok, d = h.shape
    grid_spec = pltpu.PrefetchScalarGridSpec(
        num_scalar_prefetch=1,
        grid=(n_tok // TOK_TILE,),
        in_specs=[pl.BlockSpec((1, 1, 2 * TOK_TILE), lambda i, tl: (i, 0, 0), memory_space=pltpu.SMEM),
                  pl.BlockSpec((TOK_TILE, d), lambda i, tl: (i, 0))],
        out_specs=pl.BlockSpec(memory_space=pl.ANY),
        scratch_shapes=[pltpu.VMEM((MOE_TILE, d), h.dtype), pltpu.SemaphoreType.DMA(())])
    return pl.pallas_call(
        _dispatch_kernel,
        grid_spec=grid_spec,
        out_shape=jax.ShapeDtypeStruct((n_rows, d), h.dtype),
        compiler_params=_params(("arbitrary",)),
        name="expert_dispatch",
    )(tails, pos, h)


def _moe_kernel(te_ref, nact_ref, x_ref, wg_ref, wu_ref, wd_ref, o_ref, xb_ref, acc_ref):
    i, f = pl.program_id(0), pl.program_id(1)
    active = i < nact_ref[0]
    last = f == pl.num_programs(1) - 1

    @pl.when(f == 0)
    def _():
        xb_ref[...] = x_ref[...].astype(BF16)
        acc_ref[...] = jnp.zeros_like(acc_ref)

    @pl.when(active)
    def _():
        x = xb_ref[...]
        for j in range(wg_ref.shape[2] // FFN_CHUNK):
            cols = slice(j * FFN_CHUNK, (j + 1) * FFN_CHUNK)
            act = (_silu(_dot(x, wg_ref[0, :, cols])) * _dot(x, wu_ref[0, :, cols])).astype(BF16)
            acc_ref[...] += _dot(act, wd_ref[0, cols, :])

    @pl.when(last)
    def _():
        o_ref[...] = acc_ref[...]


def _moe_ffn(x_sorted, tile_expert, n_active, wg, wu, wd):
    n_rows, d = x_sorted.shape
    ffn = wg.shape[2]
    n_tiles = n_rows // MOE_TILE
    n_f = ffn // MOE_FCHUNK

    def row(i, f, te, na):
        return jnp.minimum(i, na[0] - 1)

    def fcol(i, f, te, na):
        return jnp.where(i < na[0], f, n_f - 1)

    grid_spec = pltpu.PrefetchScalarGridSpec(
        num_scalar_prefetch=2,
        grid=(n_tiles, n_f),
        in_specs=[pl.BlockSpec((MOE_TILE, d), lambda i, f, te, na: (row(i, f, te, na), 0)),
                  pl.BlockSpec((1, d, MOE_FCHUNK), lambda i, f, te, na: (te[i], 0, fcol(i, f, te, na))),
                  pl.BlockSpec((1, d, MOE_FCHUNK), lambda i, f, te, na: (te[i], 0, fcol(i, f, te, na))),
                  pl.BlockSpec((1, MOE_FCHUNK, d), lambda i, f, te, na: (te[i], fcol(i, f, te, na), 0))],
        out_specs=pl.BlockSpec((MOE_TILE, d), lambda i, f, te, na: (i, 0)),
        scratch_shapes=[pltpu.VMEM((MOE_TILE, d), BF16), pltpu.VMEM((MOE_TILE, d), F32)])
    return pl.pallas_call(
        _moe_kernel,
        grid_spec=grid_spec,
        out_shape=jax.ShapeDtypeStruct((n_rows, d), F32),
        compiler_params=_params(("arbitrary", "arbitrary")),
        name="expert_swiglu",
    )(tile_expert, n_active, x_sorted, wg, wu, wd)


def _final_kernel(pos_ref, next_pos_ref, x_ref, route_ref, mod_ref, g_ref, y_hbm, o_ref, buf_ref, sem):
    tile = x_ref.shape[0]
    i, n = pl.program_id(0), pl.num_programs(0)
    slot = i % 2

    def fetch(p_ref, s):
        def issue(r, carry):
            for k in range(2):
                pltpu.make_async_copy(y_hbm.at[pl.ds(p_ref[0, 0, k * tile + r], 1)],
                                      buf_ref.at[s, k, pl.ds(r, 1)], sem.at[s]).start()
            return carry
        lax.fori_loop(0, tile, issue, 0, unroll=8)

    @pl.when(i == 0)
    def _():
        fetch(pos_ref, slot)

    @pl.when(i + 1 < n)
    def _():
        fetch(next_pos_ref, 1 - slot)

    for k in range(2):
        pltpu.make_async_copy(y_hbm.at[pl.ds(0, tile)], buf_ref.at[slot, k], sem.at[slot]).wait()
    route = route_ref[...]
    y = route[:, 2:3] * buf_ref[slot, 0] + route[:, 3:4] * buf_ref[slot, 1]
    x = x_ref[...] + mod_ref[0, 0][5:6] * y
    o_ref[...] = x * lax.rsqrt(jnp.mean(x * x, axis=-1, keepdims=True) + RMS_EPS) * g_ref[...]


def _final_combine(x_lat, y_sorted, pos, route, mod, layer, g_final, dims):
    n_rows, d = x_lat.shape
    n_tiles = n_rows // TOK_TILE
    tiles_per_batch = dims["S"] // TOK_TILE
    return pl.pallas_call(
        _final_kernel,
        grid=(n_tiles,),
        in_specs=[pl.BlockSpec((1, 1, 2 * TOK_TILE), lambda i: (i, 0, 0), memory_space=pltpu.SMEM),
                  pl.BlockSpec((1, 1, 2 * TOK_TILE), lambda i: (jnp.minimum(i + 1, n_tiles - 1), 0, 0),
                               memory_space=pltpu.SMEM),
                  pl.BlockSpec((TOK_TILE, d), lambda i: (i, 0)),
                  pl.BlockSpec((TOK_TILE, LANES), lambda i: (i, 0)),
                  pl.BlockSpec((1, 1, 6, d), lambda i: (layer, i // tiles_per_batch, 0, 0)),
                  pl.BlockSpec((1, d), lambda i: (0, 0)),
                  pl.BlockSpec(memory_space=pl.ANY)],
        out_specs=pl.BlockSpec((TOK_TILE, d), lambda i: (i, 0)),
        out_shape=jax.ShapeDtypeStruct((n_rows, d), F32),
        scratch_shapes=[pltpu.VMEM((2, 2, TOK_TILE, d), F32), pltpu.SemaphoreType.DMA((2,))],
        compiler_params=_params(("arbitrary",)),
        name="expert_combine_final_norm",
    )(pos, pos, x_lat, route, mod, g_final, y_sorted)


def _rope_tables(s_len):
    t = np.arange(s_len)
    lane = np.arange(LANES) % HEAD_DIM
    quarter = HEAD_DIM // 4
    inv = 1.0 / (ROPE_BASE ** (jnp.arange(quarter, dtype=F32) / quarter))
    pos = np.where(lane[None, :] < HEAD_DIM // 2, (t // GRID_W)[:, None], (t % GRID_W)[:, None]).astype(np.float32)
    ang = jnp.asarray(pos) * inv[lane % quarter][None, :]
    first = jnp.asarray((lane % (HEAD_DIM // 2)) < quarter)[None, :]
    cos, sin = jnp.cos(ang), jnp.sin(ang)
    ident = jnp.zeros((TOK_TILE, LANES), F32)
    return (jnp.concatenate([cos, ident + 1.0]),
            jnp.concatenate([jnp.where(first, -sin, 0.0), ident]),
            jnp.concatenate([jnp.where(first, 0.0, sin), ident]))


def _rearranged_w_in(w):
    offs = np.concatenate([[0], np.cumsum(IN_SIZES)])
    part = lambda j: w[:, int(offs[j]):int(offs[j + 1])]
    pad = jnp.zeros((w.shape[0], LANES - 2 * GLA_LR), w.dtype)
    order = [0, 1, 2, 3, 6, 7, 9, 10, 4, 5]
    w_cat = jnp.concatenate([part(j) for j in order] + [pad], axis=1).astype(BF16)
    return w_cat, jnp.transpose(part(8)).astype(BF16), jnp.transpose(part(11)).astype(BF16)


def _decay_up(w_up, b):
    top = jnp.concatenate([w_up[0], jnp.zeros_like(w_up[0])], axis=1)
    bot = jnp.concatenate([jnp.zeros_like(w_up[1]), w_up[1]], axis=1)
    pad = jnp.zeros((LANES - 2 * GLA_LR, 2 * GLA_QK), w_up.dtype)
    return jnp.concatenate([top, bot, pad], axis=0).astype(BF16), jnp.concatenate([b[0], b[1]])[None, :]


def kernel(x, c, ctx, c_ctx, w_mod, b_mod, g_norm1, g_norm2, w_in, gla_w_dec_up, gla_b_dec, gla_g_norm,
           diff_lambda, diff_g_norm, na_rpb, w_out, w_ffn_gate, w_ffn_up, w_ffn_down, w_router, w_moe_gate,
           w_moe_up, w_moe_down, g_final):
    nb, s_len, d = x.shape
    c_len = ctx.shape[1]
    depth = w_mod.shape[0]
    assert c_len == ATT_TILE and s_len % TOK_TILE == 0 and (nb * c_len) % TOK_TILE == 0 and nb < MOD_ROWS
    n_lat = nb * s_len
    dims = {"B": nb, "S": s_len, "n_lat": n_lat, "n_tot": n_lat + nb * c_len}

    x_all = (x.reshape(n_lat, d), ctx.reshape(nb * c_len, d))
    cc = jnp.concatenate([c, c_ctx[None, :], jnp.zeros((MOD_ROWS - nb - 1, d), F32)], axis=0)
    mod = _modulation(cc, w_mod, b_mod).reshape(depth, MOD_ROWS, 6, d)
    rope_tabs = _rope_tables(s_len)

    for l in range(depth):
        last = l == depth - 1
        w_cat, w_dvt, w_nvt = _rearranged_w_in(w_in[l])
        wup2, bdec2 = _decay_up(gla_w_dec_up[l], gla_b_dec[l])
        gla, lg, dq, dk, dvt, nq, nk, nvt = _input_projection(
            x_all, mod, l, g_norm1[l][None, :], w_cat, w_dvt, w_nvt, wup2, bdec2, rope_tabs, dims)
        o_f, o_b = _gla(gla, lg, dims)
        b_diff = _diff_attention(dq, dk, dvt, diff_lambda[l], diff_g_norm[l][None, :], l, not last, dims)
        c_na = _neighbourhood_attention(nq, nk, nvt, _na_bias_tables(na_rpb[l], s_len // GRID_W), not last, dims)
        g_gla = jnp.tile(gla_g_norm[l], GLA_HEADS)[None, :]
        n_rows = n_lat if last else dims["n_tot"]
        if l % 2 == 0:
            i = l // 2
            x_mid, h = _output_projection(x_all, o_f, o_b, gla, b_diff, c_na, mod, l, g_gla, g_norm2[l][None, :],
                                          w_out[l].astype(BF16), None, n_rows, dims)
            assert not last, "the dense channel mixer is implemented for non-final layers only"
            x_all = _dense_ffn(h, x_mid, mod, l, w_ffn_gate[i].astype(BF16), w_ffn_up[i].astype(BF16),
                               w_ffn_down[i].astype(BF16), dims)
        else:
            i = l // 2
            w_r = jnp.concatenate([w_router[i], jnp.zeros((d, LANES - N_EXPERTS), F32)], axis=1)
            x_mid, h, route = _output_projection(x_all, o_f, o_b, gla, b_diff, c_na, mod, l, g_gla,
                                                 g_norm2[l][None, :], w_out[l].astype(BF16), w_r, n_rows, dims)
            assert last, "the expert layer is implemented for the final layer only"
            pos, tails, n_sorted, tile_expert, n_active = _moe_routing(route, n_rows)
            x_sorted = _moe_dispatch(h, pos, tails, n_sorted)
            y_sorted = _moe_ffn(x_sorted, tile_expert, n_active, w_moe_gate[i].astype(BF16),
                                w_moe_up[i].astype(BF16), w_moe_down[i].astype(BF16))
            return _final_combine(x_mid, y_sorted, pos, route, mod, l, g_final[None, :], dims).reshape(nb, s_len, d)
    raise NotImplementedError("the final layer must be an expert layer")
```

```python
import functools
import math

import numpy as np
import jax
import jax.numpy as jnp
from jax import lax
from jax.experimental import pallas as pl
from jax.experimental.pallas import tpu as pltpu

F32 = jnp.float32
BF16 = jnp.bfloat16
HIGHEST = lax.Precision.HIGHEST

GRID_W = 64
HEAD_DIM = 64
ROPE_BASE = 10000.0
RMS_EPS = 1e-6
GLA_HEADS = 4
GLA_DK = 32
GLA_DV = 64
GLA_LR = 16
GLA_NORMALIZER = 16.0
GLA_CHUNK = 64
DIFF_HEADS = 4
DIFF_HD = 64
NA_HEADS = 4
NA_HD = 64
WIN_R = 8
WIN_C = 16
N_EXPERTS = 8
GLA_QK = GLA_HEADS * GLA_DK
GLA_V = GLA_HEADS * GLA_DV
DIFF_W = DIFF_HEADS * 2 * DIFF_HD
NA_W = NA_HEADS * NA_HD
IN_SIZES = (GLA_QK, GLA_QK, GLA_V, GLA_V, GLA_LR, GLA_LR, DIFF_W, DIFF_W, DIFF_W, NA_W, NA_W, NA_W)

LANES = 128
TOK_TILE = 512
ATT_TILE = 256
DIFF_QT = 512
NA_QROWS = 4
NA_KROWS = 12
FFN_CHUNK = 256
MOE_TILE = 512
MOE_FCHUNK = 1792
MOD_ROWS = 16
NEG_BIG = -1e30
LOG2E = math.log2(math.e)
VMEM_LIMIT = 56 * 1024 * 1024

C_GLA = 0
C_DQ = C_GLA + 2 * GLA_QK + 2 * GLA_V
C_DK = C_DQ + DIFF_W
C_NQ = C_DK + DIFF_W
C_NK = C_NQ + NA_W
C_DEC = C_NK + NA_W
C_END = C_DEC + LANES


def _silu(x):
    return x * (1.0 / (1.0 + jnp.exp(-x)))


def _dot(a, b):
    return jnp.dot(a, b, preferred_element_type=F32)


def _dot_nt(a, b):
    return lax.dot_general(a, b, (((1,), (1,)), ((), ())), preferred_element_type=F32)


def _dot_tn(a, b):
    return lax.dot_general(a, b, (((0,), (0,)), ((), ())), preferred_element_type=F32)


def _params(sem, vmem=VMEM_LIMIT):
    return pltpu.CompilerParams(dimension_semantics=sem, vmem_limit_bytes=vmem)


def _mod_kernel(cc_ref, w_ref, b_ref, o_ref):
    s = _silu(cc_ref[...]).astype(BF16)
    o_ref[0] = _dot(s, w_ref[0].astype(BF16)) + b_ref[0]


def _modulation(cc, w_mod, b_mod):
    depth, d, six_d = w_mod.shape
    n = six_d // d
    return pl.pallas_call(
        _mod_kernel,
        grid=(depth, n),
        in_specs=[pl.BlockSpec((MOD_ROWS, d), lambda l, j: (0, 0)),
                  pl.BlockSpec((1, d, d), lambda l, j: (l, 0, j)),
                  pl.BlockSpec((1, 1, d), lambda l, j: (l, 0, j))],
        out_specs=pl.BlockSpec((1, MOD_ROWS, d), lambda l, j: (l, 0, j)),
        out_shape=jax.ShapeDtypeStruct((depth, MOD_ROWS, six_d), F32),
        compiler_params=_params(("parallel", "parallel")),
        name="modulation",
    )(cc, w_mod, b_mod.reshape(depth, 1, six_d))


def _rope(z, a, bt, ct):
    outs = []
    for s in range(z.shape[1] // LANES):
        zs = z[:, s * LANES:(s + 1) * LANES]
        outs.append(zs * a + pltpu.roll(zs, LANES - 16, 1) * bt + pltpu.roll(zs, 16, 1) * ct)
    return jnp.concatenate(outs, axis=1)


def _split_rows(x, n_lat_tiles, width):
    if not isinstance(x, tuple):
        return [x], [pl.BlockSpec((TOK_TILE, width), lambda i: (i, 0))]
    return list(x), [pl.BlockSpec((TOK_TILE, width), lambda i: (jnp.minimum(i, n_lat_tiles - 1), 0)),
                     pl.BlockSpec((TOK_TILE, width), lambda i: (jnp.maximum(i - n_lat_tiles, 0), 0))]


def _joined_rows(n_lat_tiles, refs):
    if len(refs) == 1:
        return refs[0][...]
    return jnp.where(pl.program_id(0) < n_lat_tiles, refs[0][...], refs[1][...])


def _inproj_kernel(n_lat_tiles, n_x, *refs):
    (mod_ref, g_ref, w_ref, wvt_ref, wnt_ref, wup_ref, bdec_ref, ra_ref, rb_ref, rc_ref,
     gla_ref, lg_ref, dq_ref, dk_ref, dvt_ref, nq_ref, nk_ref, nvt_ref) = refs[n_x:]
    x = _joined_rows(n_lat_tiles, refs[:n_x])
    m = mod_ref[0, 0]
    h = x * lax.rsqrt(jnp.mean(x * x, axis=-1, keepdims=True) + RMS_EPS) * g_ref[...]
    hb = (h * (1.0 + m[1:2]) + m[0:1]).astype(BF16)

    def proj(c0, c1):
        return _dot(hb, w_ref[:, c0:c1])

    gla_ref[...] = proj(C_GLA, C_DQ).astype(BF16)
    a, bt, ct = ra_ref[...], rb_ref[...], rc_ref[...]
    dq_ref[...] = (_rope(proj(C_DQ, C_DK), a, bt, ct) * (DIFF_HD ** -0.5 * LOG2E)).astype(BF16)
    dk_ref[...] = _rope(proj(C_DK, C_NQ), a, bt, ct).astype(BF16)
    dvt_ref[0] = _dot_nt(wvt_ref[...], hb).astype(BF16)
    nq_ref[...] = (proj(C_NQ, C_NK) * (NA_HD ** -0.5 * LOG2E)).astype(BF16)
    nk_ref[...] = proj(C_NK, C_DEC).astype(BF16)
    for j in range(nvt_ref.shape[0]):
        nvt_ref[j] = _dot_nt(wnt_ref[...], hb[j * ATT_TILE:(j + 1) * ATT_TILE]).astype(BF16)
    logits = _dot(proj(C_DEC, C_END).astype(BF16), wup_ref[...]) + bdec_ref[...]
    log_sig = jnp.minimum(logits, 0.0) - jnp.log1p(jnp.exp(-jnp.abs(logits)))
    lg_ref[...] = log_sig * (1.0 / GLA_NORMALIZER)


def _input_projection(x_all, mod, layer, g1, w_cat, w_dvt, w_nvt, wup2, bdec2, rope_tabs, dims):
    n_tot, d = dims["n_tot"], g1.shape[1]
    n_lat_tiles = dims["n_lat"] // TOK_TILE
    x_args, x_specs = _split_rows(x_all, n_lat_tiles, d)
    tiles_per_batch = dims["S"] // TOK_TILE
    n_tiles = n_tot // TOK_TILE
    nb = dims["B"]

    def mod_map(i):
        return (layer, jnp.where(i < n_lat_tiles, i // tiles_per_batch, nb), 0, 0)

    def rope_map(i):
        return (jnp.where(i < n_lat_tiles, i % tiles_per_batch, tiles_per_batch), 0)

    row = lambda w: pl.BlockSpec((TOK_TILE, w), lambda i: (i, 0))
    const = lambda shape: pl.BlockSpec(shape, lambda i: (0,) * len(shape))
    tab = pl.BlockSpec((TOK_TILE, LANES), rope_map)
    sub = TOK_TILE // ATT_TILE
    out_specs = [row(C_DQ), row(2 * GLA_QK), row(DIFF_W), row(DIFF_W),
                 pl.BlockSpec((1, DIFF_W, TOK_TILE), lambda i: (i, 0, 0)), row(NA_W), row(NA_W),
                 pl.BlockSpec((sub, NA_W, ATT_TILE), lambda i: (i, 0, 0))]
    shape = lambda w, dt=BF16: jax.ShapeDtypeStruct((n_tot, w), dt)
    out_shape = [shape(C_DQ), shape(2 * GLA_QK, F32), shape(DIFF_W), shape(DIFF_W),
                 jax.ShapeDtypeStruct((n_tiles, DIFF_W, TOK_TILE), BF16), shape(NA_W), shape(NA_W),
                 jax.ShapeDtypeStruct((n_tiles * sub, NA_W, ATT_TILE), BF16)]
    return pl.pallas_call(
        functools.partial(_inproj_kernel, n_lat_tiles, len(x_args)),
        grid=(n_tiles,),
        in_specs=x_specs + [pl.BlockSpec((1, 1, 6, d), mod_map), const((1, d)), const((d, C_END)),
                            const((DIFF_W, d)), const((NA_W, d)), const((LANES, 2 * GLA_QK)),
                            const((1, 2 * GLA_QK)), tab, tab, tab],
        out_specs=out_specs,
        out_shape=out_shape,
        compiler_params=_params(("parallel",)),
        name="input_projection",
    )(*x_args, mod, g1, w_cat, w_dvt, w_nvt, wup2, bdec2, *rope_tabs)


def _gla_block_local(g_ref, l_ref, fwd, consts):
    tri, same, tri4, head_k, head_v, bd = consts
    t = g_ref.shape[0]
    q = g_ref[:, 0:GLA_QK].astype(F32) * GLA_DK ** -0.5
    k = g_ref[:, GLA_QK:2 * GLA_QK].astype(F32)
    v = g_ref[:, 2 * GLA_QK:2 * GLA_QK + GLA_V]
    lcol = 0 if fwd else GLA_QK
    lg = l_ref[:, lcol:lcol + GLA_QK]
    lg_hi = lg.astype(BF16)
    lg_lo = (lg - lg_hi.astype(F32)).astype(BF16)
    b = _dot(tri, lg_hi) + _dot(tri, lg_lo)
    total = _dot(same, lg_hi) + _dot(same, lg_lo)
    q_dec = q * jnp.exp(b)
    k_inv = (k * jnp.exp(-b)).astype(BF16)
    k_end = (k * jnp.exp(total - b)).astype(BF16)
    qs = jnp.concatenate([jnp.where(head_k == h, q_dec, 0.0) for h in range(GLA_HEADS)], axis=0)
    att = jnp.where(tri4, _dot_nt(qs.astype(BF16), k_inv), 0.0)
    ov = _dot(att.astype(BF16), v)
    o = jnp.where(head_v == 0, ov[0:t], 0.0)
    for h in range(1, GLA_HEADS):
        o = o + jnp.where(head_v == h, ov[h * t:(h + 1) * t], 0.0)
    q_dec = q_dec.astype(BF16)
    chunks = []
    for c in range(t // GLA_CHUNK):
        rows = slice(c * GLA_CHUNK, (c + 1) * GLA_CHUNK)
        chunks.append((o[rows], q_dec[rows], jnp.where(bd, _dot_tn(v[rows], k_end[rows]), 0.0),
                       jnp.exp(total[c * GLA_CHUNK:c * GLA_CHUNK + 1])))
    return chunks


def _gla_kernel(gf_ref, lf_ref, gb_ref, lb_ref, of_ref, ob_ref, sf_ref, sb_ref):
    @pl.when(pl.program_id(1) == 0)
    def _():
        sf_ref[...] = jnp.zeros_like(sf_ref)
        sb_ref[...] = jnp.zeros_like(sb_ref)

    t = gf_ref.shape[0]
    r = lax.broadcasted_iota(jnp.int32, (t, t), 0)
    c = lax.broadcasted_iota(jnp.int32, (t, t), 1)
    r4 = lax.broadcasted_iota(jnp.int32, (GLA_HEADS * t, t), 0) % t
    c4 = lax.broadcasted_iota(jnp.int32, (GLA_HEADS * t, t), 1)
    same = r // GLA_CHUNK == c // GLA_CHUNK
    same4 = r4 // GLA_CHUNK == c4 // GLA_CHUNK
    head_k = lax.broadcasted_iota(jnp.int32, (1, GLA_QK), 1) // GLA_DK
    head_v = lax.broadcasted_iota(jnp.int32, (1, GLA_V), 1) // GLA_DV
    bd = (lax.broadcasted_iota(jnp.int32, (GLA_V, GLA_QK), 0) // GLA_DV
          == lax.broadcasted_iota(jnp.int32, (GLA_V, GLA_QK), 1) // GLA_DK)
    same_b = same.astype(BF16)
    lower = (jnp.logical_and(same, c <= r).astype(BF16), same_b, jnp.logical_and(same4, c4 <= r4), head_k, head_v, bd)
    upper = (jnp.logical_and(same, c >= r).astype(BF16), same_b, jnp.logical_and(same4, c4 >= r4), head_k, head_v, bd)
    n_chunks = t // GLA_CHUNK
    scans = ((gf_ref, lf_ref, of_ref, sf_ref, True, lower, list(range(n_chunks))),
             (gb_ref, lb_ref, ob_ref, sb_ref, False, upper, list(range(n_chunks - 1, -1, -1))))
    local = [_gla_block_local(g_ref, l_ref, fwd, consts) for g_ref, l_ref, _, _, fwd, consts, _ in scans]
    for (_, _, o_ref, s_ref, _, _, order), chunks in zip(scans, local):
        s = s_ref[...]
        for cidx, (o_intra, q_dec, inc, decay) in ((cidx, chunks[cidx]) for cidx in order):
            o = o_intra + _dot_nt(q_dec, s.astype(BF16))
            o_ref[cidx * GLA_CHUNK:(cidx + 1) * GLA_CHUNK, :] = o.astype(o_ref.dtype)
            s = decay * s + inc
        s_ref[...] = s


def _gla(gla, lg, dims):
    n_tot = gla.shape[0]
    nb, steps = dims["B"], 1 + dims["S"] // ATT_TILE
    lat_blocks = dims["n_lat"] // ATT_TILE
    per_batch = dims["S"] // ATT_TILE

    def fwd_map(b, i):
        return (jnp.where(i == 0, lat_blocks + b, b * per_batch + i - 1), 0)

    def bwd_map(b, i):
        return (jnp.where(i == 0, lat_blocks + b, b * per_batch + per_batch - i), 0)

    return pl.pallas_call(
        _gla_kernel,
        grid=(nb, steps),
        in_specs=[pl.BlockSpec((ATT_TILE, C_DQ), fwd_map), pl.BlockSpec((ATT_TILE, 2 * GLA_QK), fwd_map),
                  pl.BlockSpec((ATT_TILE, C_DQ), bwd_map), pl.BlockSpec((ATT_TILE, 2 * GLA_QK), bwd_map)],
        out_specs=[pl.BlockSpec((ATT_TILE, GLA_V), fwd_map), pl.BlockSpec((ATT_TILE, GLA_V), bwd_map)],
        out_shape=[jax.ShapeDtypeStruct((n_tot, GLA_V), BF16)] * 2,
        scratch_shapes=[pltpu.VMEM((GLA_V, GLA_QK), F32), pltpu.VMEM((GLA_V, GLA_QK), F32)],
        compiler_params=_params(("parallel", "arbitrary")),
        name="gla_scan",
    )(gla, lg, gla, lg)


def _diff_kernel(lam_init, latent, *refs):
    if latent:
        q_ref, kl_ref, vl_ref, kc_ref, vc_ref, lam_ref, g_ref, o_ref, *s_refs = refs
    else:
        q_ref, kc_ref, vc_ref, lam_ref, g_ref, o_ref, *s_refs = refs
    q = q_ref[...]
    t, hw = q.shape
    c_len = kc_ref.shape[0]
    lane = lax.broadcasted_iota(jnp.int32, q.shape, 1)
    zero = jnp.zeros_like(q)
    qm = (jnp.where(lane < DIFF_HD, q, zero), jnp.where(lane >= DIFF_HD, q, zero))

    tiles = [(lambda: kc_ref[...], lambda: vc_ref[0], 0, c_len)]
    if latent:
        tiles += [(lambda j=j: kl_ref[j * TOK_TILE:(j + 1) * TOK_TILE, :], lambda j=j: vl_ref[j],
                   c_len + j * TOK_TILE, TOK_TILE) for j in range(vl_ref.shape[0])]

    gw = min(t, ATT_TILE)
    groups = [(mp, c0) for mp in range(2) for c0 in range(0, t, gw)]

    def pass1(g, tile, m):
        mp, c0 = groups[g]
        load_k, _, off, n = tile
        s = _dot_nt(load_k(), qm[mp][c0:c0 + gw])
        s_refs[g][off:off + n, :] = s
        return jnp.maximum(m, jnp.max(s, axis=0, keepdims=True))

    def pass2(g, tile, m, l, acc):
        _, load_vt, off, n = tile
        p = jnp.exp2(s_refs[g][off:off + n, :] - m)
        return l + jnp.sum(p, axis=0, keepdims=True), acc + _dot(load_vt(), p.astype(BF16))

    m_prev, outs = None, []
    for g in range(len(groups) + 1):
        m = jnp.full((1, gw), -jnp.inf, F32)
        l, acc = jnp.zeros((1, gw), F32), jnp.zeros((hw, gw), F32)
        for tile in tiles:
            if g < len(groups):
                m = pass1(g, tile, m)
            if g > 0:
                l, acc = pass2(g - 1, tile, m_prev, l, acc)
        if g > 0:
            outs.append(acc / l)
        m_prev = m
    n_half = len(groups) // 2
    on = [jnp.concatenate(outs[i * n_half:(i + 1) * n_half], axis=1) for i in range(2)]

    lam = lam_ref[...]
    lam_full = (jnp.exp(jnp.sum(lam[0:1] * lam[1:2], axis=-1, keepdims=True))
                - jnp.exp(jnp.sum(lam[2:3] * lam[3:4], axis=-1, keepdims=True)) + lam_init)
    o = jnp.transpose(on[0] - lam_full * on[1])
    o = o * lax.rsqrt(jnp.mean(o * o, axis=-1, keepdims=True) + RMS_EPS) * g_ref[...]
    o_ref[...] = (o * (1.0 - lam_init)).astype(BF16)


def _diff_attention(dq, dk, dvt, lam, g_diff, layer, with_ctx, dims):
    n_tot = dq.shape[0]
    nb, s_len = dims["B"], dims["S"]
    n_lat_q = s_len // DIFF_QT
    lat_blocks = dims["n_lat"] // ATT_TILE
    lat_tiles = dims["n_lat"] // TOK_TILE
    ctx_per_tile = TOK_TILE // ATT_TILE
    lam_init = 0.8 - 0.6 * math.exp(-0.3 * layer)
    hw = 2 * DIFF_HD
    out_shape = jax.ShapeDtypeStruct((dims["n_lat"], DIFF_W), BF16)

    k_ctx = pl.BlockSpec((ATT_TILE, hw), lambda b, h, *_: (lat_blocks + b, h))
    v_ctx = pl.BlockSpec((1, hw, ATT_TILE), lambda b, h, *_: (lat_tiles + b // ctx_per_tile, h, b % ctx_per_tile))
    small = [pl.BlockSpec((4, DIFF_HD), lambda *_: (0, 0)), pl.BlockSpec((1, hw), lambda *_: (0, 0))]
    q_lat = pl.BlockSpec((DIFF_QT, hw), lambda b, h, i: (b * n_lat_q + i, h))
    out = pl.pallas_call(
        functools.partial(_diff_kernel, lam_init, True),
        grid=(nb, DIFF_HEADS, n_lat_q),
        in_specs=[q_lat, pl.BlockSpec((s_len, hw), lambda b, h, i: (b, h)),
                  pl.BlockSpec((s_len // TOK_TILE, hw, TOK_TILE), lambda b, h, i: (b, h, 0)), k_ctx, v_ctx] + small,
        out_specs=q_lat,
        out_shape=out_shape,
        scratch_shapes=[pltpu.VMEM((ATT_TILE + s_len, ATT_TILE), F32)] * (2 * DIFF_QT // ATT_TILE),
        compiler_params=_params(("parallel", "parallel", "arbitrary")),
        name="diff_attention",
    )(dq, dk, dvt, dk, dvt, lam, g_diff)
    if not with_ctx:
        return out, None
    out_ctx = pl.pallas_call(
        functools.partial(_diff_kernel, lam_init, False),
        grid=(nb, DIFF_HEADS),
        in_specs=[pl.BlockSpec((ATT_TILE, hw), lambda b, h: (lat_blocks + b, h)), k_ctx, v_ctx] + small,
        out_specs=pl.BlockSpec((ATT_TILE, hw), lambda b, h: (b, h)),
        out_shape=jax.ShapeDtypeStruct((nb * ATT_TILE, DIFF_W), BF16),
        scratch_shapes=[pltpu.VMEM((ATT_TILE, ATT_TILE), F32)] * 2,
        compiler_params=_params(("parallel", "parallel")),
        name="diff_attention_context",
    )(dq, dk, dvt, lam, g_diff)
    return out, out_ctx


def _na_bias_tables(rpb, rows):
    wr = min(WIN_R, rows)
    n_groups = rows // NA_QROWS
    qc = np.arange(GRID_W)[None, :]
    kc = np.arange(GRID_W)[:, None]
    cs = np.clip(qc - WIN_C // 2, 0, GRID_W - WIN_C)
    col_valid = (kc >= cs) & (kc < cs + WIN_C)
    col_sel = ((kc - qc + WIN_C - 1)[None] == np.arange(2 * WIN_C - 1)[:, None, None]) & col_valid[None]
    row_sel = []
    for g in (0, 1, n_groups - 1):
        r0 = g * NA_QROWS
        u0 = int(np.clip(r0 - WIN_R // 2, 0, rows - NA_KROWS))
        r = r0 + np.arange(NA_QROWS)[None, :]
        kr = u0 + np.arange(NA_KROWS)[:, None]
        rs = np.clip(r - WIN_R // 2, 0, rows - wr)
        row_valid = (kr >= rs) & (kr < rs + wr)
        row_sel.append(((kr - r + WIN_R - 1)[None] == np.arange(2 * WIN_R - 1)[:, None, None]) & row_valid[None])
    row_sel = np.stack(row_sel)
    t = jnp.einsum("hrc,vrkd,cxq->vkxhdq", rpb.astype(F32) * LOG2E, jnp.asarray(row_sel, F32),
                   jnp.asarray(col_sel, F32), precision=HIGHEST)
    valid = row_sel.any(axis=1)[:, :, None, None, :, None] & col_valid[None, None, :, None, None, :]
    t = jnp.where(valid, t, NEG_BIG)
    return t.reshape(3, NA_KROWS // NA_QROWS, ATT_TILE, NA_HEADS * ATT_TILE)


def _na_kernel(rows, latent, *refs):
    if latent:
        q_ref, kl_ref, vl_ref, kc_ref, vc_ref, bias_ref, o_ref, *s_refs = refs
    else:
        q_ref, kc_ref, vc_ref, o_ref, *s_refs = refs
    q = q_ref[...]
    t = q.shape[0]
    head = lax.broadcasted_iota(jnp.int32, q.shape, 1) // NA_HD
    zero = jnp.zeros_like(q)
    qs = jnp.concatenate([jnp.where(head == h, q, zero) for h in range(NA_HEADS)], axis=0)

    n_local = NA_KROWS // NA_QROWS
    s = _dot_nt(kc_ref[...], qs)
    s_refs[0][...] = s
    m = jnp.max(s, axis=0, keepdims=True)
    values = [lambda: vc_ref[0]]
    if latent:
        t0 = jnp.clip(pl.program_id(1) - 1, 0, rows // NA_QROWS - n_local)
        k_local = kl_ref[pl.ds(pl.multiple_of(t0 * t, t), n_local * t), :]
        s = _dot_nt(k_local, qs) + bias_ref[0].reshape(n_local * t, NA_HEADS * t)
        for j in range(n_local):
            s_refs[1 + j][...] = s[j * t:(j + 1) * t]
            values.append(lambda j=j: vl_ref[t0 + j])
        m = jnp.maximum(m, jnp.max(s, axis=0, keepdims=True))
    l = jnp.zeros((1, NA_HEADS * t), F32)
    acc = [jnp.zeros((NA_HD, t), F32)] * NA_HEADS
    for i, load_vt in enumerate(values):
        p = jnp.exp2(s_refs[i][...] - m)
        l = l + jnp.sum(p, axis=0, keepdims=True)
        pb, vt = p.astype(BF16), load_vt()
        for h in range(NA_HEADS):
            acc[h] = acc[h] + _dot(vt[h * NA_HD:(h + 1) * NA_HD, :], pb[:, h * t:(h + 1) * t])
    o = jnp.concatenate([acc[h] / l[:, h * t:(h + 1) * t] for h in range(NA_HEADS)], axis=0)
    o_ref[...] = jnp.transpose(o).astype(BF16)


def _neighbourhood_attention(nq, nk, nvt, bias_tabs, with_ctx, dims):
    nb, s_len = dims["B"], dims["S"]
    rows = s_len // GRID_W
    n_groups = rows // NA_QROWS
    lat_blocks = dims["n_lat"] // ATT_TILE
    n_key_tiles = 1 + NA_KROWS // NA_QROWS

    def bias_map(b, g):
        return (jnp.where(g == 0, 0, jnp.where(g >= n_groups - 1, 2, 1)), 0, 0, 0)

    k_ctx = pl.BlockSpec((ATT_TILE, NA_W), lambda b, *_: (lat_blocks + b, 0))
    v_ctx = pl.BlockSpec((1, NA_W, ATT_TILE), lambda b, *_: (lat_blocks + b, 0, 0))
    q_lat = pl.BlockSpec((ATT_TILE, NA_W), lambda b, g: (b * n_groups + g, 0))
    scratch = lambda n: [pltpu.VMEM((ATT_TILE, NA_HEADS * ATT_TILE), F32)] * n
    lat = pl.pallas_call(
        functools.partial(_na_kernel, rows, True),
        grid=(nb, n_groups),
        in_specs=[q_lat, pl.BlockSpec((s_len, NA_W), lambda b, g: (b, 0)),
                  pl.BlockSpec((s_len // ATT_TILE, NA_W, ATT_TILE), lambda b, g: (b, 0, 0)), k_ctx, v_ctx,
                  pl.BlockSpec((1,) + bias_tabs.shape[1:], bias_map)],
        out_specs=q_lat,
        out_shape=jax.ShapeDtypeStruct((dims["n_lat"], NA_W), BF16),
        scratch_shapes=scratch(n_key_tiles),
        compiler_params=_params(("parallel", "arbitrary")),
        name="neighbourhood_attention",
    )(nq, nk, nvt, nk, nvt, bias_tabs)
    if not with_ctx:
        return lat, None
    ctx = pl.pallas_call(
        functools.partial(_na_kernel, rows, False),
        grid=(nb,),
        in_specs=[pl.BlockSpec((ATT_TILE, NA_W), lambda b: (lat_blocks + b, 0)), k_ctx, v_ctx],
        out_specs=pl.BlockSpec((ATT_TILE, NA_W), lambda b: (b, 0)),
        out_shape=jax.ShapeDtypeStruct((nb * ATT_TILE, NA_W), BF16),
        scratch_shapes=scratch(1),
        compiler_params=_params(("parallel",)),
        name="context_attention",
    )(nq, nk, nvt)
    return lat, ctx


def _route_top2(logits):
    lane = lax.broadcasted_iota(jnp.int32, logits.shape, 1)
    lane_f = lane.astype(F32)
    lg = jnp.where(lane < N_EXPERTS, logits, -jnp.inf)
    m1 = jnp.max(lg, axis=-1, keepdims=True)
    i1 = jnp.min(jnp.where(lg == m1, lane_f, float(LANES)), axis=-1, keepdims=True)
    lg2 = jnp.where(lane_f == i1, -jnp.inf, lg)
    m2 = jnp.max(lg2, axis=-1, keepdims=True)
    i2 = jnp.min(jnp.where(lg2 == m2, lane_f, float(LANES)), axis=-1, keepdims=True)
    e = jnp.exp(m2 - m1)
    w1 = 1.0 / (1.0 + e)
    w2 = e / (1.0 + e)
    return jnp.where(lane == 0, i1, jnp.where(lane == 1, i2, jnp.where(lane == 2, w1,
                                                                         jnp.where(lane == 3, w2, 0.0))))


def _outproj_kernel(moe, n_lat_tiles, n_x, *refs):
    x_in = _joined_rows(n_lat_tiles, refs[:n_x])
    of_ref, ob_ref, gate_ref, bd_ref, cn_ref, *rest = refs[n_x:]
    if moe:
        mod_ref, ggla_ref, g2_ref, w_ref, wr_ref, x_out_ref, h_ref, route_ref = rest
        b_diff, c_na = bd_ref[...], cn_ref[...]
    else:
        bdc_ref, cnc_ref, mod_ref, ggla_ref, g2_ref, w_ref, x_out_ref, h_ref = rest
        is_lat = pl.program_id(0) < n_lat_tiles
        b_diff = jnp.where(is_lat, bd_ref[...], bdc_ref[...])
        c_na = jnp.where(is_lat, cn_ref[...], cnc_ref[...])
    m = mod_ref[0, 0]
    o = of_ref[...].astype(F32) + ob_ref[...].astype(F32)
    avg = (lax.broadcasted_iota(jnp.int32, (GLA_V, GLA_V), 0) // GLA_DV
           == lax.broadcasted_iota(jnp.int32, (GLA_V, GLA_V), 1) // GLA_DV).astype(F32) * (1.0 / GLA_DV)
    ms = jnp.dot(o * o, avg, precision=HIGHEST, preferred_element_type=F32)
    a = o * lax.rsqrt(ms + RMS_EPS) * ggla_ref[...] * _silu(gate_ref[...].astype(F32))
    y = (_dot(a.astype(BF16), w_ref[0:GLA_V]) + _dot(b_diff, w_ref[GLA_V:GLA_V + DIFF_W])
         + _dot(c_na, w_ref[GLA_V + DIFF_W:]))
    x = x_in + m[2:3] * y
    x_out_ref[...] = x
    h = x * lax.rsqrt(jnp.mean(x * x, axis=-1, keepdims=True) + RMS_EPS) * g2_ref[...]
    h = h * (1.0 + m[4:5]) + m[3:4]
    h_ref[...] = h.astype(h_ref.dtype)
    if moe:
        h_hi, w = h.astype(BF16), wr_ref[...]
        h_lo = (h - h_hi.astype(F32)).astype(BF16)
        w_hi = w.astype(BF16)
        w_lo = (w - w_hi.astype(F32)).astype(BF16)
        route_ref[...] = _route_top2(_dot(h_hi, w_hi) + (_dot(h_hi, w_lo) + _dot(h_lo, w_hi)))


def _output_projection(x_all, o_f, o_b, gla, b_diff, c_na, mod, layer, g_gla, g2, w_out, w_router, n_rows, dims):
    d = w_out.shape[1]
    n_lat_tiles = dims["n_lat"] // TOK_TILE
    tiles_per_batch = dims["S"] // TOK_TILE
    nb = dims["B"]
    moe = w_router is not None
    assert moe == (b_diff[1] is None) and n_rows == (dims["n_lat"] if moe else dims["n_tot"])
    x_args, x_specs = _split_rows(x_all, n_lat_tiles, d)

    def mod_map(i):
        return (layer, jnp.where(i < n_lat_tiles, i // tiles_per_batch, nb), 0, 0)

    row = lambda w, cb=0: pl.BlockSpec((TOK_TILE, w), lambda i: (i, cb))
    lat_row = lambda w: pl.BlockSpec((TOK_TILE, w), lambda i: (jnp.minimum(i, n_lat_tiles - 1), 0))
    ctx_row = lambda w: pl.BlockSpec((TOK_TILE, w), lambda i: (jnp.maximum(i - n_lat_tiles, 0), 0))
    const = lambda shape: pl.BlockSpec(shape, lambda i: (0,) * len(shape))
    in_specs = x_specs + [row(GLA_V), row(GLA_V), row(GLA_V, (2 * GLA_QK + GLA_V) // GLA_V),
                          lat_row(DIFF_W), lat_row(NA_W)]
    args = x_args + [o_f, o_b, gla, b_diff[0], c_na[0]]
    if not moe:
        in_specs += [ctx_row(DIFF_W), ctx_row(NA_W)]
        args += [b_diff[1], c_na[1]]
    in_specs += [pl.BlockSpec((1, 1, 6, d), mod_map), const((1, GLA_V)), const((1, d)), const((d, d))]
    args += [mod, g_gla, g2, w_out]
    out_specs = [row(d), row(d)]
    out_shape = [jax.ShapeDtypeStruct((n_rows, d), F32), jax.ShapeDtypeStruct((n_rows, d), F32 if moe else BF16)]
    if moe:
        in_specs.append(const((d, LANES)))
        args.append(w_router)
        out_specs.append(row(LANES))
        out_shape.append(jax.ShapeDtypeStruct((n_rows, LANES), F32))
    return pl.pallas_call(
        functools.partial(_outproj_kernel, moe, n_lat_tiles, len(x_args)),
        grid=(n_rows // TOK_TILE,),
        in_specs=in_specs, out_specs=out_specs, out_shape=out_shape,
        compiler_params=_params(("parallel",)),
        name="output_projection",
    )(*args)


def _ffn_kernel(h_ref, x_ref, mod_ref, wg_ref, wu_ref, wd_ref, o_ref, acc_ref):
    h = h_ref[...]
    n_chunks = wg_ref.shape[1] // FFN_CHUNK
    for j in range(n_chunks):
        cols = slice(j * FFN_CHUNK, (j + 1) * FFN_CHUNK)
        act = (_silu(_dot(h, wg_ref[:, cols])) * _dot(h, wu_ref[:, cols])).astype(BF16)
        part = _dot(act, wd_ref[cols, :])
        if j == 0:
            acc_ref[...] = part
        else:
            acc_ref[...] += part
    o_ref[...] = x_ref[...] + mod_ref[0, 0][5:6] * acc_ref[...]


def _dense_ffn(h, x_all, mod, layer, wg, wu, wd, dims):
    n_rows, d = x_all.shape
    f = wg.shape[1]
    n_lat_tiles = dims["n_lat"] // TOK_TILE
    tiles_per_batch = dims["S"] // TOK_TILE
    nb = dims["B"]

    def mod_map(i):
        return (layer, jnp.where(i < n_lat_tiles, i // tiles_per_batch, nb), 0, 0)

    row = lambda: pl.BlockSpec((TOK_TILE, d), lambda i: (i, 0))
    const = lambda shape: pl.BlockSpec(shape, lambda i: (0, 0))
    return pl.pallas_call(
        _ffn_kernel,
        grid=(n_rows // TOK_TILE,),
        in_specs=[row(), row(), pl.BlockSpec((1, 1, 6, d), mod_map), const((d, f)), const((d, f)), const((f, d))],
        out_specs=row(),
        out_shape=jax.ShapeDtypeStruct((n_rows, d), F32),
        scratch_shapes=[pltpu.VMEM((TOK_TILE, d), F32)],
        compiler_params=_params(("parallel",)),
        name="dense_swiglu",
    )(h, x_all, mod, wg, wu, wd)


def _moe_routing(route, n_tok):
    e_flat = jnp.concatenate([route[:, 0], route[:, 1]]).astype(jnp.int32)
    onehot = (e_flat[:, None] == jnp.arange(N_EXPERTS, dtype=jnp.int32)[None, :]).astype(jnp.int32)
    csum = jnp.cumsum(onehot, axis=0)
    counts = csum[-1]
    padded = ((counts + MOE_TILE - 1) // MOE_TILE) * MOE_TILE
    ends = jnp.cumsum(padded)
    pos = jnp.sum(onehot * (csum - 1 + (ends - padded)[None, :]), axis=1)
    n_rows = 2 * n_tok + N_EXPERTS * MOE_TILE
    tile_start = jnp.arange(n_rows // MOE_TILE, dtype=jnp.int32) * MOE_TILE
    tile_expert = jnp.minimum(jnp.sum((tile_start[:, None] >= ends[None, :]).astype(jnp.int32), axis=1),
                              N_EXPERTS - 1).astype(jnp.int32)
    n_active = (ends[-1:] // MOE_TILE).astype(jnp.int32)
    unused = ends[-1] + jnp.arange(N_EXPERTS, dtype=jnp.int32) * MOE_TILE
    tails = jnp.concatenate([jnp.where(padded > 0, ends - MOE_TILE, -1),
                             jnp.where(unused < n_rows, unused, -1)]).astype(jnp.int32)
    n_tiles = n_tok // TOK_TILE
    pos = jnp.concatenate([pos[:n_tok].reshape(n_tiles, 1, TOK_TILE), pos[n_tok:].reshape(n_tiles, 1, TOK_TILE)],
                          axis=2)
    return pos, tails, n_rows, tile_expert, n_active


def _dispatch_kernel(tail_ref, pos_ref, h_ref, xs_out, zero_ref, sem):
    tile = h_ref.shape[0]

    @pl.when(pl.program_id(0) == 0)
    def _():
        zero_ref[...] = jnp.zeros_like(zero_ref)
        for e in range(2 * N_EXPERTS):
            @pl.when(tail_ref[e] >= 0)
            def _():
                rows = pl.ds(pl.multiple_of(tail_ref[e], MOE_TILE), MOE_TILE)
                tail = pltpu.make_async_copy(zero_ref, xs_out.at[rows], sem)
                tail.start()
                tail.wait()

    def issue(r, carry):
        for k in range(2):
            pltpu.make_async_copy(h_ref.at[pl.ds(r, 1)], xs_out.at[pl.ds(pos_ref[0, 0, k * tile + r], 1)],
                                  sem).start()
        return carry
    lax.fori_loop(0, tile, issue, 0, unroll=8)
    for k in range(2):
        pltpu.make_async_copy(h_ref, xs_out.at[pl.ds(0, tile)], sem).wait()


def _moe_dispatch(h, pos, tails, n_rows):
    n_tok, d = h.shape
    grid_spec = pltpu.PrefetchScalarGridSpec(
        num_scalar_prefetch=1,
        grid=(n_tok // TOK_TILE,),
        in_specs=[pl.BlockSpec((1, 1, 2 * TOK_TILE), lambda i, tl: (i, 0, 0), memory_space=pltpu.SMEM),
                  pl.BlockSpec((TOK_TILE, d), lambda i, tl: (i, 0))],
        out_specs=pl.BlockSpec(memory_space=pl.ANY),
        scratch_shapes=[pltpu.VMEM((MOE_TILE, d), h.dtype), pltpu.SemaphoreType.DMA(())])
    return pl.pallas_call(
        _dispatch_kernel,
        grid_spec=grid_spec,
        out_shape=jax.ShapeDtypeStruct((n_rows, d), h.dtype),
        compiler_params=_params(("arbitrary",)),
        name="expert_dispatch",
    )(tails, pos, h)


def _moe_kernel(te_ref, nact_ref, x_ref, wg_ref, wu_ref, wd_ref, o_ref, xb_ref, acc_ref):
    i, f = pl.program_id(0), pl.program_id(1)
    active = i < nact_ref[0]
    last = f == pl.num_programs(1) - 1

    @pl.when(f == 0)
    def _():
        xb_ref[...] = x_ref[...].astype(BF16)
        acc_ref[...] = jnp.zeros_like(acc_ref)

    @pl.when(active)
    def _():
        x = xb_ref[...]
        for j in range(wg_ref.shape[2] // FFN_CHUNK):
            cols = slice(j * FFN_CHUNK, (j + 1) * FFN_CHUNK)
            act = (_silu(_dot(x, wg_ref[0, :, cols])) * _dot(x, wu_ref[0, :, cols])).astype(BF16)
            acc_ref[...] += _dot(act, wd_ref[0, cols, :])

    @pl.when(last)
    def _():
        o_ref[...] = acc_ref[...]


def _moe_ffn(x_sorted, tile_expert, n_active, wg, wu, wd):
    n_rows, d = x_sorted.shape
    ffn = wg.shape[2]
    n_tiles = n_rows // MOE_TILE
    n_f = ffn // MOE_FCHUNK

    def row(i, f, te, na):
        return jnp.minimum(i, na[0] - 1)

    def fcol(i, f, te, na):
        return jnp.where(i < na[0], f, n_f - 1)

    grid_spec = pltpu.PrefetchScalarGridSpec(
        num_scalar_prefetch=2,
        grid=(n_tiles, n_f),
        in_specs=[pl.BlockSpec((MOE_TILE, d), lambda i, f, te, na: (row(i, f, te, na), 0)),
                  pl.BlockSpec((1, d, MOE_FCHUNK), lambda i, f, te, na: (te[i], 0, fcol(i, f, te, na))),
                  pl.BlockSpec((1, d, MOE_FCHUNK), lambda i, f, te, na: (te[i], 0, fcol(i, f, te, na))),
                  pl.BlockSpec((1, MOE_FCHUNK, d), lambda i, f, te, na: (te[i], fcol(i, f, te, na), 0))],
        out_specs=pl.BlockSpec((MOE_TILE, d), lambda i, f, te, na: (i, 0)),
        scratch_shapes=[pltpu.VMEM((MOE_TILE, d), BF16), pltpu.VMEM((MOE_TILE, d), F32)])
    return pl.pallas_call(
        _moe_kernel,
        grid_spec=grid_spec,
        out_shape=jax.ShapeDtypeStruct((n_rows, d), F32),
        compiler_params=_params(("arbitrary", "arbitrary")),
        name="expert_swiglu",
    )(tile_expert, n_active, x_sorted, wg, wu, wd)


def _final_kernel(pos_ref, next_pos_ref, x_ref, route_ref, mod_ref, g_ref, y_hbm, o_ref, buf_ref, sem):
    tile = x_ref.shape[0]
    i, n = pl.program_id(0), pl.num_programs(0)
    slot = i % 2

    def fetch(p_ref, s):
        def issue(r, carry):
            for k in range(2):
                pltpu.make_async_copy(y_hbm.at[pl.ds(p_ref[0, 0, k * tile + r], 1)],
                                      buf_ref.at[s, k, pl.ds(r, 1)], sem.at[s]).start()
            return carry
        lax.fori_loop(0, tile, issue, 0, unroll=8)

    @pl.when(i == 0)
    def _():
        fetch(pos_ref, slot)

    @pl.when(i + 1 < n)
    def _():
        fetch(next_pos_ref, 1 - slot)

    for k in range(2):
        pltpu.make_async_copy(y_hbm.at[pl.ds(0, tile)], buf_ref.at[slot, k], sem.at[slot]).wait()
    route = route_ref[...]
    y = route[:, 2:3] * buf_ref[slot, 0] + route[:, 3:4] * buf_ref[slot, 1]
    x = x_ref[...] + mod_ref[0, 0][5:6] * y
    o_ref[...] = x * lax.rsqrt(jnp.mean(x * x, axis=-1, keepdims=True) + RMS_EPS) * g_ref[...]


def _final_combine(x_lat, y_sorted, pos, route, mod, layer, g_final, dims):
    n_rows, d = x_lat.shape
    n_tiles = n_rows // TOK_TILE
    tiles_per_batch = dims["S"] // TOK_TILE
    return pl.pallas_call(
        _final_kernel,
        grid=(n_tiles,),
        in_specs=[pl.BlockSpec((1, 1, 2 * TOK_TILE), lambda i: (i, 0, 0), memory_space=pltpu.SMEM),
                  pl.BlockSpec((1, 1, 2 * TOK_TILE), lambda i: (jnp.minimum(i + 1, n_tiles - 1), 0, 0),
                               memory_space=pltpu.SMEM),
                  pl.BlockSpec((TOK_TILE, d), lambda i: (i, 0)),
                  pl.BlockSpec((TOK_TILE, LANES), lambda i: (i, 0)),
                  pl.BlockSpec((1, 1, 6, d), lambda i: (layer, i // tiles_per_batch, 0, 0)),
                  pl.BlockSpec((1, d), lambda i: (0, 0)),
                  pl.BlockSpec(memory_space=pl.ANY)],
        out_specs=pl.BlockSpec((TOK_TILE, d), lambda i: (i, 0)),
        out_shape=jax.ShapeDtypeStruct((n_rows, d), F32),
        scratch_shapes=[pltpu.VMEM((2, 2, TOK_TILE, d), F32), pltpu.SemaphoreType.DMA((2,))],
        compiler_params=_params(("arbitrary",)),
        name="expert_combine_final_norm",
    )(pos, pos, x_lat, route, mod, g_final, y_sorted)


def _rope_tables(s_len):
    t = np.arange(s_len)
    lane = np.arange(LANES) % HEAD_DIM
    quarter = HEAD_DIM // 4
    inv = 1.0 / (ROPE_BASE ** (jnp.arange(quarter, dtype=F32) / quarter))
    pos = np.where(lane[None, :] < HEAD_DIM // 2, (t // GRID_W)[:, None], (t % GRID_W)[:, None]).astype(np.float32)
    ang = jnp.asarray(pos) * inv[lane % quarter][None, :]
    first = jnp.asarray((lane % (HEAD_DIM // 2)) < quarter)[None, :]
    cos, sin = jnp.cos(ang), jnp.sin(ang)
    ident = jnp.zeros((TOK_TILE, LANES), F32)
    return (jnp.concatenate([cos, ident + 1.0]),
            jnp.concatenate([jnp.where(first, -sin, 0.0), ident]),
            jnp.concatenate([jnp.where(first, 0.0, sin), ident]))


def _rearranged_w_in(w):
    offs = np.concatenate([[0], np.cumsum(IN_SIZES)])
    part = lambda j: w[:, int(offs[j]):int(offs[j + 1])]
    pad = jnp.zeros((w.shape[0], LANES - 2 * GLA_LR), w.dtype)
    order = [0, 1, 2, 3, 6, 7, 9, 10, 4, 5]
    w_cat = jnp.concatenate([part(j) for j in order] + [pad], axis=1).astype(BF16)
    return w_cat, jnp.transpose(part(8)).astype(BF16), jnp.transpose(part(11)).astype(BF16)


def _decay_up(w_up, b):
    top = jnp.concatenate([w_up[0], jnp.zeros_like(w_up[0])], axis=1)
    bot = jnp.concatenate([jnp.zeros_like(w_up[1]), w_up[1]], axis=1)
    pad = jnp.zeros((LANES - 2 * GLA_LR, 2 * GLA_QK), w_up.dtype)
    return jnp.concatenate([top, bot, pad], axis=0).astype(BF16), jnp.concatenate([b[0], b[1]])[None, :]


def kernel(x, c, ctx, c_ctx, w_mod, b_mod, g_norm1, g_norm2, w_in, gla_w_dec_up, gla_b_dec, gla_g_norm,
           diff_lambda, diff_g_norm, na_rpb, w_out, w_ffn_gate, w_ffn_up, w_ffn_down, w_router, w_moe_gate,
           w_moe_up, w_moe_down, g_final):
    nb, s_len, d = x.shape
    c_len = ctx.shape[1]
    depth = w_mod.shape[0]
    assert c_len == ATT_TILE and s_len % TOK_TILE == 0 and (nb * c_len) % TOK_TILE == 0 and nb < MOD_ROWS
    n_lat = nb * s_len
    dims = {"B": nb, "S": s_len, "n_lat": n_lat, "n_tot": n_lat + nb * c_len}

    x_all = (x.reshape(n_lat, d), ctx.reshape(nb * c_len, d))
    cc = jnp.concatenate([c, c_ctx[None, :], jnp.zeros((MOD_ROWS - nb - 1, d), F32)], axis=0)
    mod = _modulation(cc, w_mod, b_mod).reshape(depth, MOD_ROWS, 6, d)
    rope_tabs = _rope_tables(s_len)

    for l in range(depth):
        last = l == depth - 1
        w_cat, w_dvt, w_nvt = _rearranged_w_in(w_in[l])
        wup2, bdec2 = _decay_up(gla_w_dec_up[l], gla_b_dec[l])
        gla, lg, dq, dk, dvt, nq, nk, nvt = _input_projection(
            x_all, mod, l, g_norm1[l][None, :], w_cat, w_dvt, w_nvt, wup2, bdec2, rope_tabs, dims)
        o_f, o_b = _gla(gla, lg, dims)
        b_diff = _diff_attention(dq, dk, dvt, diff_lambda[l], diff_g_norm[l][None, :], l, not last, dims)
        c_na = _neighbourhood_attention(nq, nk, nvt, _na_bias_tables(na_rpb[l], s_len // GRID_W), not last, dims)
        g_gla = jnp.tile(gla_g_norm[l], GLA_HEADS)[None, :]
        n_rows = n_lat if last else dims["n_tot"]
        if l % 2 == 0:
            i = l // 2
            x_mid, h = _output_projection(x_all, o_f, o_b, gla, b_diff, c_na, mod, l, g_gla, g_norm2[l][None, :],
                                          w_out[l].astype(BF16), None, n_rows, dims)
            assert not last, "the dense channel mixer is implemented for non-final layers only"
            x_all = _dense_ffn(h, x_mid, mod, l, w_ffn_gate[i].astype(BF16), w_ffn_up[i].astype(BF16),
                               w_ffn_down[i].astype(BF16), dims)
        else:
            i = l // 2
            w_r = jnp.concatenate([w_router[i], jnp.zeros((d, LANES - N_EXPERTS), F32)], axis=1)
            x_mid, h, route = _output_projection(x_all, o_f, o_b, gla, b_diff, c_na, mod, l, g_gla,
                                                 g_norm2[l][None, :], w_out[l].astype(BF16), w_r, n_rows, dims)
            assert last, "the expert layer is implemented for the final layer only"
            pos, tails, n_sorted, tile_expert, n_active = _moe_routing(route, n_rows)
            x_sorted = _moe_dispatch(h, pos, tails, n_sorted)
            y_sorted = _moe_ffn(x_sorted, tile_expert, n_active, w_moe_gate[i].astype(BF16),
                                w_moe_up[i].astype(BF16), w_moe_down[i].astype(BF16))
            return _final_combine(x_mid, y_sorted, pos, route, mod, l, g_final[None, :], dims).reshape(nb, s_len, d)
    raise NotImplementedError("the final layer must be an expert layer")
```

```python
import functools
import math

import numpy as np
import jax
import jax.numpy as jnp
from jax import lax
from jax.experimental import pallas as pl
from jax.experimental.pallas import tpu as pltpu

F32 = jnp.float32
BF16 = jnp.bfloat16
HIGHEST = lax.Precision.HIGHEST

GRID_W = 64
HEAD_DIM = 64
ROPE_BASE = 10000.0
RMS_EPS = 1e-6
GLA_HEADS = 4
GLA_DK = 32
GLA_DV = 64
GLA_LR = 16
GLA_NORMALIZER = 16.0
GLA_CHUNK = 64
DIFF_HEADS = 4
DIFF_HD = 64
NA_HEADS = 4
NA_HD = 64
WIN_R = 8
WIN_C = 16
N_EXPERTS = 8
GLA_QK = GLA_HEADS * GLA_DK
GLA_V = GLA_HEADS * GLA_DV
DIFF_W = DIFF_HEADS * 2 * DIFF_HD
NA_W = NA_HEADS * NA_HD
IN_SIZES = (GLA_QK, GLA_QK, GLA_V, GLA_V, GLA_LR, GLA_LR, DIFF_W, DIFF_W, DIFF_W, NA_W, NA_W, NA_W)

LANES = 128
TOK_TILE = 512
ATT_TILE = 256
DIFF_QT = 1024
DIFF_GW = 256
NA_QROWS = 4
NA_KROWS = 12
FFN_CHUNK = 256
MOE_TILE = 512
MOE_FCHUNK = 1792
MOD_ROWS = 16
NEG_BIG = -1e30
LOG2E = math.log2(math.e)
VMEM_LIMIT = 56 * 1024 * 1024

C_GLA = 0
C_DQ = C_GLA + 2 * GLA_QK + 2 * GLA_V
C_DK = C_DQ + DIFF_W
C_NQ = C_DK + DIFF_W
C_NK = C_NQ + NA_W
C_DEC = C_NK + NA_W
C_END = C_DEC + LANES


def _silu(x):
    return x * (1.0 / (1.0 + jnp.exp(-x)))


def _dot(a, b):
    return jnp.dot(a, b, preferred_element_type=F32)


def _dot_nt(a, b):
    return lax.dot_general(a, b, (((1,), (1,)), ((), ())), preferred_element_type=F32)


def _dot_tn(a, b):
    return lax.dot_general(a, b, (((0,), (0,)), ((), ())), preferred_element_type=F32)


def _params(sem, vmem=VMEM_LIMIT):
    return pltpu.CompilerParams(dimension_semantics=sem, vmem_limit_bytes=vmem)


def _mod_kernel(cc_ref, w_ref, b_ref, o_ref):
    s = _silu(cc_ref[...]).astype(BF16)
    o_ref[0] = _dot(s, w_ref[0].astype(BF16)) + b_ref[0]


def _modulation(cc, w_mod, b_mod):
    depth, d, six_d = w_mod.shape
    n = six_d // d
    return pl.pallas_call(
        _mod_kernel,
        grid=(depth, n),
        in_specs=[pl.BlockSpec((MOD_ROWS, d), lambda l, j: (0, 0)),
                  pl.BlockSpec((1, d, d), lambda l, j: (l, 0, j)),
                  pl.BlockSpec((1, 1, d), lambda l, j: (l, 0, j))],
        out_specs=pl.BlockSpec((1, MOD_ROWS, d), lambda l, j: (l, 0, j)),
        out_shape=jax.ShapeDtypeStruct((depth, MOD_ROWS, six_d), F32),
        compiler_params=_params(("parallel", "parallel")),
        name="modulation",
    )(cc, w_mod, b_mod.reshape(depth, 1, six_d))


def _rope(z, a, bt, ct):
    outs = []
    for s in range(z.shape[1] // LANES):
        zs = z[:, s * LANES:(s + 1) * LANES]
        outs.append(zs * a + pltpu.roll(zs, LANES - 16, 1) * bt + pltpu.roll(zs, 16, 1) * ct)
    return jnp.concatenate(outs, axis=1)


def _split_rows(x, n_lat_tiles, width):
    if not isinstance(x, tuple):
        return [x], [pl.BlockSpec((TOK_TILE, width), lambda i: (i, 0))]
    return list(x), [pl.BlockSpec((TOK_TILE, width), lambda i: (jnp.minimum(i, n_lat_tiles - 1), 0)),
                     pl.BlockSpec((TOK_TILE, width), lambda i: (jnp.maximum(i - n_lat_tiles, 0), 0))]


def _joined_rows(n_lat_tiles, refs):
    if len(refs) == 1:
        return refs[0][...]
    return jnp.where(pl.program_id(0) < n_lat_tiles, refs[0][...], refs[1][...])


def _inproj_kernel(n_lat_tiles, n_x, *refs):
    (mod_ref, g_ref, w_ref, wvt_ref, wnt_ref, wup_ref, bdec_ref, ra_ref, rb_ref, rc_ref,
     gla_ref, lg_ref, dq_ref, dk_ref, dvt_ref, nq_ref, nk_ref, nvt_ref) = refs[n_x:]
    x = _joined_rows(n_lat_tiles, refs[:n_x])
    m = mod_ref[0, 0]
    h = x * lax.rsqrt(jnp.mean(x * x, axis=-1, keepdims=True) + RMS_EPS) * g_ref[...]
    hb = (h * (1.0 + m[1:2]) + m[0:1]).astype(BF16)

    def proj(c0, c1):
        return _dot(hb, w_ref[:, c0:c1])

    gla_ref[...] = proj(C_GLA, C_DQ).astype(BF16)
    a, bt, ct = ra_ref[...], rb_ref[...], rc_ref[...]
    dq_ref[...] = (_rope(proj(C_DQ, C_DK), a, bt, ct) * (DIFF_HD ** -0.5 * LOG2E)).astype(BF16)
    dk_ref[...] = _rope(proj(C_DK, C_NQ), a, bt, ct).astype(BF16)
    dvt_ref[0] = _dot_nt(wvt_ref[...], hb).astype(BF16)
    nq_ref[...] = (proj(C_NQ, C_NK) * (NA_HD ** -0.5 * LOG2E)).astype(BF16)
    nk_ref[...] = proj(C_NK, C_DEC).astype(BF16)
    for j in range(nvt_ref.shape[0]):
        nvt_ref[j] = _dot_nt(wnt_ref[...], hb[j * ATT_TILE:(j + 1) * ATT_TILE]).astype(BF16)
    logits = _dot(proj(C_DEC, C_END).astype(BF16), wup_ref[...]) + bdec_ref[...]
    log_sig = jnp.minimum(logits, 0.0) - jnp.log1p(jnp.exp(-jnp.abs(logits)))
    lg_ref[...] = log_sig * (1.0 / GLA_NORMALIZER)


def _input_projection(x_all, mod, layer, g1, w_cat, w_dvt, w_nvt, wup2, bdec2, rope_tabs, dims):
    n_tot, d = dims["n_tot"], g1.shape[1]
    n_lat_tiles = dims["n_lat"] // TOK_TILE
    x_args, x_specs = _split_rows(x_all, n_lat_tiles, d)
    tiles_per_batch = dims["S"] // TOK_TILE
    n_tiles = n_tot // TOK_TILE
    nb = dims["B"]

    def mod_map(i):
        return (layer, jnp.where(i < n_lat_tiles, i // tiles_per_batch, nb), 0, 0)

    def rope_map(i):
        return (jnp.where(i < n_lat_tiles, i % tiles_per_batch, tiles_per_batch), 0)

    row = lambda w: pl.BlockSpec((TOK_TILE, w), lambda i: (i, 0))
    const = lambda shape: pl.BlockSpec(shape, lambda i: (0,) * len(shape))
    tab = pl.BlockSpec((TOK_TILE, LANES), rope_map)
    sub = TOK_TILE // ATT_TILE
    out_specs = [row(C_DQ), row(2 * GLA_QK), row(DIFF_W), row(DIFF_W),
                 pl.BlockSpec((1, DIFF_W, TOK_TILE), lambda i: (i, 0, 0)), row(NA_W), row(NA_W),
                 pl.BlockSpec((sub, NA_W, ATT_TILE), lambda i: (i, 0, 0))]
    shape = lambda w, dt=BF16: jax.ShapeDtypeStruct((n_tot, w), dt)
    out_shape = [shape(C_DQ), shape(2 * GLA_QK, F32), shape(DIFF_W), shape(DIFF_W),
                 jax.ShapeDtypeStruct((n_tiles, DIFF_W, TOK_TILE), BF16), shape(NA_W), shape(NA_W),
                 jax.ShapeDtypeStruct((n_tiles * sub, NA_W, ATT_TILE), BF16)]
    return pl.pallas_call(
        functools.partial(_inproj_kernel, n_lat_tiles, len(x_args)),
        grid=(n_tiles,),
        in_specs=x_specs + [pl.BlockSpec((1, 1, 6, d), mod_map), const((1, d)), const((d, C_END)),
                            const((DIFF_W, d)), const((NA_W, d)), const((LANES, 2 * GLA_QK)),
                            const((1, 2 * GLA_QK)), tab, tab, tab],
        out_specs=out_specs,
        out_shape=out_shape,
        compiler_params=_params(("parallel",)),
        name="input_projection",
    )(*x_args, mod, g1, w_cat, w_dvt, w_nvt, wup2, bdec2, *rope_tabs)


def _gla_block_local(g_ref, l_ref, fwd, consts):
    tri, same, tri4, head_k, head_v, bd = consts
    t = g_ref.shape[0]
    q = g_ref[:, 0:GLA_QK].astype(F32) * GLA_DK ** -0.5
    k = g_ref[:, GLA_QK:2 * GLA_QK].astype(F32)
    v = g_ref[:, 2 * GLA_QK:2 * GLA_QK + GLA_V]
    lcol = 0 if fwd else GLA_QK
    lg = l_ref[:, lcol:lcol + GLA_QK]
    lg_hi = lg.astype(BF16)
    lg_lo = (lg - lg_hi.astype(F32)).astype(BF16)
    b = _dot(tri, lg_hi) + _dot(tri, lg_lo)
    total = _dot(same, lg_hi) + _dot(same, lg_lo)
    q_dec = q * jnp.exp(b)
    k_inv = (k * jnp.exp(-b)).astype(BF16)
    k_end = (k * jnp.exp(total - b)).astype(BF16)
    qs = jnp.concatenate([jnp.where(head_k == h, q_dec, 0.0) for h in range(GLA_HEADS)], axis=0)
    att = jnp.where(tri4, _dot_nt(qs.astype(BF16), k_inv), 0.0)
    ov = _dot(att.astype(BF16), v)
    o = jnp.where(head_v == 0, ov[0:t], 0.0)
    for h in range(1, GLA_HEADS):
        o = o + jnp.where(head_v == h, ov[h * t:(h + 1) * t], 0.0)
    q_dec = q_dec.astype(BF16)
    chunks = []
    for c in range(t // GLA_CHUNK):
        rows = slice(c * GLA_CHUNK, (c + 1) * GLA_CHUNK)
        chunks.append((o[rows], q_dec[rows], jnp.where(bd, _dot_tn(v[rows], k_end[rows]), 0.0),
                       jnp.exp(total[c * GLA_CHUNK:c * GLA_CHUNK + 1])))
    return chunks


def _gla_kernel(gf_ref, lf_ref, gb_ref, lb_ref, of_ref, ob_ref, sf_ref, sb_ref):
    @pl.when(pl.program_id(1) == 0)
    def _():
        sf_ref[...] = jnp.zeros_like(sf_ref)
        sb_ref[...] = jnp.zeros_like(sb_ref)

    t = gf_ref.shape[0]
    r = lax.broadcasted_iota(jnp.int32, (t, t), 0)
    c = lax.broadcasted_iota(jnp.int32, (t, t), 1)
    r4 = lax.broadcasted_iota(jnp.int32, (GLA_HEADS * t, t), 0) % t
    c4 = lax.broadcasted_iota(jnp.int32, (GLA_HEADS * t, t), 1)
    same = r // GLA_CHUNK == c // GLA_CHUNK
    same4 = r4 // GLA_CHUNK == c4 // GLA_CHUNK
    head_k = lax.broadcasted_iota(jnp.int32, (1, GLA_QK), 1) // GLA_DK
    head_v = lax.broadcasted_iota(jnp.int32, (1, GLA_V), 1) // GLA_DV
    bd = (lax.broadcasted_iota(jnp.int32, (GLA_V, GLA_QK), 0) // GLA_DV
          == lax.broadcasted_iota(jnp.int32, (GLA_V, GLA_QK), 1) // GLA_DK)
    same_b = same.astype(BF16)
    lower = (jnp.logical_and(same, c <= r).astype(BF16), same_b, jnp.logical_and(same4, c4 <= r4), head_k, head_v, bd)
    upper = (jnp.logical_and(same, c >= r).astype(BF16), same_b, jnp.logical_and(same4, c4 >= r4), head_k, head_v, bd)
    n_chunks = t // GLA_CHUNK
    scans = ((gf_ref, lf_ref, of_ref, sf_ref, True, lower, list(range(n_chunks))),
             (gb_ref, lb_ref, ob_ref, sb_ref, False, upper, list(range(n_chunks - 1, -1, -1))))
    local = [_gla_block_local(g_ref, l_ref, fwd, consts) for g_ref, l_ref, _, _, fwd, consts, _ in scans]
    for (_, _, o_ref, s_ref, _, _, order), chunks in zip(scans, local):
        s = s_ref[...]
        for cidx, (o_intra, q_dec, inc, decay) in ((cidx, chunks[cidx]) for cidx in order):
            o = o_intra + _dot_nt(q_dec, s.astype(BF16))
            o_ref[cidx * GLA_CHUNK:(cidx + 1) * GLA_CHUNK, :] = o.astype(o_ref.dtype)
            s = decay * s + inc
        s_ref[...] = s


def _gla(gla, lg, dims):
    n_tot = gla.shape[0]
    nb, steps = dims["B"], 1 + dims["S"] // ATT_TILE
    lat_blocks = dims["n_lat"] // ATT_TILE
    per_batch = dims["S"] // ATT_TILE

    def fwd_map(b, i):
        return (jnp.where(i == 0, lat_blocks + b, b * per_batch + i - 1), 0)

    def bwd_map(b, i):
        return (jnp.where(i == 0, lat_blocks + b, b * per_batch + per_batch - i), 0)

    return pl.pallas_call(
        _gla_kernel,
        grid=(nb, steps),
        in_specs=[pl.BlockSpec((ATT_TILE, C_DQ), fwd_map), pl.BlockSpec((ATT_TILE, 2 * GLA_QK), fwd_map),
                  pl.BlockSpec((ATT_TILE, C_DQ), bwd_map), pl.BlockSpec((ATT_TILE, 2 * GLA_QK), bwd_map)],
        out_specs=[pl.BlockSpec((ATT_TILE, GLA_V), fwd_map), pl.BlockSpec((ATT_TILE, GLA_V), bwd_map)],
        out_shape=[jax.ShapeDtypeStruct((n_tot, GLA_V), BF16)] * 2,
        scratch_shapes=[pltpu.VMEM((GLA_V, GLA_QK), F32), pltpu.VMEM((GLA_V, GLA_QK), F32)],
        compiler_params=_params(("parallel", "arbitrary")),
        name="gla_scan",
    )(gla, lg, gla, lg)


def _diff_kernel(lam_init, latent, *refs):
    if latent:
        q_ref, kl_ref, vl_ref, kc_ref, vc_ref, lam_ref, g_ref, o_ref, *s_refs = refs
    else:
        q_ref, kc_ref, vc_ref, lam_ref, g_ref, o_ref, *s_refs = refs
    q = q_ref[...]
    t, hw = q.shape
    c_len = kc_ref.shape[0]
    lane = lax.broadcasted_iota(jnp.int32, q.shape, 1)
    zero = jnp.zeros_like(q)
    qm = (jnp.where(lane < DIFF_HD, q, zero), jnp.where(lane >= DIFF_HD, q, zero))

    tiles = [(lambda: kc_ref[...], lambda: vc_ref[0], 0, c_len)]
    if latent:
        tiles += [(lambda j=j: kl_ref[j * TOK_TILE:(j + 1) * TOK_TILE, :], lambda j=j: vl_ref[j],
                   c_len + j * TOK_TILE, TOK_TILE) for j in range(vl_ref.shape[0])]

    gw = min(t, DIFF_GW)
    groups = [(mp, c0) for mp in range(2) for c0 in range(0, t, gw)]

    def pass1(g, tile, m):
        mp, c0 = groups[g]
        load_k, _, off, n = tile
        s = _dot_nt(load_k(), qm[mp][c0:c0 + gw])
        s_refs[g][off:off + n, :] = s
        return jnp.maximum(m, jnp.max(s, axis=0, keepdims=True))

    def pass2(g, tile, m, l, acc):
        _, load_vt, off, n = tile
        p = jnp.exp2(s_refs[g][off:off + n, :] - m)
        return l + jnp.sum(p, axis=0, keepdims=True), acc + _dot(load_vt(), p.astype(BF16))

    m_prev, outs = None, []
    for g in range(len(groups) + 1):
        m = jnp.full((1, gw), -jnp.inf, F32)
        l, acc = jnp.zeros((1, gw), F32), jnp.zeros((hw, gw), F32)
        for tile in tiles:
            if g < len(groups):
                m = pass1(g, tile, m)
            if g > 0:
                l, acc = pass2(g - 1, tile, m_prev, l, acc)
        if g > 0:
            outs.append(acc / l)
        m_prev = m
    n_half = len(groups) // 2
    on = [jnp.concatenate(outs[i * n_half:(i + 1) * n_half], axis=1) for i in range(2)]

    lam = lam_ref[...]
    lam_full = (jnp.exp(jnp.sum(lam[0:1] * lam[1:2], axis=-1, keepdims=True))
                - jnp.exp(jnp.sum(lam[2:3] * lam[3:4], axis=-1, keepdims=True)) + lam_init)
    o = jnp.transpose(on[0] - lam_full * on[1])
    o = o * lax.rsqrt(jnp.mean(o * o, axis=-1, keepdims=True) + RMS_EPS) * g_ref[...]
    o_ref[...] = (o * (1.0 - lam_init)).astype(BF16)


def _diff_attention(dq, dk, dvt, lam, g_diff, layer, with_ctx, dims):
    n_tot = dq.shape[0]
    nb, s_len = dims["B"], dims["S"]
    n_lat_q = s_len // DIFF_QT
    lat_blocks = dims["n_lat"] // ATT_TILE
    lat_tiles = dims["n_lat"] // TOK_TILE
    ctx_per_tile = TOK_TILE // ATT_TILE
    lam_init = 0.8 - 0.6 * math.exp(-0.3 * layer)
    hw = 2 * DIFF_HD
    out_shape = jax.ShapeDtypeStruct((dims["n_lat"], DIFF_W), BF16)

    k_ctx = pl.BlockSpec((ATT_TILE, hw), lambda b, h, *_: (lat_blocks + b, h))
    v_ctx = pl.BlockSpec((1, hw, ATT_TILE), lambda b, h, *_: (lat_tiles + b // ctx_per_tile, h, b % ctx_per_tile))
    small = [pl.BlockSpec((4, DIFF_HD), lambda *_: (0, 0)), pl.BlockSpec((1, hw), lambda *_: (0, 0))]
    q_lat = pl.BlockSpec((DIFF_QT, hw), lambda b, h, i: (b * n_lat_q + i, h))
    out = pl.pallas_call(
        functools.partial(_diff_kernel, lam_init, True),
        grid=(nb, DIFF_HEADS, n_lat_q),
        in_specs=[q_lat, pl.BlockSpec((s_len, hw), lambda b, h, i: (b, h)),
                  pl.BlockSpec((s_len // TOK_TILE, hw, TOK_TILE), lambda b, h, i: (b, h, 0)), k_ctx, v_ctx] + small,
        out_specs=q_lat,
        out_shape=out_shape,
        scratch_shapes=[pltpu.VMEM((ATT_TILE + s_len, DIFF_GW), F32)] * (2 * DIFF_QT // DIFF_GW),
        compiler_params=_params(("parallel", "parallel", "arbitrary")),
        name="diff_attention",
    )(dq, dk, dvt, dk, dvt, lam, g_diff)
    if not with_ctx:
        return out, None
    out_ctx = pl.pallas_call(
        functools.partial(_diff_kernel, lam_init, False),
        grid=(nb, DIFF_HEADS),
        in_specs=[pl.BlockSpec((ATT_TILE, hw), lambda b, h: (lat_blocks + b, h)), k_ctx, v_ctx] + small,
        out_specs=pl.BlockSpec((ATT_TILE, hw), lambda b, h: (b, h)),
        out_shape=jax.ShapeDtypeStruct((nb * ATT_TILE, DIFF_W), BF16),
        scratch_shapes=[pltpu.VMEM((ATT_TILE, ATT_TILE), F32)] * 2,
        compiler_params=_params(("parallel", "parallel")),
        name="diff_attention_context",
    )(dq, dk, dvt, lam, g_diff)
    return out, out_ctx


def _na_bias_tables(rpb, rows):
    wr = min(WIN_R, rows)
    n_groups = rows // NA_QROWS
    qc = np.arange(GRID_W)[None, :]
    kc = np.arange(GRID_W)[:, None]
    cs = np.clip(qc - WIN_C // 2, 0, GRID_W - WIN_C)
    col_valid = (kc >= cs) & (kc < cs + WIN_C)
    col_sel = ((kc - qc + WIN_C - 1)[None] == np.arange(2 * WIN_C - 1)[:, None, None]) & col_valid[None]
    row_sel = []
    for g in (0, 1, n_groups - 1):
        r0 = g * NA_QROWS
        u0 = int(np.clip(r0 - WIN_R // 2, 0, rows - NA_KROWS))
        r = r0 + np.arange(NA_QROWS)[None, :]
        kr = u0 + np.arange(NA_KROWS)[:, None]
        rs = np.clip(r - WIN_R // 2, 0, rows - wr)
        row_valid = (kr >= rs) & (kr < rs + wr)
        row_sel.append(((kr - r + WIN_R - 1)[None] == np.arange(2 * WIN_R - 1)[:, None, None]) & row_valid[None])
    row_sel = np.stack(row_sel)
    t = jnp.einsum("hrc,vrkd,cxq->vkxhdq", rpb.astype(F32) * LOG2E, jnp.asarray(row_sel, F32),
                   jnp.asarray(col_sel, F32), precision=HIGHEST)
    valid = row_sel.any(axis=1)[:, :, None, None, :, None] & col_valid[None, None, :, None, None, :]
    t = jnp.where(valid, t, NEG_BIG)
    return t.reshape(3, NA_KROWS // NA_QROWS, ATT_TILE, NA_HEADS * ATT_TILE)


def _na_kernel(rows, latent, *refs):
    if latent:
        q_ref, kl_ref, vl_ref, kc_ref, vc_ref, bias_ref, o_ref, *s_refs = refs
    else:
        q_ref, kc_ref, vc_ref, o_ref, *s_refs = refs
    q = q_ref[...]
    t = q.shape[0]
    head = lax.broadcasted_iota(jnp.int32, q.shape, 1) // NA_HD
    zero = jnp.zeros_like(q)
    qs = jnp.concatenate([jnp.where(head == h, q, zero) for h in range(NA_HEADS)], axis=0)

    n_local = NA_KROWS // NA_QROWS
    s = _dot_nt(kc_ref[...], qs)
    s_refs[0][...] = s
    m = jnp.max(s, axis=0, keepdims=True)
    values = [lambda: vc_ref[0]]
    if latent:
        t0 = jnp.clip(pl.program_id(1) - 1, 0, rows // NA_QROWS - n_local)
        k_local = kl_ref[pl.ds(pl.multiple_of(t0 * t, t), n_local * t), :]
        s = _dot_nt(k_local, qs) + bias_ref[0].reshape(n_local * t, NA_HEADS * t)
        for j in range(n_local):
            s_refs[1 + j][...] = s[j * t:(j + 1) * t]
            values.append(lambda j=j: vl_ref[t0 + j])
        m = jnp.maximum(m, jnp.max(s, axis=0, keepdims=True))
    l = jnp.zeros((1, NA_HEADS * t), F32)
    acc = [jnp.zeros((NA_HD, t), F32)] * NA_HEADS
    for i, load_vt in enumerate(values):
        p = jnp.exp2(s_refs[i][...] - m)
        l = l + jnp.sum(p, axis=0, keepdims=True)
        pb, vt = p.astype(BF16), load_vt()
        for h in range(NA_HEADS):
            acc[h] = acc[h] + _dot(vt[h * NA_HD:(h + 1) * NA_HD, :], pb[:, h * t:(h + 1) * t])
    o = jnp.concatenate([acc[h] / l[:, h * t:(h + 1) * t] for h in range(NA_HEADS)], axis=0)
    o_ref[...] = jnp.transpose(o).astype(BF16)


def _neighbourhood_attention(nq, nk, nvt, bias_tabs, with_ctx, dims):
    nb, s_len = dims["B"], dims["S"]
    rows = s_len // GRID_W
    n_groups = rows // NA_QROWS
    lat_blocks = dims["n_lat"] // ATT_TILE
    n_key_tiles = 1 + NA_KROWS // NA_QROWS

    def bias_map(b, g):
        return (jnp.where(g == 0, 0, jnp.where(g >= n_groups - 1, 2, 1)), 0, 0, 0)

    k_ctx = pl.BlockSpec((ATT_TILE, NA_W), lambda b, *_: (lat_blocks + b, 0))
    v_ctx = pl.BlockSpec((1, NA_W, ATT_TILE), lambda b, *_: (lat_blocks + b, 0, 0))
    q_lat = pl.BlockSpec((ATT_TILE, NA_W), lambda b, g: (b * n_groups + g, 0))
    scratch = lambda n: [pltpu.VMEM((ATT_TILE, NA_HEADS * ATT_TILE), F32)] * n
    lat = pl.pallas_call(
        functools.partial(_na_kernel, rows, True),
        grid=(nb, n_groups),
        in_specs=[q_lat, pl.BlockSpec((s_len, NA_W), lambda b, g: (b, 0)),
                  pl.BlockSpec((s_len // ATT_TILE, NA_W, ATT_TILE), lambda b, g: (b, 0, 0)), k_ctx, v_ctx,
                  pl.BlockSpec((1,) + bias_tabs.shape[1:], bias_map)],
        out_specs=q_lat,
        out_shape=jax.ShapeDtypeStruct((dims["n_lat"], NA_W), BF16),
        scratch_shapes=scratch(n_key_tiles),
        compiler_params=_params(("parallel", "arbitrary")),
        name="neighbourhood_attention",
    )(nq, nk, nvt, nk, nvt, bias_tabs)
    if not with_ctx:
        return lat, None
    ctx = pl.pallas_call(
        functools.partial(_na_kernel, rows, False),
        grid=(nb,),
        in_specs=[pl.BlockSpec((ATT_TILE, NA_W), lambda b: (lat_blocks + b, 0)), k_ctx, v_ctx],
        out_specs=pl.BlockSpec((ATT_TILE, NA_W), lambda b: (b, 0)),
        out_shape=jax.ShapeDtypeStruct((nb * ATT_TILE, NA_W), BF16),
        scratch_shapes=scratch(1),
        compiler_params=_params(("parallel",)),
        name="context_attention",
    )(nq, nk, nvt)
    return lat, ctx


def _route_top2(logits):
    lane = lax.broadcasted_iota(jnp.int32, logits.shape, 1)
    lane_f = lane.astype(F32)
    lg = jnp.where(lane < N_EXPERTS, logits, -jnp.inf)
    m1 = jnp.max(lg, axis=-1, keepdims=True)
    i1 = jnp.min(jnp.where(lg == m1, lane_f, float(LANES)), axis=-1, keepdims=True)
    lg2 = jnp.where(lane_f == i1, -jnp.inf, lg)
    m2 = jnp.max(lg2, axis=-1, keepdims=True)
    i2 = jnp.min(jnp.where(lg2 == m2, lane_f, float(LANES)), axis=-1, keepdims=True)
    e = jnp.exp(m2 - m1)
    w1 = 1.0 / (1.0 + e)
    w2 = e / (1.0 + e)
    return jnp.where(lane == 0, i1, jnp.where(lane == 1, i2, jnp.where(lane == 2, w1,
                                                                         jnp.where(lane == 3, w2, 0.0))))


def _outproj_kernel(moe, n_lat_tiles, n_x, *refs):
    x_in = _joined_rows(n_lat_tiles, refs[:n_x])
    of_ref, ob_ref, gate_ref, bd_ref, cn_ref, *rest = refs[n_x:]
    if moe:
        mod_ref, ggla_ref, g2_ref, w_ref, wr_ref, x_out_ref, h_ref, route_ref = rest
        b_diff, c_na = bd_ref[...], cn_ref[...]
    else:
        bdc_ref, cnc_ref, mod_ref, ggla_ref, g2_ref, w_ref, x_out_ref, h_ref = rest
        is_lat = pl.program_id(0) < n_lat_tiles
        b_diff = jnp.where(is_lat, bd_ref[...], bdc_ref[...])
        c_na = jnp.where(is_lat, cn_ref[...], cnc_ref[...])
    m = mod_ref[0, 0]
    o = of_ref[...].astype(F32) + ob_ref[...].astype(F32)
    avg = (lax.broadcasted_iota(jnp.int32, (GLA_V, GLA_V), 0) // GLA_DV
           == lax.broadcasted_iota(jnp.int32, (GLA_V, GLA_V), 1) // GLA_DV).astype(F32) * (1.0 / GLA_DV)
    ms = jnp.dot(o * o, avg, precision=HIGHEST, preferred_element_type=F32)
    a = o * lax.rsqrt(ms + RMS_EPS) * ggla_ref[...] * _silu(gate_ref[...].astype(F32))
    y = (_dot(a.astype(BF16), w_ref[0:GLA_V]) + _dot(b_diff, w_ref[GLA_V:GLA_V + DIFF_W])
         + _dot(c_na, w_ref[GLA_V + DIFF_W:]))
    x = x_in + m[2:3] * y
    x_out_ref[...] = x
    h = x * lax.rsqrt(jnp.mean(x * x, axis=-1, keepdims=True) + RMS_EPS) * g2_ref[...]
    h = h * (1.0 + m[4:5]) + m[3:4]
    h_ref[...] = h.astype(h_ref.dtype)
    if moe:
        h_hi, w = h.astype(BF16), wr_ref[...]
        h_lo = (h - h_hi.astype(F32)).astype(BF16)
        w_hi = w.astype(BF16)
        w_lo = (w - w_hi.astype(F32)).astype(BF16)
        route_ref[...] = _route_top2(_dot(h_hi, w_hi) + (_dot(h_hi, w_lo) + _dot(h_lo, w_hi)))


def _output_projection(x_all, o_f, o_b, gla, b_diff, c_na, mod, layer, g_gla, g2, w_out, w_router, n_rows, dims):
    d = w_out.shape[1]
    n_lat_tiles = dims["n_lat"] // TOK_TILE
    tiles_per_batch = dims["S"] // TOK_TILE
    nb = dims["B"]
    moe = w_router is not None
    assert moe == (b_diff[1] is None) and n_rows == (dims["n_lat"] if moe else dims["n_tot"])
    x_args, x_specs = _split_rows(x_all, n_lat_tiles, d)

    def mod_map(i):
        return (layer, jnp.where(i < n_lat_tiles, i // tiles_per_batch, nb), 0, 0)

    row = lambda w, cb=0: pl.BlockSpec((TOK_TILE, w), lambda i: (i, cb))
    lat_row = lambda w: pl.BlockSpec((TOK_TILE, w), lambda i: (jnp.minimum(i, n_lat_tiles - 1), 0))
    ctx_row = lambda w: pl.BlockSpec((TOK_TILE, w), lambda i: (jnp.maximum(i - n_lat_tiles, 0), 0))
    const = lambda shape: pl.BlockSpec(shape, lambda i: (0,) * len(shape))
    in_specs = x_specs + [row(GLA_V), row(GLA_V), row(GLA_V, (2 * GLA_QK + GLA_V) // GLA_V),
                          lat_row(DIFF_W), lat_row(NA_W)]
    args = x_args + [o_f, o_b, gla, b_diff[0], c_na[0]]
    if not moe:
        in_specs += [ctx_row(DIFF_W), ctx_row(NA_W)]
        args += [b_diff[1], c_na[1]]
    in_specs += [pl.BlockSpec((1, 1, 6, d), mod_map), const((1, GLA_V)), const((1, d)), const((d, d))]
    args += [mod, g_gla, g2, w_out]
    out_specs = [row(d), row(d)]
    out_shape = [jax.ShapeDtypeStruct((n_rows, d), F32), jax.ShapeDtypeStruct((n_rows, d), F32 if moe else BF16)]
    if moe:
        in_specs.append(const((d, LANES)))
        args.append(w_router)
        out_specs.append(row(LANES))
        out_shape.append(jax.ShapeDtypeStruct((n_rows, LANES), F32))
    return pl.pallas_call(
        functools.partial(_outproj_kernel, moe, n_lat_tiles, len(x_args)),
        grid=(n_rows // TOK_TILE,),
        in_specs=in_specs, out_specs=out_specs, out_shape=out_shape,
        compiler_params=_params(("parallel",)),
        name="output_projection",
    )(*args)


def _ffn_kernel(h_ref, x_ref, mod_ref, wg_ref, wu_ref, wd_ref, o_ref, acc_ref):
    h = h_ref[...]
    n_chunks = wg_ref.shape[1] // FFN_CHUNK
    for j in range(n_chunks):
        cols = slice(j * FFN_CHUNK, (j + 1) * FFN_CHUNK)
        act = (_silu(_dot(h, wg_ref[:, cols])) * _dot(h, wu_ref[:, cols])).astype(BF16)
        part = _dot(act, wd_ref[cols, :])
        if j == 0:
            acc_ref[...] = part
        else:
            acc_ref[...] += part
    o_ref[...] = x_ref[...] + mod_ref[0, 0][5:6] * acc_ref[...]


def _dense_ffn(h, x_all, mod, layer, wg, wu, wd, dims):
    n_rows, d = x_all.shape
    f = wg.shape[1]
    n_lat_tiles = dims["n_lat"] // TOK_TILE
    tiles_per_batch = dims["S"] // TOK_TILE
    nb = dims["B"]

    def mod_map(i):
        return (layer, jnp.where(i < n_lat_tiles, i // tiles_per_batch, nb), 0, 0)

    row = lambda: pl.BlockSpec((TOK_TILE, d), lambda i: (i, 0))
    const = lambda shape: pl.BlockSpec(shape, lambda i: (0, 0))
    return pl.pallas_call(
        _ffn_kernel,
        grid=(n_rows // TOK_TILE,),
        in_specs=[row(), row(), pl.BlockSpec((1, 1, 6, d), mod_map), const((d, f)), const((d, f)), const((f, d))],
        out_specs=row(),
        out_shape=jax.ShapeDtypeStruct((n_rows, d), F32),
        scratch_shapes=[pltpu.VMEM((TOK_TILE, d), F32)],
        compiler_params=_params(("parallel",)),
        name="dense_swiglu",
    )(h, x_all, mod, wg, wu, wd)


def _moe_routing(route, n_tok):
    e_flat = jnp.concatenate([route[:, 0], route[:, 1]]).astype(jnp.int32)
    onehot = (e_flat[:, None] == jnp.arange(N_EXPERTS, dtype=jnp.int32)[None, :]).astype(jnp.int32)
    csum = jnp.cumsum(onehot, axis=0)
    counts = csum[-1]
    padded = ((counts + MOE_TILE - 1) // MOE_TILE) * MOE_TILE
    ends = jnp.cumsum(padded)
    pos = jnp.sum(onehot * (csum - 1 + (ends - padded)[None, :]), axis=1)
    n_rows = 2 * n_tok + N_EXPERTS * MOE_TILE
    tile_start = jnp.arange(n_rows // MOE_TILE, dtype=jnp.int32) * MOE_TILE
    tile_expert = jnp.minimum(jnp.sum((tile_start[:, None] >= ends[None, :]).astype(jnp.int32), axis=1),
                              N_EXPERTS - 1).astype(jnp.int32)
    n_active = (ends[-1:] // MOE_TILE).astype(jnp.int32)
    unused = ends[-1] + jnp.arange(N_EXPERTS, dtype=jnp.int32) * MOE_TILE
    tails = jnp.concatenate([jnp.where(padded > 0, ends - MOE_TILE, -1),
                             jnp.where(unused < n_rows, unused, -1)]).astype(jnp.int32)
    n_tiles = n_tok // TOK_TILE
    pos = jnp.concatenate([pos[:n_tok].reshape(n_tiles, 1, TOK_TILE), pos[n_tok:].reshape(n_tiles, 1, TOK_TILE)],
                          axis=2)
    return pos, tails, n_rows, tile_expert, n_active


def _dispatch_kernel(tail_ref, pos_ref, h_ref, xs_out, zero_ref, sem):
    tile = h_ref.shape[0]

    @pl.when(pl.program_id(0) == 0)
    def _():
        zero_ref[...] = jnp.zeros_like(zero_ref)
        for e in range(2 * N_EXPERTS):
            @pl.when(tail_ref[e] >= 0)
            def _():
                rows = pl.ds(pl.multiple_of(tail_ref[e], MOE_TILE), MOE_TILE)
                tail = pltpu.make_async_copy(zero_ref, xs_out.at[rows], sem)
                tail.start()
                tail.wait()

    def issue(r, carry):
        for k in range(2):
            pltpu.make_async_copy(h_ref.at[pl.ds(r, 1)], xs_out.at[pl.ds(pos_ref[0, 0, k * tile + r], 1)],
                                  sem).start()
        return carry
    lax.fori_loop(0, tile, issue, 0, unroll=8)
    for k in range(2):
        pltpu.make_async_copy(h_ref, xs_out.at[pl.ds(0, tile)], sem).wait()


def _moe_dispatch(h, pos, tails, n_rows):
    n_tok, d = h.shape
    grid_spec = pltpu.PrefetchScalarGridSpec(
        num_scalar_prefetch=1,
        grid=(n_tok // TOK_TILE,),
        in_specs=[pl.BlockSpec((1, 1, 2 * TOK_TILE), lambda i, tl: (i, 0, 0), memory_space=pltpu.SMEM),
                  pl.BlockSpec((TOK_TILE, d), lambda i, tl: (i, 0))],
        out_specs=pl.BlockSpec(memory_space=pl.ANY),
        scratch_shapes=[pltpu.VMEM((MOE_TILE, d), h.dtype), pltpu.SemaphoreType.DMA(())])
    return pl.pallas_call(
        _dispatch_kernel,
        grid_spec=grid_spec,
        out_shape=jax.ShapeDtypeStruct((n_rows, d), h.dtype),
        compiler_params=_params(("arbitrary",)),
        name="expert_dispatch",
    )(tails, pos, h)


def _moe_kernel(te_ref, nact_ref, x_ref, wg_ref, wu_ref, wd_ref, o_ref, xb_ref, acc_ref):
    i, f = pl.program_id(0), pl.program_id(1)
    active = i < nact_ref[0]
    last = f == pl.num_programs(1) - 1

    @pl.when(f == 0)
    def _():
        xb_ref[...] = x_ref[...].astype(BF16)
        acc_ref[...] = jnp.zeros_like(acc_ref)

    @pl.when(active)
    def _():
        x = xb_ref[...]
        for j in range(wg_ref.shape[2] // FFN_CHUNK):
            cols = slice(j * FFN_CHUNK, (j + 1) * FFN_CHUNK)
            act = (_silu(_dot(x, wg_ref[0, :, cols])) * _dot(x, wu_ref[0, :, cols])).astype(BF16)
            acc_ref[...] += _dot(act, wd_ref[0, cols, :])

    @pl.when(last)
    def _():
        o_ref[...] = acc_ref[...]


def _moe_ffn(x_sorted, tile_expert, n_active, wg, wu, wd):
    n_rows, d = x_sorted.shape
    ffn = wg.shape[2]
    n_tiles = n_rows // MOE_TILE
    n_f = ffn // MOE_FCHUNK

    def row(i, f, te, na):
        return jnp.minimum(i, na[0] - 1)

    def fcol(i, f, te, na):
        return jnp.where(i < na[0], f, n_f - 1)

    grid_spec = pltpu.PrefetchScalarGridSpec(
        num_scalar_prefetch=2,
        grid=(n_tiles, n_f),
        in_specs=[pl.BlockSpec((MOE_TILE, d), lambda i, f, te, na: (row(i, f, te, na), 0)),
                  pl.BlockSpec((1, d, MOE_FCHUNK), lambda i, f, te, na: (te[i], 0, fcol(i, f, te, na))),
                  pl.BlockSpec((1, d, MOE_FCHUNK), lambda i, f, te, na: (te[i], 0, fcol(i, f, te, na))),
                  pl.BlockSpec((1, MOE_FCHUNK, d), lambda i, f, te, na: (te[i], fcol(i, f, te, na), 0))],
        out_specs=pl.BlockSpec((MOE_TILE, d), lambda i, f, te, na: (i, 0)),
        scratch_shapes=[pltpu.VMEM((MOE_TILE, d), BF16), pltpu.VMEM((MOE_TILE, d), F32)])
    return pl.pallas_call(
        _moe_kernel,
        grid_spec=grid_spec,
        out_shape=jax.ShapeDtypeStruct((n_rows, d), F32),
        compiler_params=_params(("arbitrary", "arbitrary")),
        name="expert_swiglu",
    )(tile_expert, n_active, x_sorted, wg, wu, wd)


def _final_kernel(pos_ref, next_pos_ref, x_ref, route_ref, mod_ref, g_ref, y_hbm, o_ref, buf_ref, sem):
    tile = x_ref.shape[0]
    i, n = pl.program_id(0), pl.num_programs(0)
    slot = i % 2

    def fetch(p_ref, s):
        def issue(r, carry):
            for k in range(2):
                pltpu.make_async_copy(y_hbm.at[pl.ds(p_ref[0, 0, k * tile + r], 1)],
                                      buf_ref.at[s, k, pl.ds(r, 1)], sem.at[s]).start()
            return carry
        lax.fori_loop(0, tile, issue, 0, unroll=8)

    @pl.when(i == 0)
    def _():
        fetch(pos_ref, slot)

    @pl.when(i + 1 < n)
    def _():
        fetch(next_pos_ref, 1 - slot)

    for k in range(2):
        pltpu.make_async_copy(y_hbm.at[pl.ds(0, tile)], buf_ref.at[slot, k], sem.at[slot]).wait()
    route = route_ref[...]
    y = route[:, 2:3] * buf_ref[slot, 0] + route[:, 3:4] * buf_ref[slot, 1]
    x = x_ref[...] + mod_ref[0, 0][5:6] * y
    o_ref[...] = x * lax.rsqrt(jnp.mean(x * x, axis=-1, keepdims=True) + RMS_EPS) * g_ref[...]


def _final_combine(x_lat, y_sorted, pos, route, mod, layer, g_final, dims):
    n_rows, d = x_lat.shape
    n_tiles = n_rows // TOK_TILE
    tiles_per_batch = dims["S"] // TOK_TILE
    return pl.pallas_call(
        _final_kernel,
        grid=(n_tiles,),
        in_specs=[pl.BlockSpec((1, 1, 2 * TOK_TILE), lambda i: (i, 0, 0), memory_space=pltpu.SMEM),
                  pl.BlockSpec((1, 1, 2 * TOK_TILE), lambda i: (jnp.minimum(i + 1, n_tiles - 1), 0, 0),
                               memory_space=pltpu.SMEM),
                  pl.BlockSpec((TOK_TILE, d), lambda i: (i, 0)),
                  pl.BlockSpec((TOK_TILE, LANES), lambda i: (i, 0)),
                  pl.BlockSpec((1, 1, 6, d), lambda i: (layer, i // tiles_per_batch, 0, 0)),
                  pl.BlockSpec((1, d), lambda i: (0, 0)),
                  pl.BlockSpec(memory_space=pl.ANY)],
        out_specs=pl.BlockSpec((TOK_TILE, d), lambda i: (i, 0)),
        out_shape=jax.ShapeDtypeStruct((n_rows, d), F32),
        scratch_shapes=[pltpu.VMEM((2, 2, TOK_TILE, d), F32), pltpu.SemaphoreType.DMA((2,))],
        compiler_params=_params(("arbitrary",)),
        name="expert_combine_final_norm",
    )(pos, pos, x_lat, route, mod, g_final, y_sorted)


def _rope_tables(s_len):
    t = np.arange(s_len)
    lane = np.arange(LANES) % HEAD_DIM
    quarter = HEAD_DIM // 4
    inv = 1.0 / (ROPE_BASE ** (jnp.arange(quarter, dtype=F32) / quarter))
    pos = np.where(lane[None, :] < HEAD_DIM // 2, (t // GRID_W)[:, None], (t % GRID_W)[:, None]).astype(np.float32)
    ang = jnp.asarray(pos) * inv[lane % quarter][None, :]
    first = jnp.asarray((lane % (HEAD_DIM // 2)) < quarter)[None, :]
    cos, sin = jnp.cos(ang), jnp.sin(ang)
    ident = jnp.zeros((TOK_TILE, LANES), F32)
    return (jnp.concatenate([cos, ident + 1.0]),
            jnp.concatenate([jnp.where(first, -sin, 0.0), ident]),
            jnp.concatenate([jnp.where(first, 0.0, sin), ident]))


def _rearranged_w_in(w):
    offs = np.concatenate([[0], np.cumsum(IN_SIZES)])
    part = lambda j: w[:, int(offs[j]):int(offs[j + 1])]
    pad = jnp.zeros((w.shape[0], LANES - 2 * GLA_LR), w.dtype)
    order = [0, 1, 2, 3, 6, 7, 9, 10, 4, 5]
    w_cat = jnp.concatenate([part(j) for j in order] + [pad], axis=1).astype(BF16)
    return w_cat, jnp.transpose(part(8)).astype(BF16), jnp.transpose(part(11)).astype(BF16)


def _decay_up(w_up, b):
    top = jnp.concatenate([w_up[0], jnp.zeros_like(w_up[0])], axis=1)
    bot = jnp.concatenate([jnp.zeros_like(w_up[1]), w_up[1]], axis=1)
    pad = jnp.zeros((LANES - 2 * GLA_LR, 2 * GLA_QK), w_up.dtype)
    return jnp.concatenate([top, bot, pad], axis=0).astype(BF16), jnp.concatenate([b[0], b[1]])[None, :]


def kernel(x, c, ctx, c_ctx, w_mod, b_mod, g_norm1, g_norm2, w_in, gla_w_dec_up, gla_b_dec, gla_g_norm,
           diff_lambda, diff_g_norm, na_rpb, w_out, w_ffn_gate, w_ffn_up, w_ffn_down, w_router, w_moe_gate,
           w_moe_up, w_moe_down, g_final):
    nb, s_len, d = x.shape
    c_len = ctx.shape[1]
    depth = w_mod.shape[0]
    assert c_len == ATT_TILE and s_len % TOK_TILE == 0 and (nb * c_len) % TOK_TILE == 0 and nb < MOD_ROWS
    n_lat = nb * s_len
    dims = {"B": nb, "S": s_len, "n_lat": n_lat, "n_tot": n_lat + nb * c_len}

    x_all = (x.reshape(n_lat, d), ctx.reshape(nb * c_len, d))
    cc = jnp.concatenate([c, c_ctx[None, :], jnp.zeros((MOD_ROWS - nb - 1, d), F32)], axis=0)
    mod = _modulation(cc, w_mod, b_mod).reshape(depth, MOD_ROWS, 6, d)
    rope_tabs = _rope_tables(s_len)

    for l in range(depth):
        last = l == depth - 1
        w_cat, w_dvt, w_nvt = _rearranged_w_in(w_in[l])
        wup2, bdec2 = _decay_up(gla_w_dec_up[l], gla_b_dec[l])
        gla, lg, dq, dk, dvt, nq, nk, nvt = _input_projection(
            x_all, mod, l, g_norm1[l][None, :], w_cat, w_dvt, w_nvt, wup2, bdec2, rope_tabs, dims)
        o_f, o_b = _gla(gla, lg, dims)
        b_diff = _diff_attention(dq, dk, dvt, diff_lambda[l], diff_g_norm[l][None, :], l, not last, dims)
        c_na = _neighbourhood_attention(nq, nk, nvt, _na_bias_tables(na_rpb[l], s_len // GRID_W), not last, dims)
        g_gla = jnp.tile(gla_g_norm[l], GLA_HEADS)[None, :]
        n_rows = n_lat if last else dims["n_tot"]
        if l % 2 == 0:
            i = l // 2
            x_mid, h = _output_projection(x_all, o_f, o_b, gla, b_diff, c_na, mod, l, g_gla, g_norm2[l][None, :],
                                          w_out[l].astype(BF16), None, n_rows, dims)
            assert not last, "the dense channel mixer is implemented for non-final layers only"
            x_all = _dense_ffn(h, x_mid, mod, l, w_ffn_gate[i].astype(BF16), w_ffn_up[i].astype(BF16),
                               w_ffn_down[i].astype(BF16), dims)
        else:
            i = l // 2
            w_r = jnp.concatenate([w_router[i], jnp.zeros((d, LANES - N_EXPERTS), F32)], axis=1)
            x_mid, h, route = _output_projection(x_all, o_f, o_b, gla, b_diff, c_na, mod, l, g_gla,
                                                 g_norm2[l][None, :], w_out[l].astype(BF16), w_r, n_rows, dims)
            assert last, "the expert layer is implemented for the final layer only"
            pos, tails, n_sorted, tile_expert, n_active = _moe_routing(route, n_rows)
            x_sorted = _moe_dispatch(h, pos, tails, n_sorted)
            y_sorted = _moe_ffn(x_sorted, tile_expert, n_active, w_moe_gate[i].astype(BF16),
                                w_moe_up[i].astype(BF16), w_moe_down[i].astype(BF16))
            return _final_combine(x_mid, y_sorted, pos, route, mod, l, g_final[None, :], dims).reshape(nb, s_len, d)
    raise NotImplementedError("the final layer must be an expert layer")
```

```python
import functools
import math

import numpy as np
import jax
import jax.numpy as jnp
from jax import lax
from jax.experimental import pallas as pl
from jax.experimental.pallas import tpu as pltpu

F32 = jnp.float32
BF16 = jnp.bfloat16
HIGHEST = lax.Precision.HIGHEST

GRID_W = 64
HEAD_DIM = 64
ROPE_BASE = 10000.0
RMS_EPS = 1e-6
GLA_HEADS = 4
GLA_DK = 32
GLA_DV = 64
GLA_LR = 16
GLA_NORMALIZER = 16.0
GLA_CHUNK = 64
DIFF_HEADS = 4
DIFF_HD = 64
NA_HEADS = 4
NA_HD = 64
WIN_R = 8
WIN_C = 16
N_EXPERTS = 8
GLA_QK = GLA_HEADS * GLA_DK
GLA_V = GLA_HEADS * GLA_DV
DIFF_W = DIFF_HEADS * 2 * DIFF_HD
NA_W = NA_HEADS * NA_HD
IN_SIZES = (GLA_QK, GLA_QK, GLA_V, GLA_V, GLA_LR, GLA_LR, DIFF_W, DIFF_W, DIFF_W, NA_W, NA_W, NA_W)

LANES = 128
TOK_TILE = 512
ATT_TILE = 256
DIFF_QT = 2048
DIFF_GW = 256
NA_QROWS = 4
NA_KROWS = 12
FFN_CHUNK = 256
MOE_TILE = 512
MOE_FCHUNK = 1792
MOD_ROWS = 16
NEG_BIG = -1e30
LOG2E = math.log2(math.e)
VMEM_LIMIT = 56 * 1024 * 1024

C_GLA = 0
C_DQ = C_GLA + 2 * GLA_QK + 2 * GLA_V
C_DK = C_DQ + DIFF_W
C_NQ = C_DK + DIFF_W
C_NK = C_NQ + NA_W
C_DEC = C_NK + NA_W
C_END = C_DEC + LANES


def _silu(x):
    return x * (1.0 / (1.0 + jnp.exp(-x)))


def _dot(a, b):
    return jnp.dot(a, b, preferred_element_type=F32)


def _dot_nt(a, b):
    return lax.dot_general(a, b, (((1,), (1,)), ((), ())), preferred_element_type=F32)


def _dot_tn(a, b):
    return lax.dot_general(a, b, (((0,), (0,)), ((), ())), preferred_element_type=F32)


def _params(sem, vmem=VMEM_LIMIT):
    return pltpu.CompilerParams(dimension_semantics=sem, vmem_limit_bytes=vmem)


def _mod_kernel(cc_ref, w_ref, b_ref, o_ref):
    s = _silu(cc_ref[...]).astype(BF16)
    o_ref[0] = _dot(s, w_ref[0].astype(BF16)) + b_ref[0]


def _modulation(cc, w_mod, b_mod):
    depth, d, six_d = w_mod.shape
    n = six_d // d
    return pl.pallas_call(
        _mod_kernel,
        grid=(depth, n),
        in_specs=[pl.BlockSpec((MOD_ROWS, d), lambda l, j: (0, 0)),
                  pl.BlockSpec((1, d, d), lambda l, j: (l, 0, j)),
                  pl.BlockSpec((1, 1, d), lambda l, j: (l, 0, j))],
        out_specs=pl.BlockSpec((1, MOD_ROWS, d), lambda l, j: (l, 0, j)),
        out_shape=jax.ShapeDtypeStruct((depth, MOD_ROWS, six_d), F32),
        compiler_params=_params(("parallel", "parallel")),
        name="modulation",
    )(cc, w_mod, b_mod.reshape(depth, 1, six_d))


def _rope(z, a, bt, ct):
    outs = []
    for s in range(z.shape[1] // LANES):
        zs = z[:, s * LANES:(s + 1) * LANES]
        outs.append(zs * a + pltpu.roll(zs, LANES - 16, 1) * bt + pltpu.roll(zs, 16, 1) * ct)
    return jnp.concatenate(outs, axis=1)


def _split_rows(x, n_lat_tiles, width):
    if not isinstance(x, tuple):
        return [x], [pl.BlockSpec((TOK_TILE, width), lambda i: (i, 0))]
    return list(x), [pl.BlockSpec((TOK_TILE, width), lambda i: (jnp.minimum(i, n_lat_tiles - 1), 0)),
                     pl.BlockSpec((TOK_TILE, width), lambda i: (jnp.maximum(i - n_lat_tiles, 0), 0))]


def _joined_rows(n_lat_tiles, refs):
    if len(refs) == 1:
        return refs[0][...]
    return jnp.where(pl.program_id(0) < n_lat_tiles, refs[0][...], refs[1][...])


def _inproj_kernel(n_lat_tiles, n_x, *refs):
    (mod_ref, g_ref, w_ref, wvt_ref, wnt_ref, wup_ref, bdec_ref, ra_ref, rb_ref, rc_ref,
     gla_ref, lg_ref, dq_ref, dk_ref, dvt_ref, nq_ref, nk_ref, nvt_ref) = refs[n_x:]
    x = _joined_rows(n_lat_tiles, refs[:n_x])
    m = mod_ref[0, 0]
    h = x * lax.rsqrt(jnp.mean(x * x, axis=-1, keepdims=True) + RMS_EPS) * g_ref[...]
    hb = (h * (1.0 + m[1:2]) + m[0:1]).astype(BF16)

    def proj(c0, c1):
        return _dot(hb, w_ref[:, c0:c1])

    gla_ref[...] = proj(C_GLA, C_DQ).astype(BF16)
    a, bt, ct = ra_ref[...], rb_ref[...], rc_ref[...]
    dq_ref[...] = (_rope(proj(C_DQ, C_DK), a, bt, ct) * (DIFF_HD ** -0.5 * LOG2E)).astype(BF16)
    dk_ref[...] = _rope(proj(C_DK, C_NQ), a, bt, ct).astype(BF16)
    dvt_ref[0] = _dot_nt(wvt_ref[...], hb).astype(BF16)
    nq_ref[...] = (proj(C_NQ, C_NK) * (NA_HD ** -0.5 * LOG2E)).astype(BF16)
    nk_ref[...] = proj(C_NK, C_DEC).astype(BF16)
    for j in range(nvt_ref.shape[0]):
        nvt_ref[j] = _dot_nt(wnt_ref[...], hb[j * ATT_TILE:(j + 1) * ATT_TILE]).astype(BF16)
    logits = _dot(proj(C_DEC, C_END).astype(BF16), wup_ref[...]) + bdec_ref[...]
    log_sig = jnp.minimum(logits, 0.0) - jnp.log1p(jnp.exp(-jnp.abs(logits)))
    lg_ref[...] = log_sig * (1.0 / GLA_NORMALIZER)


def _input_projection(x_all, mod, layer, g1, w_cat, w_dvt, w_nvt, wup2, bdec2, rope_tabs, dims):
    n_tot, d = dims["n_tot"], g1.shape[1]
    n_lat_tiles = dims["n_lat"] // TOK_TILE
    x_args, x_specs = _split_rows(x_all, n_lat_tiles, d)
    tiles_per_batch = dims["S"] // TOK_TILE
    n_tiles = n_tot // TOK_TILE
    nb = dims["B"]

    def mod_map(i):
        return (layer, jnp.where(i < n_lat_tiles, i // tiles_per_batch, nb), 0, 0)

    def rope_map(i):
        return (jnp.where(i < n_lat_tiles, i % tiles_per_batch, tiles_per_batch), 0)

    row = lambda w: pl.BlockSpec((TOK_TILE, w), lambda i: (i, 0))
    const = lambda shape: pl.BlockSpec(shape, lambda i: (0,) * len(shape))
    tab = pl.BlockSpec((TOK_TILE, LANES), rope_map)
    sub = TOK_TILE // ATT_TILE
    out_specs = [row(C_DQ), row(2 * GLA_QK), row(DIFF_W), row(DIFF_W),
                 pl.BlockSpec((1, DIFF_W, TOK_TILE), lambda i: (i, 0, 0)), row(NA_W), row(NA_W),
                 pl.BlockSpec((sub, NA_W, ATT_TILE), lambda i: (i, 0, 0))]
    shape = lambda w, dt=BF16: jax.ShapeDtypeStruct((n_tot, w), dt)
    out_shape = [shape(C_DQ), shape(2 * GLA_QK, F32), shape(DIFF_W), shape(DIFF_W),
                 jax.ShapeDtypeStruct((n_tiles, DIFF_W, TOK_TILE), BF16), shape(NA_W), shape(NA_W),
                 jax.ShapeDtypeStruct((n_tiles * sub, NA_W, ATT_TILE), BF16)]
    return pl.pallas_call(
        functools.partial(_inproj_kernel, n_lat_tiles, len(x_args)),
        grid=(n_tiles,),
        in_specs=x_specs + [pl.BlockSpec((1, 1, 6, d), mod_map), const((1, d)), const((d, C_END)),
                            const((DIFF_W, d)), const((NA_W, d)), const((LANES, 2 * GLA_QK)),
                            const((1, 2 * GLA_QK)), tab, tab, tab],
        out_specs=out_specs,
        out_shape=out_shape,
        compiler_params=_params(("parallel",)),
        name="input_projection",
    )(*x_args, mod, g1, w_cat, w_dvt, w_nvt, wup2, bdec2, *rope_tabs)


def _gla_block_local(g_ref, l_ref, fwd, consts):
    tri, same, tri4, head_k, head_v, bd = consts
    t = g_ref.shape[0]
    q = g_ref[:, 0:GLA_QK].astype(F32) * GLA_DK ** -0.5
    k = g_ref[:, GLA_QK:2 * GLA_QK].astype(F32)
    v = g_ref[:, 2 * GLA_QK:2 * GLA_QK + GLA_V]
    lcol = 0 if fwd else GLA_QK
    lg = l_ref[:, lcol:lcol + GLA_QK]
    lg_hi = lg.astype(BF16)
    lg_lo = (lg - lg_hi.astype(F32)).astype(BF16)
    b = _dot(tri, lg_hi) + _dot(tri, lg_lo)
    total = _dot(same, lg_hi) + _dot(same, lg_lo)
    q_dec = q * jnp.exp(b)
    k_inv = (k * jnp.exp(-b)).astype(BF16)
    k_end = (k * jnp.exp(total - b)).astype(BF16)
    qs = jnp.concatenate([jnp.where(head_k == h, q_dec, 0.0) for h in range(GLA_HEADS)], axis=0)
    att = jnp.where(tri4, _dot_nt(qs.astype(BF16), k_inv), 0.0)
    ov = _dot(att.astype(BF16), v)
    o = jnp.where(head_v == 0, ov[0:t], 0.0)
    for h in range(1, GLA_HEADS):
        o = o + jnp.where(head_v == h, ov[h * t:(h + 1) * t], 0.0)
    q_dec = q_dec.astype(BF16)
    chunks = []
    for c in range(t // GLA_CHUNK):
        rows = slice(c * GLA_CHUNK, (c + 1) * GLA_CHUNK)
        chunks.append((o[rows], q_dec[rows], jnp.where(bd, _dot_tn(v[rows], k_end[rows]), 0.0),
                       jnp.exp(total[c * GLA_CHUNK:c * GLA_CHUNK + 1])))
    return chunks


def _gla_kernel(gf_ref, lf_ref, gb_ref, lb_ref, of_ref, ob_ref, sf_ref, sb_ref):
    @pl.when(pl.program_id(1) == 0)
    def _():
        sf_ref[...] = jnp.zeros_like(sf_ref)
        sb_ref[...] = jnp.zeros_like(sb_ref)

    t = gf_ref.shape[0]
    r = lax.broadcasted_iota(jnp.int32, (t, t), 0)
    c = lax.broadcasted_iota(jnp.int32, (t, t), 1)
    r4 = lax.broadcasted_iota(jnp.int32, (GLA_HEADS * t, t), 0) % t
    c4 = lax.broadcasted_iota(jnp.int32, (GLA_HEADS * t, t), 1)
    same = r // GLA_CHUNK == c // GLA_CHUNK
    same4 = r4 // GLA_CHUNK == c4 // GLA_CHUNK
    head_k = lax.broadcasted_iota(jnp.int32, (1, GLA_QK), 1) // GLA_DK
    head_v = lax.broadcasted_iota(jnp.int32, (1, GLA_V), 1) // GLA_DV
    bd = (lax.broadcasted_iota(jnp.int32, (GLA_V, GLA_QK), 0) // GLA_DV
          == lax.broadcasted_iota(jnp.int32, (GLA_V, GLA_QK), 1) // GLA_DK)
    same_b = same.astype(BF16)
    lower = (jnp.logical_and(same, c <= r).astype(BF16), same_b, jnp.logical_and(same4, c4 <= r4), head_k, head_v, bd)
    upper = (jnp.logical_and(same, c >= r).astype(BF16), same_b, jnp.logical_and(same4, c4 >= r4), head_k, head_v, bd)
    n_chunks = t // GLA_CHUNK
    scans = ((gf_ref, lf_ref, of_ref, sf_ref, True, lower, list(range(n_chunks))),
             (gb_ref, lb_ref, ob_ref, sb_ref, False, upper, list(range(n_chunks - 1, -1, -1))))
    local = [_gla_block_local(g_ref, l_ref, fwd, consts) for g_ref, l_ref, _, _, fwd, consts, _ in scans]
    for (_, _, o_ref, s_ref, _, _, order), chunks in zip(scans, local):
        s = s_ref[...]
        for cidx, (o_intra, q_dec, inc, decay) in ((cidx, chunks[cidx]) for cidx in order):
            o = o_intra + _dot_nt(q_dec, s.astype(BF16))
            o_ref[cidx * GLA_CHUNK:(cidx + 1) * GLA_CHUNK, :] = o.astype(o_ref.dtype)
            s = decay * s + inc
        s_ref[...] = s


def _gla(gla, lg, dims):
    n_tot = gla.shape[0]
    nb, steps = dims["B"], 1 + dims["S"] // ATT_TILE
    lat_blocks = dims["n_lat"] // ATT_TILE
    per_batch = dims["S"] // ATT_TILE

    def fwd_map(b, i):
        return (jnp.where(i == 0, lat_blocks + b, b * per_batch + i - 1), 0)

    def bwd_map(b, i):
        return (jnp.where(i == 0, lat_blocks + b, b * per_batch + per_batch - i), 0)

    return pl.pallas_call(
        _gla_kernel,
        grid=(nb, steps),
        in_specs=[pl.BlockSpec((ATT_TILE, C_DQ), fwd_map), pl.BlockSpec((ATT_TILE, 2 * GLA_QK), fwd_map),
                  pl.BlockSpec((ATT_TILE, C_DQ), bwd_map), pl.BlockSpec((ATT_TILE, 2 * GLA_QK), bwd_map)],
        out_specs=[pl.BlockSpec((ATT_TILE, GLA_V), fwd_map), pl.BlockSpec((ATT_TILE, GLA_V), bwd_map)],
        out_shape=[jax.ShapeDtypeStruct((n_tot, GLA_V), BF16)] * 2,
        scratch_shapes=[pltpu.VMEM((GLA_V, GLA_QK), F32), pltpu.VMEM((GLA_V, GLA_QK), F32)],
        compiler_params=_params(("parallel", "arbitrary")),
        name="gla_scan",
    )(gla, lg, gla, lg)


def _diff_kernel(lam_init, latent, *refs):
    if latent:
        q_ref, kl_ref, vl_ref, kc_ref, vc_ref, lam_ref, g_ref, o_ref, *s_refs = refs
    else:
        q_ref, kc_ref, vc_ref, lam_ref, g_ref, o_ref, *s_refs = refs
    q = q_ref[...]
    t, hw = q.shape
    c_len = kc_ref.shape[0]
    lane = lax.broadcasted_iota(jnp.int32, q.shape, 1)
    zero = jnp.zeros_like(q)
    qm = (jnp.where(lane < DIFF_HD, q, zero), jnp.where(lane >= DIFF_HD, q, zero))

    tiles = [(lambda: kc_ref[...], lambda: vc_ref[0], 0, c_len)]
    if latent:
        tiles += [(lambda j=j: kl_ref[j * TOK_TILE:(j + 1) * TOK_TILE, :], lambda j=j: vl_ref[j],
                   c_len + j * TOK_TILE, TOK_TILE) for j in range(vl_ref.shape[0])]

    gw = min(t, DIFF_GW)
    groups = [(mp, c0) for mp in range(2) for c0 in range(0, t, gw)]

    def pass1(g, tile, m):
        mp, c0 = groups[g]
        load_k, _, off, n = tile
        s = _dot_nt(load_k(), qm[mp][c0:c0 + gw])
        s_refs[g % 2][off:off + n, :] = s
        return jnp.maximum(m, jnp.max(s, axis=0, keepdims=True))

    def pass2(g, tile, m, l, acc):
        _, load_vt, off, n = tile
        p = jnp.exp2(s_refs[g % 2][off:off + n, :] - m)
        return l + jnp.sum(p, axis=0, keepdims=True), acc + _dot(load_vt(), p.astype(BF16))

    m_prev, outs = None, []
    for g in range(len(groups) + 1):
        m = jnp.full((1, gw), -jnp.inf, F32)
        l, acc = jnp.zeros((1, gw), F32), jnp.zeros((hw, gw), F32)
        for tile in tiles:
            if g < len(groups):
                m = pass1(g, tile, m)
            if g > 0:
                l, acc = pass2(g - 1, tile, m_prev, l, acc)
        if g > 0:
            outs.append(acc / l)
        m_prev = m
    n_half = len(groups) // 2
    on = [jnp.concatenate(outs[i * n_half:(i + 1) * n_half], axis=1) for i in range(2)]

    lam = lam_ref[...]
    lam_full = (jnp.exp(jnp.sum(lam[0:1] * lam[1:2], axis=-1, keepdims=True))
                - jnp.exp(jnp.sum(lam[2:3] * lam[3:4], axis=-1, keepdims=True)) + lam_init)
    o = jnp.transpose(on[0] - lam_full * on[1])
    o = o * lax.rsqrt(jnp.mean(o * o, axis=-1, keepdims=True) + RMS_EPS) * g_ref[...]
    o_ref[...] = (o * (1.0 - lam_init)).astype(BF16)


def _diff_attention(dq, dk, dvt, lam, g_diff, layer, with_ctx, dims):
    n_tot = dq.shape[0]
    nb, s_len = dims["B"], dims["S"]
    qt = min(DIFF_QT, s_len)
    n_lat_q = s_len // qt
    lat_blocks = dims["n_lat"] // ATT_TILE
    lat_tiles = dims["n_lat"] // TOK_TILE
    ctx_per_tile = TOK_TILE // ATT_TILE
    lam_init = 0.8 - 0.6 * math.exp(-0.3 * layer)
    hw = 2 * DIFF_HD
    out_shape = jax.ShapeDtypeStruct((dims["n_lat"], DIFF_W), BF16)

    k_ctx = pl.BlockSpec((ATT_TILE, hw), lambda b, h, *_: (lat_blocks + b, h))
    v_ctx = pl.BlockSpec((1, hw, ATT_TILE), lambda b, h, *_: (lat_tiles + b // ctx_per_tile, h, b % ctx_per_tile))
    small = [pl.BlockSpec((4, DIFF_HD), lambda *_: (0, 0)), pl.BlockSpec((1, hw), lambda *_: (0, 0))]
    q_lat = pl.BlockSpec((qt, hw), lambda b, h, i: (b * n_lat_q + i, h))
    out = pl.pallas_call(
        functools.partial(_diff_kernel, lam_init, True),
        grid=(nb, DIFF_HEADS, n_lat_q),
        in_specs=[q_lat, pl.BlockSpec((s_len, hw), lambda b, h, i: (b, h)),
                  pl.BlockSpec((s_len // TOK_TILE, hw, TOK_TILE), lambda b, h, i: (b, h, 0)), k_ctx, v_ctx] + small,
        out_specs=q_lat,
        out_shape=out_shape,
        scratch_shapes=[pltpu.VMEM((ATT_TILE + s_len, DIFF_GW), F32)] * 2,
        compiler_params=_params(("parallel", "parallel", "arbitrary")),
        name="diff_attention",
    )(dq, dk, dvt, dk, dvt, lam, g_diff)
    if not with_ctx:
        return out, None
    out_ctx = pl.pallas_call(
        functools.partial(_diff_kernel, lam_init, False),
        grid=(nb, DIFF_HEADS),
        in_specs=[pl.BlockSpec((ATT_TILE, hw), lambda b, h: (lat_blocks + b, h)), k_ctx, v_ctx] + small,
        out_specs=pl.BlockSpec((ATT_TILE, hw), lambda b, h: (b, h)),
        out_shape=jax.ShapeDtypeStruct((nb * ATT_TILE, DIFF_W), BF16),
        scratch_shapes=[pltpu.VMEM((ATT_TILE, ATT_TILE), F32)] * 2,
        compiler_params=_params(("parallel", "parallel")),
        name="diff_attention_context",
    )(dq, dk, dvt, lam, g_diff)
    return out, out_ctx


def _na_bias_tables(rpb, rows):
    wr = min(WIN_R, rows)
    n_groups = rows // NA_QROWS
    qc = np.arange(GRID_W)[None, :]
    kc = np.arange(GRID_W)[:, None]
    cs = np.clip(qc - WIN_C // 2, 0, GRID_W - WIN_C)
    col_valid = (kc >= cs) & (kc < cs + WIN_C)
    col_sel = ((kc - qc + WIN_C - 1)[None] == np.arange(2 * WIN_C - 1)[:, None, None]) & col_valid[None]
    row_sel = []
    for g in (0, 1, n_groups - 1):
        r0 = g * NA_QROWS
        u0 = int(np.clip(r0 - WIN_R // 2, 0, rows - NA_KROWS))
        r = r0 + np.arange(NA_QROWS)[None, :]
        kr = u0 + np.arange(NA_KROWS)[:, None]
        rs = np.clip(r - WIN_R // 2, 0, rows - wr)
        row_valid = (kr >= rs) & (kr < rs + wr)
        row_sel.append(((kr - r + WIN_R - 1)[None] == np.arange(2 * WIN_R - 1)[:, None, None]) & row_valid[None])
    row_sel = np.stack(row_sel)
    t = jnp.einsum("hrc,vrkd,cxq->vkxhdq", rpb.astype(F32) * LOG2E, jnp.asarray(row_sel, F32),
                   jnp.asarray(col_sel, F32), precision=HIGHEST)
    valid = row_sel.any(axis=1)[:, :, None, None, :, None] & col_valid[None, None, :, None, None, :]
    t = jnp.where(valid, t, NEG_BIG)
    return t.reshape(3, NA_KROWS // NA_QROWS, ATT_TILE, NA_HEADS * ATT_TILE)


def _na_kernel(rows, latent, *refs):
    if latent:
        q_ref, kl_ref, vl_ref, kc_ref, vc_ref, bias_ref, o_ref, *s_refs = refs
    else:
        q_ref, kc_ref, vc_ref, o_ref, *s_refs = refs
    q = q_ref[...]
    t = q.shape[0]
    head = lax.broadcasted_iota(jnp.int32, q.shape, 1) // NA_HD
    zero = jnp.zeros_like(q)
    qs = jnp.concatenate([jnp.where(head == h, q, zero) for h in range(NA_HEADS)], axis=0)

    n_local = NA_KROWS // NA_QROWS
    s = _dot_nt(kc_ref[...], qs)
    s_refs[0][...] = s
    m = jnp.max(s, axis=0, keepdims=True)
    values = [lambda: vc_ref[0]]
    if latent:
        t0 = jnp.clip(pl.program_id(1) - 1, 0, rows // NA_QROWS - n_local)
        k_local = kl_ref[pl.ds(pl.multiple_of(t0 * t, t), n_local * t), :]
        s = _dot_nt(k_local, qs) + bias_ref[0].reshape(n_local * t, NA_HEADS * t)
        for j in range(n_local):
            s_refs[1 + j][...] = s[j * t:(j + 1) * t]
            values.append(lambda j=j: vl_ref[t0 + j])
        m = jnp.maximum(m, jnp.max(s, axis=0, keepdims=True))
    l = jnp.zeros((1, NA_HEADS * t), F32)
    acc = [jnp.zeros((NA_HD, t), F32)] * NA_HEADS
    for i, load_vt in enumerate(values):
        p = jnp.exp2(s_refs[i][...] - m)
        l = l + jnp.sum(p, axis=0, keepdims=True)
        pb, vt = p.astype(BF16), load_vt()
        for h in range(NA_HEADS):
            acc[h] = acc[h] + _dot(vt[h * NA_HD:(h + 1) * NA_HD, :], pb[:, h * t:(h + 1) * t])
    o = jnp.concatenate([acc[h] / l[:, h * t:(h + 1) * t] for h in range(NA_HEADS)], axis=0)
    o_ref[...] = jnp.transpose(o).astype(BF16)


def _neighbourhood_attention(nq, nk, nvt, bias_tabs, with_ctx, dims):
    nb, s_len = dims["B"], dims["S"]
    rows = s_len // GRID_W
    n_groups = rows // NA_QROWS
    lat_blocks = dims["n_lat"] // ATT_TILE
    n_key_tiles = 1 + NA_KROWS // NA_QROWS

    def bias_map(b, g):
        return (jnp.where(g == 0, 0, jnp.where(g >= n_groups - 1, 2, 1)), 0, 0, 0)

    k_ctx = pl.BlockSpec((ATT_TILE, NA_W), lambda b, *_: (lat_blocks + b, 0))
    v_ctx = pl.BlockSpec((1, NA_W, ATT_TILE), lambda b, *_: (lat_blocks + b, 0, 0))
    q_lat = pl.BlockSpec((ATT_TILE, NA_W), lambda b, g: (b * n_groups + g, 0))
    scratch = lambda n: [pltpu.VMEM((ATT_TILE, NA_HEADS * ATT_TILE), F32)] * n
    lat = pl.pallas_call(
        functools.partial(_na_kernel, rows, True),
        grid=(nb, n_groups),
        in_specs=[q_lat, pl.BlockSpec((s_len, NA_W), lambda b, g: (b, 0)),
                  pl.BlockSpec((s_len // ATT_TILE, NA_W, ATT_TILE), lambda b, g: (b, 0, 0)), k_ctx, v_ctx,
                  pl.BlockSpec((1,) + bias_tabs.shape[1:], bias_map)],
        out_specs=q_lat,
        out_shape=jax.ShapeDtypeStruct((dims["n_lat"], NA_W), BF16),
        scratch_shapes=scratch(n_key_tiles),
        compiler_params=_params(("parallel", "arbitrary")),
        name="neighbourhood_attention",
    )(nq, nk, nvt, nk, nvt, bias_tabs)
    if not with_ctx:
        return lat, None
    ctx = pl.pallas_call(
        functools.partial(_na_kernel, rows, False),
        grid=(nb,),
        in_specs=[pl.BlockSpec((ATT_TILE, NA_W), lambda b: (lat_blocks + b, 0)), k_ctx, v_ctx],
        out_specs=pl.BlockSpec((ATT_TILE, NA_W), lambda b: (b, 0)),
        out_shape=jax.ShapeDtypeStruct((nb * ATT_TILE, NA_W), BF16),
        scratch_shapes=scratch(1),
        compiler_params=_params(("parallel",)),
        name="context_attention",
    )(nq, nk, nvt)
    return lat, ctx


def _route_top2(logits):
    lane = lax.broadcasted_iota(jnp.int32, logits.shape, 1)
    lane_f = lane.astype(F32)
    lg = jnp.where(lane < N_EXPERTS, logits, -jnp.inf)
    m1 = jnp.max(lg, axis=-1, keepdims=True)
    i1 = jnp.min(jnp.where(lg == m1, lane_f, float(LANES)), axis=-1, keepdims=True)
    lg2 = jnp.where(lane_f == i1, -jnp.inf, lg)
    m2 = jnp.max(lg2, axis=-1, keepdims=True)
    i2 = jnp.min(jnp.where(lg2 == m2, lane_f, float(LANES)), axis=-1, keepdims=True)
    e = jnp.exp(m2 - m1)
    w1 = 1.0 / (1.0 + e)
    w2 = e / (1.0 + e)
    return jnp.where(lane == 0, i1, jnp.where(lane == 1, i2, jnp.where(lane == 2, w1,
                                                                         jnp.where(lane == 3, w2, 0.0))))


def _outproj_kernel(moe, n_lat_tiles, n_x, *refs):
    x_in = _joined_rows(n_lat_tiles, refs[:n_x])
    of_ref, ob_ref, gate_ref, bd_ref, cn_ref, *rest = refs[n_x:]
    if moe:
        mod_ref, ggla_ref, g2_ref, w_ref, wr_ref, x_out_ref, h_ref, route_ref = rest
        b_diff, c_na = bd_ref[...], cn_ref[...]
    else:
        bdc_ref, cnc_ref, mod_ref, ggla_ref, g2_ref, w_ref, x_out_ref, h_ref = rest
        is_lat = pl.program_id(0) < n_lat_tiles
        b_diff = jnp.where(is_lat, bd_ref[...], bdc_ref[...])
        c_na = jnp.where(is_lat, cn_ref[...], cnc_ref[...])
    m = mod_ref[0, 0]
    o = of_ref[...].astype(F32) + ob_ref[...].astype(F32)
    avg = (lax.broadcasted_iota(jnp.int32, (GLA_V, GLA_V), 0) // GLA_DV
           == lax.broadcasted_iota(jnp.int32, (GLA_V, GLA_V), 1) // GLA_DV).astype(F32) * (1.0 / GLA_DV)
    ms = jnp.dot(o * o, avg, precision=HIGHEST, preferred_element_type=F32)
    a = o * lax.rsqrt(ms + RMS_EPS) * ggla_ref[...] * _silu(gate_ref[...].astype(F32))
    y = (_dot(a.astype(BF16), w_ref[0:GLA_V]) + _dot(b_diff, w_ref[GLA_V:GLA_V + DIFF_W])
         + _dot(c_na, w_ref[GLA_V + DIFF_W:]))
    x = x_in + m[2:3] * y
    x_out_ref[...] = x
    h = x * lax.rsqrt(jnp.mean(x * x, axis=-1, keepdims=True) + RMS_EPS) * g2_ref[...]
    h = h * (1.0 + m[4:5]) + m[3:4]
    h_ref[...] = h.astype(h_ref.dtype)
    if moe:
        h_hi, w = h.astype(BF16), wr_ref[...]
        h_lo = (h - h_hi.astype(F32)).astype(BF16)
        w_hi = w.astype(BF16)
        w_lo = (w - w_hi.astype(F32)).astype(BF16)
        route_ref[...] = _route_top2(_dot(h_hi, w_hi) + (_dot(h_hi, w_lo) + _dot(h_lo, w_hi)))


def _output_projection(x_all, o_f, o_b, gla, b_diff, c_na, mod, layer, g_gla, g2, w_out, w_router, n_rows, dims):
    d = w_out.shape[1]
    n_lat_tiles = dims["n_lat"] // TOK_TILE
    tiles_per_batch = dims["S"] // TOK_TILE
    nb = dims["B"]
    moe = w_router is not None
    assert moe == (b_diff[1] is None) and n_rows == (dims["n_lat"] if moe else dims["n_tot"])
    x_args, x_specs = _split_rows(x_all, n_lat_tiles, d)

    def mod_map(i):
        return (layer, jnp.where(i < n_lat_tiles, i // tiles_per_batch, nb), 0, 0)

    row = lambda w, cb=0: pl.BlockSpec((TOK_TILE, w), lambda i: (i, cb))
    lat_row = lambda w: pl.BlockSpec((TOK_TILE, w), lambda i: (jnp.minimum(i, n_lat_tiles - 1), 0))
    ctx_row = lambda w: pl.BlockSpec((TOK_TILE, w), lambda i: (jnp.maximum(i - n_lat_tiles, 0), 0))
    const = lambda shape: pl.BlockSpec(shape, lambda i: (0,) * len(shape))
    in_specs = x_specs + [row(GLA_V), row(GLA_V), row(GLA_V, (2 * GLA_QK + GLA_V) // GLA_V),
                          lat_row(DIFF_W), lat_row(NA_W)]
    args = x_args + [o_f, o_b, gla, b_diff[0], c_na[0]]
    if not moe:
        in_specs += [ctx_row(DIFF_W), ctx_row(NA_W)]
        args += [b_diff[1], c_na[1]]
    in_specs += [pl.BlockSpec((1, 1, 6, d), mod_map), const((1, GLA_V)), const((1, d)), const((d, d))]
    args += [mod, g_gla, g2, w_out]
    out_specs = [row(d), row(d)]
    out_shape = [jax.ShapeDtypeStruct((n_rows, d), F32), jax.ShapeDtypeStruct((n_rows, d), F32 if moe else BF16)]
    if moe:
        in_specs.append(const((d, LANES)))
        args.append(w_router)
        out_specs.append(row(LANES))
        out_shape.append(jax.ShapeDtypeStruct((n_rows, LANES), F32))
    return pl.pallas_call(
        functools.partial(_outproj_kernel, moe, n_lat_tiles, len(x_args)),
        grid=(n_rows // TOK_TILE,),
        in_specs=in_specs, out_specs=out_specs, out_shape=out_shape,
        compiler_params=_params(("parallel",)),
        name="output_projection",
    )(*args)


def _ffn_kernel(h_ref, x_ref, mod_ref, wg_ref, wu_ref, wd_ref, o_ref, acc_ref):
    h = h_ref[...]
    n_chunks = wg_ref.shape[1] // FFN_CHUNK
    for j in range(n_chunks):
        cols = slice(j * FFN_CHUNK, (j + 1) * FFN_CHUNK)
        act = (_silu(_dot(h, wg_ref[:, cols])) * _dot(h, wu_ref[:, cols])).astype(BF16)
        part = _dot(act, wd_ref[cols, :])
        if j == 0:
            acc_ref[...] = part
        else:
            acc_ref[...] += part
    o_ref[...] = x_ref[...] + mod_ref[0, 0][5:6] * acc_ref[...]


def _dense_ffn(h, x_all, mod, layer, wg, wu, wd, dims):
    n_rows, d = x_all.shape
    f = wg.shape[1]
    n_lat_tiles = dims["n_lat"] // TOK_TILE
    tiles_per_batch = dims["S"] // TOK_TILE
    nb = dims["B"]

    def mod_map(i):
        return (layer, jnp.where(i < n_lat_tiles, i // tiles_per_batch, nb), 0, 0)

    row = lambda: pl.BlockSpec((TOK_TILE, d), lambda i: (i, 0))
    const = lambda shape: pl.BlockSpec(shape, lambda i: (0, 0))
    return pl.pallas_call(
        _ffn_kernel,
        grid=(n_rows // TOK_TILE,),
        in_specs=[row(), row(), pl.BlockSpec((1, 1, 6, d), mod_map), const((d, f)), const((d, f)), const((f, d))],
        out_specs=row(),
        out_shape=jax.ShapeDtypeStruct((n_rows, d), F32),
        scratch_shapes=[pltpu.VMEM((TOK_TILE, d), F32)],
        compiler_params=_params(("parallel",)),
        name="dense_swiglu",
    )(h, x_all, mod, wg, wu, wd)


def _moe_routing(route, n_tok):
    e_flat = jnp.concatenate([route[:, 0], route[:, 1]]).astype(jnp.int32)
    onehot = (e_flat[:, None] == jnp.arange(N_EXPERTS, dtype=jnp.int32)[None, :]).astype(jnp.int32)
    csum = jnp.cumsum(onehot, axis=0)
    counts = csum[-1]
    padded = ((counts + MOE_TILE - 1) // MOE_TILE) * MOE_TILE
    ends = jnp.cumsum(padded)
    pos = jnp.sum(onehot * (csum - 1 + (ends - padded)[None, :]), axis=1)
    n_rows = 2 * n_tok + N_EXPERTS * MOE_TILE
    tile_start = jnp.arange(n_rows // MOE_TILE, dtype=jnp.int32) * MOE_TILE
    tile_expert = jnp.minimum(jnp.sum((tile_start[:, None] >= ends[None, :]).astype(jnp.int32), axis=1),
                              N_EXPERTS - 1).astype(jnp.int32)
    n_active = (ends[-1:] // MOE_TILE).astype(jnp.int32)
    unused = ends[-1] + jnp.arange(N_EXPERTS, dtype=jnp.int32) * MOE_TILE
    tails = jnp.concatenate([jnp.where(padded > 0, ends - MOE_TILE, -1),
                             jnp.where(unused < n_rows, unused, -1)]).astype(jnp.int32)
    n_tiles = n_tok // TOK_TILE
    pos = jnp.concatenate([pos[:n_tok].reshape(n_tiles, 1, TOK_TILE), pos[n_tok:].reshape(n_tiles, 1, TOK_TILE)],
                          axis=2)
    return pos, tails, n_rows, tile_expert, n_active


def _dispatch_kernel(tail_ref, pos_ref, h_ref, xs_out, zero_ref, sem):
    tile = h_ref.shape[0]

    @pl.when(pl.program_id(0) == 0)
    def _():
        zero_ref[...] = jnp.zeros_like(zero_ref)
        for e in range(2 * N_EXPERTS):
            @pl.when(tail_ref[e] >= 0)
            def _():
                rows = pl.ds(pl.multiple_of(tail_ref[e], MOE_TILE), MOE_TILE)
                tail = pltpu.make_async_copy(zero_ref, xs_out.at[rows], sem)
                tail.start()
                tail.wait()

    def issue(r, carry):
        for k in range(2):
            pltpu.make_async_copy(h_ref.at[pl.ds(r, 1)], xs_out.at[pl.ds(pos_ref[0, 0, k * tile + r], 1)],
                                  sem).start()
        return carry
    lax.fori_loop(0, tile, issue, 0, unroll=8)
    for k in range(2):
        pltpu.make_async_copy(h_ref, xs_out.at[pl.ds(0, tile)], sem).wait()


def _moe_dispatch(h, pos, tails, n_rows):
    n_tok, d = h.shape
    grid_spec = pltpu.PrefetchScalarGridSpec(
        num_scalar_prefetch=1,
        grid=(n_tok // TOK_TILE,),
        in_specs=[pl.BlockSpec((1, 1, 2 * TOK_TILE), lambda i, tl: (i, 0, 0), memory_space=pltpu.SMEM),
                  pl.BlockSpec((TOK_TILE, d), lambda i, tl: (i, 0))],
        out_specs=pl.BlockSpec(memory_space=pl.ANY),
        scratch_shapes=[pltpu.VMEM((MOE_TILE, d), h.dtype), pltpu.SemaphoreType.DMA(())])
    return pl.pallas_call(
        _dispatch_kernel,
        grid_spec=grid_spec,
        out_shape=jax.ShapeDtypeStruct((n_rows, d), h.dtype),
        compiler_params=_params(("arbitrary",)),
        name="expert_dispatch",
    )(tails, pos, h)


def _moe_kernel(te_ref, nact_ref, x_ref, wg_ref, wu_ref, wd_ref, o_ref, xb_ref, acc_ref):
    i, f = pl.program_id(0), pl.program_id(1)
    active = i < nact_ref[0]
    last = f == pl.num_programs(1) - 1

    @pl.when(f == 0)
    def _():
        xb_ref[...] = x_ref[...].astype(BF16)
        acc_ref[...] = jnp.zeros_like(acc_ref)

    @pl.when(active)
    def _():
        x = xb_ref[...]
        for j in range(wg_ref.shape[2] // FFN_CHUNK):
            cols = slice(j * FFN_CHUNK, (j + 1) * FFN_CHUNK)
            act = (_silu(_dot(x, wg_ref[0, :, cols])) * _dot(x, wu_ref[0, :, cols])).astype(BF16)
            acc_ref[...] += _dot(act, wd_ref[0, cols, :])

    @pl.when(last)
    def _():
        o_ref[...] = acc_ref[...]


def _moe_ffn(x_sorted, tile_expert, n_active, wg, wu, wd):
    n_rows, d = x_sorted.shape
    ffn = wg.shape[2]
    n_tiles = n_rows // MOE_TILE
    n_f = ffn // MOE_FCHUNK

    def row(i, f, te, na):
        return jnp.minimum(i, na[0] - 1)

    def fcol(i, f, te, na):
        return jnp.where(i < na[0], f, n_f - 1)

    grid_spec = pltpu.PrefetchScalarGridSpec(
        num_scalar_prefetch=2,
        grid=(n_tiles, n_f),
        in_specs=[pl.BlockSpec((MOE_TILE, d), lambda i, f, te, na: (row(i, f, te, na), 0)),
                  pl.BlockSpec((1, d, MOE_FCHUNK), lambda i, f, te, na: (te[i], 0, fcol(i, f, te, na))),
                  pl.BlockSpec((1, d, MOE_FCHUNK), lambda i, f, te, na: (te[i], 0, fcol(i, f, te, na))),
                  pl.BlockSpec((1, MOE_FCHUNK, d), lambda i, f, te, na: (te[i], fcol(i, f, te, na), 0))],
        out_specs=pl.BlockSpec((MOE_TILE, d), lambda i, f, te, na: (i, 0)),
        scratch_shapes=[pltpu.VMEM((MOE_TILE, d), BF16), pltpu.VMEM((MOE_TILE, d), F32)])
    return pl.pallas_call(
        _moe_kernel,
        grid_spec=grid_spec,
        out_shape=jax.ShapeDtypeStruct((n_rows, d), F32),
        compiler_params=_params(("arbitrary", "arbitrary")),
        name="expert_swiglu",
    )(tile_expert, n_active, x_sorted, wg, wu, wd)


def _final_kernel(pos_ref, next_pos_ref, x_ref, route_ref, mod_ref, g_ref, y_hbm, o_ref, buf_ref, sem):
    tile = x_ref.shape[0]
    i, n = pl.program_id(0), pl.num_programs(0)
    slot = i % 2

    def fetch(p_ref, s):
        def issue(r, carry):
            for k in range(2):
                pltpu.make_async_copy(y_hbm.at[pl.ds(p_ref[0, 0, k * tile + r], 1)],
                                      buf_ref.at[s, k, pl.ds(r, 1)], sem.at[s]).start()
            return carry
        lax.fori_loop(0, tile, issue, 0, unroll=8)

    @pl.when(i == 0)
    def _():
        fetch(pos_ref, slot)

    @pl.when(i + 1 < n)
    def _():
        fetch(next_pos_ref, 1 - slot)

    for k in range(2):
        pltpu.make_async_copy(y_hbm.at[pl.ds(0, tile)], buf_ref.at[slot, k], sem.at[slot]).wait()
    route = route_ref[...]
    y = route[:, 2:3] * buf_ref[slot, 0] + route[:, 3:4] * buf_ref[slot, 1]
    x = x_ref[...] + mod_ref[0, 0][5:6] * y
    o_ref[...] = x * lax.rsqrt(jnp.mean(x * x, axis=-1, keepdims=True) + RMS_EPS) * g_ref[...]


def _final_combine(x_lat, y_sorted, pos, route, mod, layer, g_final, dims):
    n_rows, d = x_lat.shape
    n_tiles = n_rows // TOK_TILE
    tiles_per_batch = dims["S"] // TOK_TILE
    return pl.pallas_call(
        _final_kernel,
        grid=(n_tiles,),
        in_specs=[pl.BlockSpec((1, 1, 2 * TOK_TILE), lambda i: (i, 0, 0), memory_space=pltpu.SMEM),
                  pl.BlockSpec((1, 1, 2 * TOK_TILE), lambda i: (jnp.minimum(i + 1, n_tiles - 1), 0, 0),
                               memory_space=pltpu.SMEM),
                  pl.BlockSpec((TOK_TILE, d), lambda i: (i, 0)),
                  pl.BlockSpec((TOK_TILE, LANES), lambda i: (i, 0)),
                  pl.BlockSpec((1, 1, 6, d), lambda i: (layer, i // tiles_per_batch, 0, 0)),
                  pl.BlockSpec((1, d), lambda i: (0, 0)),
                  pl.BlockSpec(memory_space=pl.ANY)],
        out_specs=pl.BlockSpec((TOK_TILE, d), lambda i: (i, 0)),
        out_shape=jax.ShapeDtypeStruct((n_rows, d), F32),
        scratch_shapes=[pltpu.VMEM((2, 2, TOK_TILE, d), F32), pltpu.SemaphoreType.DMA((2,))],
        compiler_params=_params(("arbitrary",)),
        name="expert_combine_final_norm",
    )(pos, pos, x_lat, route, mod, g_final, y_sorted)


def _rope_tables(s_len):
    t = np.arange(s_len)
    lane = np.arange(LANES) % HEAD_DIM
    quarter = HEAD_DIM // 4
    inv = 1.0 / (ROPE_BASE ** (jnp.arange(quarter, dtype=F32) / quarter))
    pos = np.where(lane[None, :] < HEAD_DIM // 2, (t // GRID_W)[:, None], (t % GRID_W)[:, None]).astype(np.float32)
    ang = jnp.asarray(pos) * inv[lane % quarter][None, :]
    first = jnp.asarray((lane % (HEAD_DIM // 2)) < quarter)[None, :]
    cos, sin = jnp.cos(ang), jnp.sin(ang)
    ident = jnp.zeros((TOK_TILE, LANES), F32)
    return (jnp.concatenate([cos, ident + 1.0]),
            jnp.concatenate([jnp.where(first, -sin, 0.0), ident]),
            jnp.concatenate([jnp.where(first, 0.0, sin), ident]))


def _rearranged_w_in(w):
    offs = np.concatenate([[0], np.cumsum(IN_SIZES)])
    part = lambda j: w[:, int(offs[j]):int(offs[j + 1])]
    pad = jnp.zeros((w.shape[0], LANES - 2 * GLA_LR), w.dtype)
    order = [0, 1, 2, 3, 6, 7, 9, 10, 4, 5]
    w_cat = jnp.concatenate([part(j) for j in order] + [pad], axis=1).astype(BF16)
    return w_cat, jnp.transpose(part(8)).astype(BF16), jnp.transpose(part(11)).astype(BF16)


def _decay_up(w_up, b):
    top = jnp.concatenate([w_up[0], jnp.zeros_like(w_up[0])], axis=1)
    bot = jnp.concatenate([jnp.zeros_like(w_up[1]), w_up[1]], axis=1)
    pad = jnp.zeros((LANES - 2 * GLA_LR, 2 * GLA_QK), w_up.dtype)
    return jnp.concatenate([top, bot, pad], axis=0).astype(BF16), jnp.concatenate([b[0], b[1]])[None, :]


def kernel(x, c, ctx, c_ctx, w_mod, b_mod, g_norm1, g_norm2, w_in, gla_w_dec_up, gla_b_dec, gla_g_norm,
           diff_lambda, diff_g_norm, na_rpb, w_out, w_ffn_gate, w_ffn_up, w_ffn_down, w_router, w_moe_gate,
           w_moe_up, w_moe_down, g_final):
    nb, s_len, d = x.shape
    c_len = ctx.shape[1]
    depth = w_mod.shape[0]
    assert c_len == ATT_TILE and s_len % TOK_TILE == 0 and (nb * c_len) % TOK_TILE == 0 and nb < MOD_ROWS
    n_lat = nb * s_len
    dims = {"B": nb, "S": s_len, "n_lat": n_lat, "n_tot": n_lat + nb * c_len}

    x_all = (x.reshape(n_lat, d), ctx.reshape(nb * c_len, d))
    cc = jnp.concatenate([c, c_ctx[None, :], jnp.zeros((MOD_ROWS - nb - 1, d), F32)], axis=0)
    mod = _modulation(cc, w_mod, b_mod).reshape(depth, MOD_ROWS, 6, d)
    rope_tabs = _rope_tables(s_len)

    for l in range(depth):
        last = l == depth - 1
        w_cat, w_dvt, w_nvt = _rearranged_w_in(w_in[l])
        wup2, bdec2 = _decay_up(gla_w_dec_up[l], gla_b_dec[l])
        gla, lg, dq, dk, dvt, nq, nk, nvt = _input_projection(
            x_all, mod, l, g_norm1[l][None, :], w_cat, w_dvt, w_nvt, wup2, bdec2, rope_tabs, dims)
        o_f, o_b = _gla(gla, lg, dims)
        b_diff = _diff_attention(dq, dk, dvt, diff_lambda[l], diff_g_norm[l][None, :], l, not last, dims)
        c_na = _neighbourhood_attention(nq, nk, nvt, _na_bias_tables(na_rpb[l], s_len // GRID_W), not last, dims)
        g_gla = jnp.tile(gla_g_norm[l], GLA_HEADS)[None, :]
        n_rows = n_lat if last else dims["n_tot"]
        if l % 2 == 0:
            i = l // 2
            x_mid, h = _output_projection(x_all, o_f, o_b, gla, b_diff, c_na, mod, l, g_gla, g_norm2[l][None, :],
                                          w_out[l].astype(BF16), None, n_rows, dims)
            assert not last, "the dense channel mixer is implemented for non-final layers only"
            x_all = _dense_ffn(h, x_mid, mod, l, w_ffn_gate[i].astype(BF16), w_ffn_up[i].astype(BF16),
                               w_ffn_down[i].astype(BF16), dims)
        else:
            i = l // 2
            w_r = jnp.concatenate([w_router[i], jnp.zeros((d, LANES - N_EXPERTS), F32)], axis=1)
            x_mid, h, route = _output_projection(x_all, o_f, o_b, gla, b_diff, c_na, mod, l, g_gla,
                                                 g_norm2[l][None, :], w_out[l].astype(BF16), w_r, n_rows, dims)
            assert last, "the expert layer is implemented for the final layer only"
            pos, tails, n_sorted, tile_expert, n_active = _moe_routing(route, n_rows)
            x_sorted = _moe_dispatch(h, pos, tails, n_sorted)
            y_sorted = _moe_ffn(x_sorted, tile_expert, n_active, w_moe_gate[i].astype(BF16),
                                w_moe_up[i].astype(BF16), w_moe_down[i].astype(BF16))
            return _final_combine(x_mid, y_sorted, pos, route, mod, l, g_final[None, :], dims).reshape(nb, s_len, d)
    raise NotImplementedError("the final layer must be an expert layer")
```

```python
import functools
import math

import numpy as np
import jax
import jax.numpy as jnp
from jax import lax
from jax.experimental import pallas as pl
from jax.experimental.pallas import tpu as pltpu

F32 = jnp.float32
BF16 = jnp.bfloat16
HIGHEST = lax.Precision.HIGHEST

GRID_W = 64
HEAD_DIM = 64
ROPE_BASE = 10000.0
RMS_EPS = 1e-6
GLA_HEADS = 4
GLA_DK = 32
GLA_DV = 64
GLA_LR = 16
GLA_NORMALIZER = 16.0
GLA_CHUNK = 64
DIFF_HEADS = 4
DIFF_HD = 64
NA_HEADS = 4
NA_HD = 64
WIN_R = 8
WIN_C = 16
N_EXPERTS = 8
GLA_QK = GLA_HEADS * GLA_DK
GLA_V = GLA_HEADS * GLA_DV
DIFF_W = DIFF_HEADS * 2 * DIFF_HD
NA_W = NA_HEADS * NA_HD
IN_SIZES = (GLA_QK, GLA_QK, GLA_V, GLA_V, GLA_LR, GLA_LR, DIFF_W, DIFF_W, DIFF_W, NA_W, NA_W, NA_W)

LANES = 128
TOK_TILE = 512
ATT_TILE = 256
DIFF_QT = 2048
DIFF_GW = 256
NA_QROWS = 4
NA_KROWS = 12
FFN_CHUNK = 256
MOE_TILE = 512
MOE_FCHUNK = 3584
MOD_ROWS = 16
NEG_BIG = -1e30
LOG2E = math.log2(math.e)
VMEM_LIMIT = 56 * 1024 * 1024

C_GLA = 0
C_DQ = C_GLA + 2 * GLA_QK + 2 * GLA_V
C_DK = C_DQ + DIFF_W
C_NQ = C_DK + DIFF_W
C_NK = C_NQ + NA_W
C_DEC = C_NK + NA_W
C_END = C_DEC + LANES


def _silu(x):
    return x * (1.0 / (1.0 + jnp.exp(-x)))


def _dot(a, b):
    return jnp.dot(a, b, preferred_element_type=F32)


def _dot_nt(a, b):
    return lax.dot_general(a, b, (((1,), (1,)), ((), ())), preferred_element_type=F32)


def _dot_tn(a, b):
    return lax.dot_general(a, b, (((0,), (0,)), ((), ())), preferred_element_type=F32)


def _params(sem, vmem=VMEM_LIMIT):
    return pltpu.CompilerParams(dimension_semantics=sem, vmem_limit_bytes=vmem)


def _mod_kernel(cc_ref, w_ref, b_ref, o_ref):
    s = _silu(cc_ref[...]).astype(BF16)
    o_ref[0] = _dot(s, w_ref[0].astype(BF16)) + b_ref[0]


def _modulation(cc, w_mod, b_mod):
    depth, d, six_d = w_mod.shape
    n = six_d // d
    return pl.pallas_call(
        _mod_kernel,
        grid=(depth, n),
        in_specs=[pl.BlockSpec((MOD_ROWS, d), lambda l, j: (0, 0)),
                  pl.BlockSpec((1, d, d), lambda l, j: (l, 0, j)),
                  pl.BlockSpec((1, 1, d), lambda l, j: (l, 0, j))],
        out_specs=pl.BlockSpec((1, MOD_ROWS, d), lambda l, j: (l, 0, j)),
        out_shape=jax.ShapeDtypeStruct((depth, MOD_ROWS, six_d), F32),
        compiler_params=_params(("parallel", "parallel")),
        name="modulation",
    )(cc, w_mod, b_mod.reshape(depth, 1, six_d))


def _rope(z, a, bt, ct):
    outs = []
    for s in range(z.shape[1] // LANES):
        zs = z[:, s * LANES:(s + 1) * LANES]
        outs.append(zs * a + pltpu.roll(zs, LANES - 16, 1) * bt + pltpu.roll(zs, 16, 1) * ct)
    return jnp.concatenate(outs, axis=1)


def _split_rows(x, n_lat_tiles, width):
    if not isinstance(x, tuple):
        return [x], [pl.BlockSpec((TOK_TILE, width), lambda i: (i, 0))]
    return list(x), [pl.BlockSpec((TOK_TILE, width), lambda i: (jnp.minimum(i, n_lat_tiles - 1), 0)),
                     pl.BlockSpec((TOK_TILE, width), lambda i: (jnp.maximum(i - n_lat_tiles, 0), 0))]


def _joined_rows(n_lat_tiles, refs):
    if len(refs) == 1:
        return refs[0][...]
    return jnp.where(pl.program_id(0) < n_lat_tiles, refs[0][...], refs[1][...])


def _inproj_kernel(n_lat_tiles, n_x, *refs):
    (mod_ref, g_ref, w_ref, wvt_ref, wnt_ref, wup_ref, bdec_ref, ra_ref, rb_ref, rc_ref,
     gla_ref, lg_ref, dq_ref, dk_ref, dvt_ref, nq_ref, nk_ref, nvt_ref) = refs[n_x:]
    x = _joined_rows(n_lat_tiles, refs[:n_x])
    m = mod_ref[0, 0]
    h = x * lax.rsqrt(jnp.mean(x * x, axis=-1, keepdims=True) + RMS_EPS) * g_ref[...]
    hb = (h * (1.0 + m[1:2]) + m[0:1]).astype(BF16)

    def proj(c0, c1):
        return _dot(hb, w_ref[:, c0:c1])

    gla_ref[...] = proj(C_GLA, C_DQ).astype(BF16)
    a, bt, ct = ra_ref[...], rb_ref[...], rc_ref[...]
    dq_ref[...] = (_rope(proj(C_DQ, C_DK), a, bt, ct) * (DIFF_HD ** -0.5 * LOG2E)).astype(BF16)
    dk_ref[...] = _rope(proj(C_DK, C_NQ), a, bt, ct).astype(BF16)
    dvt_ref[0] = _dot_nt(wvt_ref[...], hb).astype(BF16)
    nq_ref[...] = (proj(C_NQ, C_NK) * (NA_HD ** -0.5 * LOG2E)).astype(BF16)
    nk_ref[...] = proj(C_NK, C_DEC).astype(BF16)
    for j in range(nvt_ref.shape[0]):
        nvt_ref[j] = _dot_nt(wnt_ref[...], hb[j * ATT_TILE:(j + 1) * ATT_TILE]).astype(BF16)
    logits = _dot(proj(C_DEC, C_END).astype(BF16), wup_ref[...]) + bdec_ref[...]
    log_sig = jnp.minimum(logits, 0.0) - jnp.log1p(jnp.exp(-jnp.abs(logits)))
    lg_ref[...] = log_sig * (1.0 / GLA_NORMALIZER)


def _input_projection(x_all, mod, layer, g1, w_cat, w_dvt, w_nvt, wup2, bdec2, rope_tabs, dims):
    n_tot, d = dims["n_tot"], g1.shape[1]
    n_lat_tiles = dims["n_lat"] // TOK_TILE
    x_args, x_specs = _split_rows(x_all, n_lat_tiles, d)
    tiles_per_batch = dims["S"] // TOK_TILE
    n_tiles = n_tot // TOK_TILE
    nb = dims["B"]

    def mod_map(i):
        return (layer, jnp.where(i < n_lat_tiles, i // tiles_per_batch, nb), 0, 0)

    def rope_map(i):
        return (jnp.where(i < n_lat_tiles, i % tiles_per_batch, tiles_per_batch), 0)

    row = lambda w: pl.BlockSpec((TOK_TILE, w), lambda i: (i, 0))
    const = lambda shape: pl.BlockSpec(shape, lambda i: (0,) * len(shape))
    tab = pl.BlockSpec((TOK_TILE, LANES), rope_map)
    sub = TOK_TILE // ATT_TILE
    out_specs = [row(C_DQ), row(2 * GLA_QK), row(DIFF_W), row(DIFF_W),
                 pl.BlockSpec((1, DIFF_W, TOK_TILE), lambda i: (i, 0, 0)), row(NA_W), row(NA_W),
                 pl.BlockSpec((sub, NA_W, ATT_TILE), lambda i: (i, 0, 0))]
    shape = lambda w, dt=BF16: jax.ShapeDtypeStruct((n_tot, w), dt)
    out_shape = [shape(C_DQ), shape(2 * GLA_QK, F32), shape(DIFF_W), shape(DIFF_W),
                 jax.ShapeDtypeStruct((n_tiles, DIFF_W, TOK_TILE), BF16), shape(NA_W), shape(NA_W),
                 jax.ShapeDtypeStruct((n_tiles * sub, NA_W, ATT_TILE), BF16)]
    return pl.pallas_call(
        functools.partial(_inproj_kernel, n_lat_tiles, len(x_args)),
        grid=(n_tiles,),
        in_specs=x_specs + [pl.BlockSpec((1, 1, 6, d), mod_map), const((1, d)), const((d, C_END)),
                            const((DIFF_W, d)), const((NA_W, d)), const((LANES, 2 * GLA_QK)),
                            const((1, 2 * GLA_QK)), tab, tab, tab],
        out_specs=out_specs,
        out_shape=out_shape,
        compiler_params=_params(("parallel",)),
        name="input_projection",
    )(*x_args, mod, g1, w_cat, w_dvt, w_nvt, wup2, bdec2, *rope_tabs)


def _gla_block_local(g_ref, l_ref, fwd, consts):
    tri, same, tri4, head_k, head_v, bd = consts
    t = g_ref.shape[0]
    q = g_ref[:, 0:GLA_QK].astype(F32) * GLA_DK ** -0.5
    k = g_ref[:, GLA_QK:2 * GLA_QK].astype(F32)
    v = g_ref[:, 2 * GLA_QK:2 * GLA_QK + GLA_V]
    lcol = 0 if fwd else GLA_QK
    lg = l_ref[:, lcol:lcol + GLA_QK]
    lg_hi = lg.astype(BF16)
    lg_lo = (lg - lg_hi.astype(F32)).astype(BF16)
    b = _dot(tri, lg_hi) + _dot(tri, lg_lo)
    total = _dot(same, lg_hi) + _dot(same, lg_lo)
    q_dec = q * jnp.exp(b)
    k_inv = (k * jnp.exp(-b)).astype(BF16)
    k_end = (k * jnp.exp(total - b)).astype(BF16)
    qs = jnp.concatenate([jnp.where(head_k == h, q_dec, 0.0) for h in range(GLA_HEADS)], axis=0)
    att = jnp.where(tri4, _dot_nt(qs.astype(BF16), k_inv), 0.0)
    ov = _dot(att.astype(BF16), v)
    o = jnp.where(head_v == 0, ov[0:t], 0.0)
    for h in range(1, GLA_HEADS):
        o = o + jnp.where(head_v == h, ov[h * t:(h + 1) * t], 0.0)
    q_dec = q_dec.astype(BF16)
    chunks = []
    for c in range(t // GLA_CHUNK):
        rows = slice(c * GLA_CHUNK, (c + 1) * GLA_CHUNK)
        chunks.append((o[rows], q_dec[rows], jnp.where(bd, _dot_tn(v[rows], k_end[rows]), 0.0),
                       jnp.exp(total[c * GLA_CHUNK:c * GLA_CHUNK + 1])))
    return chunks


def _gla_kernel(gf_ref, lf_ref, gb_ref, lb_ref, of_ref, ob_ref, sf_ref, sb_ref):
    @pl.when(pl.program_id(1) == 0)
    def _():
        sf_ref[...] = jnp.zeros_like(sf_ref)
        sb_ref[...] = jnp.zeros_like(sb_ref)

    t = gf_ref.shape[0]
    r = lax.broadcasted_iota(jnp.int32, (t, t), 0)
    c = lax.broadcasted_iota(jnp.int32, (t, t), 1)
    r4 = lax.broadcasted_iota(jnp.int32, (GLA_HEADS * t, t), 0) % t
    c4 = lax.broadcasted_iota(jnp.int32, (GLA_HEADS * t, t), 1)
    same = r // GLA_CHUNK == c // GLA_CHUNK
    same4 = r4 // GLA_CHUNK == c4 // GLA_CHUNK
    head_k = lax.broadcasted_iota(jnp.int32, (1, GLA_QK), 1) // GLA_DK
    head_v = lax.broadcasted_iota(jnp.int32, (1, GLA_V), 1) // GLA_DV
    bd = (lax.broadcasted_iota(jnp.int32, (GLA_V, GLA_QK), 0) // GLA_DV
          == lax.broadcasted_iota(jnp.int32, (GLA_V, GLA_QK), 1) // GLA_DK)
    same_b = same.astype(BF16)
    lower = (jnp.logical_and(same, c <= r).astype(BF16), same_b, jnp.logical_and(same4, c4 <= r4), head_k, head_v, bd)
    upper = (jnp.logical_and(same, c >= r).astype(BF16), same_b, jnp.logical_and(same4, c4 >= r4), head_k, head_v, bd)
    n_chunks = t // GLA_CHUNK
    scans = ((gf_ref, lf_ref, of_ref, sf_ref, True, lower, list(range(n_chunks))),
             (gb_ref, lb_ref, ob_ref, sb_ref, False, upper, list(range(n_chunks - 1, -1, -1))))
    local = [_gla_block_local(g_ref, l_ref, fwd, consts) for g_ref, l_ref, _, _, fwd, consts, _ in scans]
    for (_, _, o_ref, s_ref, _, _, order), chunks in zip(scans, local):
        s = s_ref[...]
        for cidx, (o_intra, q_dec, inc, decay) in ((cidx, chunks[cidx]) for cidx in order):
            o = o_intra + _dot_nt(q_dec, s.astype(BF16))
            o_ref[cidx * GLA_CHUNK:(cidx + 1) * GLA_CHUNK, :] = o.astype(o_ref.dtype)
            s = decay * s + inc
        s_ref[...] = s


def _gla(gla, lg, dims):
    n_tot = gla.shape[0]
    nb, steps = dims["B"], 1 + dims["S"] // ATT_TILE
    lat_blocks = dims["n_lat"] // ATT_TILE
    per_batch = dims["S"] // ATT_TILE

    def fwd_map(b, i):
        return (jnp.where(i == 0, lat_blocks + b, b * per_batch + i - 1), 0)

    def bwd_map(b, i):
        return (jnp.where(i == 0, lat_blocks + b, b * per_batch + per_batch - i), 0)

    return pl.pallas_call(
        _gla_kernel,
        grid=(nb, steps),
        in_specs=[pl.BlockSpec((ATT_TILE, C_DQ), fwd_map), pl.BlockSpec((ATT_TILE, 2 * GLA_QK), fwd_map),
                  pl.BlockSpec((ATT_TILE, C_DQ), bwd_map), pl.BlockSpec((ATT_TILE, 2 * GLA_QK), bwd_map)],
        out_specs=[pl.BlockSpec((ATT_TILE, GLA_V), fwd_map), pl.BlockSpec((ATT_TILE, GLA_V), bwd_map)],
        out_shape=[jax.ShapeDtypeStruct((n_tot, GLA_V), BF16)] * 2,
        scratch_shapes=[pltpu.VMEM((GLA_V, GLA_QK), F32), pltpu.VMEM((GLA_V, GLA_QK), F32)],
        compiler_params=_params(("parallel", "arbitrary")),
        name="gla_scan",
    )(gla, lg, gla, lg)


def _diff_kernel(lam_init, latent, *refs):
    if latent:
        q_ref, kl_ref, vl_ref, kc_ref, vc_ref, lam_ref, g_ref, o_ref, *s_refs = refs
    else:
        q_ref, kc_ref, vc_ref, lam_ref, g_ref, o_ref, *s_refs = refs
    q = q_ref[...]
    t, hw = q.shape
    c_len = kc_ref.shape[0]
    lane = lax.broadcasted_iota(jnp.int32, q.shape, 1)
    zero = jnp.zeros_like(q)
    qm = (jnp.where(lane < DIFF_HD, q, zero), jnp.where(lane >= DIFF_HD, q, zero))

    tiles = [(lambda: kc_ref[...], lambda: vc_ref[0], 0, c_len)]
    if latent:
        tiles += [(lambda j=j: kl_ref[j * TOK_TILE:(j + 1) * TOK_TILE, :], lambda j=j: vl_ref[j],
                   c_len + j * TOK_TILE, TOK_TILE) for j in range(vl_ref.shape[0])]

    gw = min(t, DIFF_GW)
    groups = [(mp, c0) for mp in range(2) for c0 in range(0, t, gw)]

    def pass1(g, tile, m):
        mp, c0 = groups[g]
        load_k, _, off, n = tile
        s = _dot_nt(load_k(), qm[mp][c0:c0 + gw])
        s_refs[g % 2][off:off + n, :] = s
        return jnp.maximum(m, jnp.max(s, axis=0, keepdims=True))

    def pass2(g, tile, m, l, acc):
        _, load_vt, off, n = tile
        p = jnp.exp2(s_refs[g % 2][off:off + n, :] - m)
        return l + jnp.sum(p, axis=0, keepdims=True), acc + _dot(load_vt(), p.astype(BF16))

    m_prev, outs = None, []
    for g in range(len(groups) + 1):
        m = jnp.full((1, gw), -jnp.inf, F32)
        l, acc = jnp.zeros((1, gw), F32), jnp.zeros((hw, gw), F32)
        for tile in tiles:
            if g < len(groups):
                m = pass1(g, tile, m)
            if g > 0:
                l, acc = pass2(g - 1, tile, m_prev, l, acc)
        if g > 0:
            outs.append(acc / l)
        m_prev = m
    n_half = len(groups) // 2
    on = [jnp.concatenate(outs[i * n_half:(i + 1) * n_half], axis=1) for i in range(2)]

    lam = lam_ref[...]
    lam_full = (jnp.exp(jnp.sum(lam[0:1] * lam[1:2], axis=-1, keepdims=True))
                - jnp.exp(jnp.sum(lam[2:3] * lam[3:4], axis=-1, keepdims=True)) + lam_init)
    o = jnp.transpose(on[0] - lam_full * on[1])
    o = o * lax.rsqrt(jnp.mean(o * o, axis=-1, keepdims=True) + RMS_EPS) * g_ref[...]
    o_ref[...] = (o * (1.0 - lam_init)).astype(BF16)


def _diff_attention(dq, dk, dvt, lam, g_diff, layer, with_ctx, dims):
    n_tot = dq.shape[0]
    nb, s_len = dims["B"], dims["S"]
    qt = min(DIFF_QT, s_len)
    n_lat_q = s_len // qt
    lat_blocks = dims["n_lat"] // ATT_TILE
    lat_tiles = dims["n_lat"] // TOK_TILE
    ctx_per_tile = TOK_TILE // ATT_TILE
    lam_init = 0.8 - 0.6 * math.exp(-0.3 * layer)
    hw = 2 * DIFF_HD
    out_shape = jax.ShapeDtypeStruct((dims["n_lat"], DIFF_W), BF16)

    k_ctx = pl.BlockSpec((ATT_TILE, hw), lambda b, h, *_: (lat_blocks + b, h))
    v_ctx = pl.BlockSpec((1, hw, ATT_TILE), lambda b, h, *_: (lat_tiles + b // ctx_per_tile, h, b % ctx_per_tile))
    small = [pl.BlockSpec((4, DIFF_HD), lambda *_: (0, 0)), pl.BlockSpec((1, hw), lambda *_: (0, 0))]
    q_lat = pl.BlockSpec((qt, hw), lambda b, h, i: (b * n_lat_q + i, h))
    out = pl.pallas_call(
        functools.partial(_diff_kernel, lam_init, True),
        grid=(nb, DIFF_HEADS, n_lat_q),
        in_specs=[q_lat, pl.BlockSpec((s_len, hw), lambda b, h, i: (b, h)),
                  pl.BlockSpec((s_len // TOK_TILE, hw, TOK_TILE), lambda b, h, i: (b, h, 0)), k_ctx, v_ctx] + small,
        out_specs=q_lat,
        out_shape=out_shape,
        scratch_shapes=[pltpu.VMEM((ATT_TILE + s_len, DIFF_GW), F32)] * 2,
        compiler_params=_params(("parallel", "parallel", "arbitrary")),
        name="diff_attention",
    )(dq, dk, dvt, dk, dvt, lam, g_diff)
    if not with_ctx:
        return out, None
    out_ctx = pl.pallas_call(
        functools.partial(_diff_kernel, lam_init, False),
        grid=(nb, DIFF_HEADS),
        in_specs=[pl.BlockSpec((ATT_TILE, hw), lambda b, h: (lat_blocks + b, h)), k_ctx, v_ctx] + small,
        out_specs=pl.BlockSpec((ATT_TILE, hw), lambda b, h: (b, h)),
        out_shape=jax.ShapeDtypeStruct((nb * ATT_TILE, DIFF_W), BF16),
        scratch_shapes=[pltpu.VMEM((ATT_TILE, ATT_TILE), F32)] * 2,
        compiler_params=_params(("parallel", "parallel")),
        name="diff_attention_context",
    )(dq, dk, dvt, lam, g_diff)
    return out, out_ctx


def _na_bias_tables(rpb, rows):
    wr = min(WIN_R, rows)
    n_groups = rows // NA_QROWS
    qc = np.arange(GRID_W)[None, :]
    kc = np.arange(GRID_W)[:, None]
    cs = np.clip(qc - WIN_C // 2, 0, GRID_W - WIN_C)
    col_valid = (kc >= cs) & (kc < cs + WIN_C)
    col_sel = ((kc - qc + WIN_C - 1)[None] == np.arange(2 * WIN_C - 1)[:, None, None]) & col_valid[None]
    row_sel = []
    for g in (0, 1, n_groups - 1):
        r0 = g * NA_QROWS
        u0 = int(np.clip(r0 - WIN_R // 2, 0, rows - NA_KROWS))
        r = r0 + np.arange(NA_QROWS)[None, :]
        kr = u0 + np.arange(NA_KROWS)[:, None]
        rs = np.clip(r - WIN_R // 2, 0, rows - wr)
        row_valid = (kr >= rs) & (kr < rs + wr)
        row_sel.append(((kr - r + WIN_R - 1)[None] == np.arange(2 * WIN_R - 1)[:, None, None]) & row_valid[None])
    row_sel = np.stack(row_sel)
    t = jnp.einsum("hrc,vrkd,cxq->vkxhdq", rpb.astype(F32) * LOG2E, jnp.asarray(row_sel, F32),
                   jnp.asarray(col_sel, F32), precision=HIGHEST)
    valid = row_sel.any(axis=1)[:, :, None, None, :, None] & col_valid[None, None, :, None, None, :]
    t = jnp.where(valid, t, NEG_BIG)
    return t.reshape(3, NA_KROWS // NA_QROWS, ATT_TILE, NA_HEADS * ATT_TILE)


def _na_kernel(rows, latent, *refs):
    if latent:
        q_ref, kl_ref, vl_ref, kc_ref, vc_ref, bias_ref, o_ref, *s_refs = refs
    else:
        q_ref, kc_ref, vc_ref, o_ref, *s_refs = refs
    q = q_ref[...]
    t = q.shape[0]
    head = lax.broadcasted_iota(jnp.int32, q.shape, 1) // NA_HD
    zero = jnp.zeros_like(q)
    qs = jnp.concatenate([jnp.where(head == h, q, zero) for h in range(NA_HEADS)], axis=0)

    n_local = NA_KROWS // NA_QROWS
    s = _dot_nt(kc_ref[...], qs)
    s_refs[0][...] = s
    m = jnp.max(s, axis=0, keepdims=True)
    values = [lambda: vc_ref[0]]
    if latent:
        t0 = jnp.clip(pl.program_id(1) - 1, 0, rows // NA_QROWS - n_local)
        k_local = kl_ref[pl.ds(pl.multiple_of(t0 * t, t), n_local * t), :]
        s = _dot_nt(k_local, qs) + bias_ref[0].reshape(n_local * t, NA_HEADS * t)
        for j in range(n_local):
            s_refs[1 + j][...] = s[j * t:(j + 1) * t]
            values.append(lambda j=j: vl_ref[t0 + j])
        m = jnp.maximum(m, jnp.max(s, axis=0, keepdims=True))
    l = jnp.zeros((1, NA_HEADS * t), F32)
    acc = [jnp.zeros((NA_HD, t), F32)] * NA_HEADS
    for i, load_vt in enumerate(values):
        p = jnp.exp2(s_refs[i][...] - m)
        l = l + jnp.sum(p, axis=0, keepdims=True)
        pb, vt = p.astype(BF16), load_vt()
        for h in range(NA_HEADS):
            acc[h] = acc[h] + _dot(vt[h * NA_HD:(h + 1) * NA_HD, :], pb[:, h * t:(h + 1) * t])
    o = jnp.concatenate([acc[h] / l[:, h * t:(h + 1) * t] for h in range(NA_HEADS)], axis=0)
    o_ref[...] = jnp.transpose(o).astype(BF16)


def _neighbourhood_attention(nq, nk, nvt, bias_tabs, with_ctx, dims):
    nb, s_len = dims["B"], dims["S"]
    rows = s_len // GRID_W
    n_groups = rows // NA_QROWS
    lat_blocks = dims["n_lat"] // ATT_TILE
    n_key_tiles = 1 + NA_KROWS // NA_QROWS

    def bias_map(b, g):
        return (jnp.where(g == 0, 0, jnp.where(g >= n_groups - 1, 2, 1)), 0, 0, 0)

    k_ctx = pl.BlockSpec((ATT_TILE, NA_W), lambda b, *_: (lat_blocks + b, 0))
    v_ctx = pl.BlockSpec((1, NA_W, ATT_TILE), lambda b, *_: (lat_blocks + b, 0, 0))
    q_lat = pl.BlockSpec((ATT_TILE, NA_W), lambda b, g: (b * n_groups + g, 0))
    scratch = lambda n: [pltpu.VMEM((ATT_TILE, NA_HEADS * ATT_TILE), F32)] * n
    lat = pl.pallas_call(
        functools.partial(_na_kernel, rows, True),
        grid=(nb, n_groups),
        in_specs=[q_lat, pl.BlockSpec((s_len, NA_W), lambda b, g: (b, 0)),
                  pl.BlockSpec((s_len // ATT_TILE, NA_W, ATT_TILE), lambda b, g: (b, 0, 0)), k_ctx, v_ctx,
                  pl.BlockSpec((1,) + bias_tabs.shape[1:], bias_map)],
        out_specs=q_lat,
        out_shape=jax.ShapeDtypeStruct((dims["n_lat"], NA_W), BF16),
        scratch_shapes=scratch(n_key_tiles),
        compiler_params=_params(("parallel", "arbitrary")),
        name="neighbourhood_attention",
    )(nq, nk, nvt, nk, nvt, bias_tabs)
    if not with_ctx:
        return lat, None
    ctx = pl.pallas_call(
        functools.partial(_na_kernel, rows, False),
        grid=(nb,),
        in_specs=[pl.BlockSpec((ATT_TILE, NA_W), lambda b: (lat_blocks + b, 0)), k_ctx, v_ctx],
        out_specs=pl.BlockSpec((ATT_TILE, NA_W), lambda b: (b, 0)),
        out_shape=jax.ShapeDtypeStruct((nb * ATT_TILE, NA_W), BF16),
        scratch_shapes=scratch(1),
        compiler_params=_params(("parallel",)),
        name="context_attention",
    )(nq, nk, nvt)
    return lat, ctx


def _route_top2(logits):
    lane = lax.broadcasted_iota(jnp.int32, logits.shape, 1)
    lane_f = lane.astype(F32)
    lg = jnp.where(lane < N_EXPERTS, logits, -jnp.inf)
    m1 = jnp.max(lg, axis=-1, keepdims=True)
    i1 = jnp.min(jnp.where(lg == m1, lane_f, float(LANES)), axis=-1, keepdims=True)
    lg2 = jnp.where(lane_f == i1, -jnp.inf, lg)
    m2 = jnp.max(lg2, axis=-1, keepdims=True)
    i2 = jnp.min(jnp.where(lg2 == m2, lane_f, float(LANES)), axis=-1, keepdims=True)
    e = jnp.exp(m2 - m1)
    w1 = 1.0 / (1.0 + e)
    w2 = e / (1.0 + e)
    return jnp.where(lane == 0, i1, jnp.where(lane == 1, i2, jnp.where(lane == 2, w1,
                                                                         jnp.where(lane == 3, w2, 0.0))))


def _outproj_kernel(moe, n_lat_tiles, n_x, *refs):
    x_in = _joined_rows(n_lat_tiles, refs[:n_x])
    of_ref, ob_ref, gate_ref, bd_ref, cn_ref, *rest = refs[n_x:]
    if moe:
        mod_ref, ggla_ref, g2_ref, w_ref, wr_ref, x_out_ref, h_ref, route_ref = rest
        b_diff, c_na = bd_ref[...], cn_ref[...]
    else:
        bdc_ref, cnc_ref, mod_ref, ggla_ref, g2_ref, w_ref, x_out_ref, h_ref = rest
        is_lat = pl.program_id(0) < n_lat_tiles
        b_diff = jnp.where(is_lat, bd_ref[...], bdc_ref[...])
        c_na = jnp.where(is_lat, cn_ref[...], cnc_ref[...])
    m = mod_ref[0, 0]
    o = of_ref[...].astype(F32) + ob_ref[...].astype(F32)
    avg = (lax.broadcasted_iota(jnp.int32, (GLA_V, GLA_V), 0) // GLA_DV
           == lax.broadcasted_iota(jnp.int32, (GLA_V, GLA_V), 1) // GLA_DV).astype(F32) * (1.0 / GLA_DV)
    ms = jnp.dot(o * o, avg, precision=HIGHEST, preferred_element_type=F32)
    a = o * lax.rsqrt(ms + RMS_EPS) * ggla_ref[...] * _silu(gate_ref[...].astype(F32))
    y = (_dot(a.astype(BF16), w_ref[0:GLA_V]) + _dot(b_diff, w_ref[GLA_V:GLA_V + DIFF_W])
         + _dot(c_na, w_ref[GLA_V + DIFF_W:]))
    x = x_in + m[2:3] * y
    x_out_ref[...] = x
    h = x * lax.rsqrt(jnp.mean(x * x, axis=-1, keepdims=True) + RMS_EPS) * g2_ref[...]
    h = h * (1.0 + m[4:5]) + m[3:4]
    h_ref[...] = h.astype(h_ref.dtype)
    if moe:
        h_hi, w = h.astype(BF16), wr_ref[...]
        h_lo = (h - h_hi.astype(F32)).astype(BF16)
        w_hi = w.astype(BF16)
        w_lo = (w - w_hi.astype(F32)).astype(BF16)
        route_ref[...] = _route_top2(_dot(h_hi, w_hi) + (_dot(h_hi, w_lo) + _dot(h_lo, w_hi)))


def _output_projection(x_all, o_f, o_b, gla, b_diff, c_na, mod, layer, g_gla, g2, w_out, w_router, n_rows, dims):
    d = w_out.shape[1]
    n_lat_tiles = dims["n_lat"] // TOK_TILE
    tiles_per_batch = dims["S"] // TOK_TILE
    nb = dims["B"]
    moe = w_router is not None
    assert moe == (b_diff[1] is None) and n_rows == (dims["n_lat"] if moe else dims["n_tot"])
    x_args, x_specs = _split_rows(x_all, n_lat_tiles, d)

    def mod_map(i):
        return (layer, jnp.where(i < n_lat_tiles, i // tiles_per_batch, nb), 0, 0)

    row = lambda w, cb=0: pl.BlockSpec((TOK_TILE, w), lambda i: (i, cb))
    lat_row = lambda w: pl.BlockSpec((TOK_TILE, w), lambda i: (jnp.minimum(i, n_lat_tiles - 1), 0))
    ctx_row = lambda w: pl.BlockSpec((TOK_TILE, w), lambda i: (jnp.maximum(i - n_lat_tiles, 0), 0))
    const = lambda shape: pl.BlockSpec(shape, lambda i: (0,) * len(shape))
    in_specs = x_specs + [row(GLA_V), row(GLA_V), row(GLA_V, (2 * GLA_QK + GLA_V) // GLA_V),
                          lat_row(DIFF_W), lat_row(NA_W)]
    args = x_args + [o_f, o_b, gla, b_diff[0], c_na[0]]
    if not moe:
        in_specs += [ctx_row(DIFF_W), ctx_row(NA_W)]
        args += [b_diff[1], c_na[1]]
    in_specs += [pl.BlockSpec((1, 1, 6, d), mod_map), const((1, GLA_V)), const((1, d)), const((d, d))]
    args += [mod, g_gla, g2, w_out]
    out_specs = [row(d), row(d)]
    out_shape = [jax.ShapeDtypeStruct((n_rows, d), F32), jax.ShapeDtypeStruct((n_rows, d), F32 if moe else BF16)]
    if moe:
        in_specs.append(const((d, LANES)))
        args.append(w_router)
        out_specs.append(row(LANES))
        out_shape.append(jax.ShapeDtypeStruct((n_rows, LANES), F32))
    return pl.pallas_call(
        functools.partial(_outproj_kernel, moe, n_lat_tiles, len(x_args)),
        grid=(n_rows // TOK_TILE,),
        in_specs=in_specs, out_specs=out_specs, out_shape=out_shape,
        compiler_params=_params(("parallel",)),
        name="output_projection",
    )(*args)


def _ffn_kernel(h_ref, x_ref, mod_ref, wg_ref, wu_ref, wd_ref, o_ref, acc_ref):
    h = h_ref[...]
    n_chunks = wg_ref.shape[1] // FFN_CHUNK
    for j in range(n_chunks):
        cols = slice(j * FFN_CHUNK, (j + 1) * FFN_CHUNK)
        act = (_silu(_dot(h, wg_ref[:, cols])) * _dot(h, wu_ref[:, cols])).astype(BF16)
        part = _dot(act, wd_ref[cols, :])
        if j == 0:
            acc_ref[...] = part
        else:
            acc_ref[...] += part
    o_ref[...] = x_ref[...] + mod_ref[0, 0][5:6] * acc_ref[...]


def _dense_ffn(h, x_all, mod, layer, wg, wu, wd, dims):
    n_rows, d = x_all.shape
    f = wg.shape[1]
    n_lat_tiles = dims["n_lat"] // TOK_TILE
    tiles_per_batch = dims["S"] // TOK_TILE
    nb = dims["B"]

    def mod_map(i):
        return (layer, jnp.where(i < n_lat_tiles, i // tiles_per_batch, nb), 0, 0)

    row = lambda: pl.BlockSpec((TOK_TILE, d), lambda i: (i, 0))
    const = lambda shape: pl.BlockSpec(shape, lambda i: (0, 0))
    return pl.pallas_call(
        _ffn_kernel,
        grid=(n_rows // TOK_TILE,),
        in_specs=[row(), row(), pl.BlockSpec((1, 1, 6, d), mod_map), const((d, f)), const((d, f)), const((f, d))],
        out_specs=row(),
        out_shape=jax.ShapeDtypeStruct((n_rows, d), F32),
        scratch_shapes=[pltpu.VMEM((TOK_TILE, d), F32)],
        compiler_params=_params(("parallel",)),
        name="dense_swiglu",
    )(h, x_all, mod, wg, wu, wd)


def _moe_routing(route, n_tok):
    e_flat = jnp.concatenate([route[:, 0], route[:, 1]]).astype(jnp.int32)
    onehot = (e_flat[:, None] == jnp.arange(N_EXPERTS, dtype=jnp.int32)[None, :]).astype(jnp.int32)
    csum = jnp.cumsum(onehot, axis=0)
    counts = csum[-1]
    padded = ((counts + MOE_TILE - 1) // MOE_TILE) * MOE_TILE
    ends = jnp.cumsum(padded)
    pos = jnp.sum(onehot * (csum - 1 + (ends - padded)[None, :]), axis=1)
    n_rows = 2 * n_tok + N_EXPERTS * MOE_TILE
    tile_start = jnp.arange(n_rows // MOE_TILE, dtype=jnp.int32) * MOE_TILE
    tile_expert = jnp.minimum(jnp.sum((tile_start[:, None] >= ends[None, :]).astype(jnp.int32), axis=1),
                              N_EXPERTS - 1).astype(jnp.int32)
    n_active = (ends[-1:] // MOE_TILE).astype(jnp.int32)
    unused = ends[-1] + jnp.arange(N_EXPERTS, dtype=jnp.int32) * MOE_TILE
    tails = jnp.concatenate([jnp.where(padded > 0, ends - MOE_TILE, -1),
                             jnp.where(unused < n_rows, unused, -1)]).astype(jnp.int32)
    n_tiles = n_tok // TOK_TILE
    pos = jnp.concatenate([pos[:n_tok].reshape(n_tiles, 1, TOK_TILE), pos[n_tok:].reshape(n_tiles, 1, TOK_TILE)],
                          axis=2)
    return pos, tails, n_rows, tile_expert, n_active


def _dispatch_kernel(tail_ref, pos_ref, h_ref, xs_out, zero_ref, sem):
    tile = h_ref.shape[0]

    @pl.when(pl.program_id(0) == 0)
    def _():
        zero_ref[...] = jnp.zeros_like(zero_ref)
        for e in range(2 * N_EXPERTS):
            @pl.when(tail_ref[e] >= 0)
            def _():
                rows = pl.ds(pl.multiple_of(tail_ref[e], MOE_TILE), MOE_TILE)
                tail = pltpu.make_async_copy(zero_ref, xs_out.at[rows], sem)
                tail.start()
                tail.wait()

    def issue(r, carry):
        for k in range(2):
            pltpu.make_async_copy(h_ref.at[pl.ds(r, 1)], xs_out.at[pl.ds(pos_ref[0, 0, k * tile + r], 1)],
                                  sem).start()
        return carry
    lax.fori_loop(0, tile, issue, 0, unroll=8)
    for k in range(2):
        pltpu.make_async_copy(h_ref, xs_out.at[pl.ds(0, tile)], sem).wait()


def _moe_dispatch(h, pos, tails, n_rows):
    n_tok, d = h.shape
    grid_spec = pltpu.PrefetchScalarGridSpec(
        num_scalar_prefetch=1,
        grid=(n_tok // TOK_TILE,),
        in_specs=[pl.BlockSpec((1, 1, 2 * TOK_TILE), lambda i, tl: (i, 0, 0), memory_space=pltpu.SMEM),
                  pl.BlockSpec((TOK_TILE, d), lambda i, tl: (i, 0))],
        out_specs=pl.BlockSpec(memory_space=pl.ANY),
        scratch_shapes=[pltpu.VMEM((MOE_TILE, d), h.dtype), pltpu.SemaphoreType.DMA(())])
    return pl.pallas_call(
        _dispatch_kernel,
        grid_spec=grid_spec,
        out_shape=jax.ShapeDtypeStruct((n_rows, d), h.dtype),
        compiler_params=_params(("arbitrary",)),
        name="expert_dispatch",
    )(tails, pos, h)


def _moe_kernel(te_ref, nact_ref, x_ref, wg_ref, wu_ref, wd_ref, o_ref, xb_ref, acc_ref):
    i, f = pl.program_id(0), pl.program_id(1)
    active = i < nact_ref[0]
    last = f == pl.num_programs(1) - 1

    @pl.when(f == 0)
    def _():
        xb_ref[...] = x_ref[...].astype(BF16)
        acc_ref[...] = jnp.zeros_like(acc_ref)

    @pl.when(active)
    def _():
        x = xb_ref[...]
        for j in range(wg_ref.shape[2] // FFN_CHUNK):
            cols = slice(j * FFN_CHUNK, (j + 1) * FFN_CHUNK)
            act = (_silu(_dot(x, wg_ref[0, :, cols])) * _dot(x, wu_ref[0, :, cols])).astype(BF16)
            acc_ref[...] += _dot(act, wd_ref[0, cols, :])

    @pl.when(last)
    def _():
        o_ref[...] = acc_ref[...]


def _moe_ffn(x_sorted, tile_expert, n_active, wg, wu, wd):
    n_rows, d = x_sorted.shape
    ffn = wg.shape[2]
    n_tiles = n_rows // MOE_TILE
    n_f = ffn // MOE_FCHUNK

    def row(i, f, te, na):
        return jnp.minimum(i, na[0] - 1)

    def fcol(i, f, te, na):
        return jnp.where(i < na[0], f, n_f - 1)

    grid_spec = pltpu.PrefetchScalarGridSpec(
        num_scalar_prefetch=2,
        grid=(n_tiles, n_f),
        in_specs=[pl.BlockSpec((MOE_TILE, d), lambda i, f, te, na: (row(i, f, te, na), 0)),
                  pl.BlockSpec((1, d, MOE_FCHUNK), lambda i, f, te, na: (te[i], 0, fcol(i, f, te, na))),
                  pl.BlockSpec((1, d, MOE_FCHUNK), lambda i, f, te, na: (te[i], 0, fcol(i, f, te, na))),
                  pl.BlockSpec((1, MOE_FCHUNK, d), lambda i, f, te, na: (te[i], fcol(i, f, te, na), 0))],
        out_specs=pl.BlockSpec((MOE_TILE, d), lambda i, f, te, na: (i, 0)),
        scratch_shapes=[pltpu.VMEM((MOE_TILE, d), BF16), pltpu.VMEM((MOE_TILE, d), F32)])
    return pl.pallas_call(
        _moe_kernel,
        grid_spec=grid_spec,
        out_shape=jax.ShapeDtypeStruct((n_rows, d), F32),
        compiler_params=_params(("arbitrary", "arbitrary")),
        name="expert_swiglu",
    )(tile_expert, n_active, x_sorted, wg, wu, wd)


def _final_kernel(pos_ref, next_pos_ref, x_ref, route_ref, mod_ref, g_ref, y_hbm, o_ref, buf_ref, sem):
    tile = x_ref.shape[0]
    i, n = pl.program_id(0), pl.num_programs(0)
    slot = i % 2

    def fetch(p_ref, s):
        def issue(r, carry):
            for k in range(2):
                pltpu.make_async_copy(y_hbm.at[pl.ds(p_ref[0, 0, k * tile + r], 1)],
                                      buf_ref.at[s, k, pl.ds(r, 1)], sem.at[s]).start()
            return carry
        lax.fori_loop(0, tile, issue, 0, unroll=8)

    @pl.when(i == 0)
    def _():
        fetch(pos_ref, slot)

    @pl.when(i + 1 < n)
    def _():
        fetch(next_pos_ref, 1 - slot)

    for k in range(2):
        pltpu.make_async_copy(y_hbm.at[pl.ds(0, tile)], buf_ref.at[slot, k], sem.at[slot]).wait()
    route = route_ref[...]
    y = route[:, 2:3] * buf_ref[slot, 0] + route[:, 3:4] * buf_ref[slot, 1]
    x = x_ref[...] + mod_ref[0, 0][5:6] * y
    o_ref[...] = x * lax.rsqrt(jnp.mean(x * x, axis=-1, keepdims=True) + RMS_EPS) * g_ref[...]


def _final_combine(x_lat, y_sorted, pos, route, mod, layer, g_final, dims):
    n_rows, d = x_lat.shape
    n_tiles = n_rows // TOK_TILE
    tiles_per_batch = dims["S"] // TOK_TILE
    return pl.pallas_call(
        _final_kernel,
        grid=(n_tiles,),
        in_specs=[pl.BlockSpec((1, 1, 2 * TOK_TILE), lambda i: (i, 0, 0), memory_space=pltpu.SMEM),
                  pl.BlockSpec((1, 1, 2 * TOK_TILE), lambda i: (jnp.minimum(i + 1, n_tiles - 1), 0, 0),
                               memory_space=pltpu.SMEM),
                  pl.BlockSpec((TOK_TILE, d), lambda i: (i, 0)),
                  pl.BlockSpec((TOK_TILE, LANES), lambda i: (i, 0)),
                  pl.BlockSpec((1, 1, 6, d), lambda i: (layer, i // tiles_per_batch, 0, 0)),
                  pl.BlockSpec((1, d), lambda i: (0, 0)),
                  pl.BlockSpec(memory_space=pl.ANY)],
        out_specs=pl.BlockSpec((TOK_TILE, d), lambda i: (i, 0)),
        out_shape=jax.ShapeDtypeStruct((n_rows, d), F32),
        scratch_shapes=[pltpu.VMEM((2, 2, TOK_TILE, d), F32), pltpu.SemaphoreType.DMA((2,))],
        compiler_params=_params(("arbitrary",)),
        name="expert_combine_final_norm",
    )(pos, pos, x_lat, route, mod, g_final, y_sorted)


def _rope_tables(s_len):
    t = np.arange(s_len)
    lane = np.arange(LANES) % HEAD_DIM
    quarter = HEAD_DIM // 4
    inv = 1.0 / (ROPE_BASE ** (jnp.arange(quarter, dtype=F32) / quarter))
    pos = np.where(lane[None, :] < HEAD_DIM // 2, (t // GRID_W)[:, None], (t % GRID_W)[:, None]).astype(np.float32)
    ang = jnp.asarray(pos) * inv[lane % quarter][None, :]
    first = jnp.asarray((lane % (HEAD_DIM // 2)) < quarter)[None, :]
    cos, sin = jnp.cos(ang), jnp.sin(ang)
    ident = jnp.zeros((TOK_TILE, LANES), F32)
    return (jnp.concatenate([cos, ident + 1.0]),
            jnp.concatenate([jnp.where(first, -sin, 0.0), ident]),
            jnp.concatenate([jnp.where(first, 0.0, sin), ident]))


def _rearranged_w_in(w):
    offs = np.concatenate([[0], np.cumsum(IN_SIZES)])
    part = lambda j: w[:, int(offs[j]):int(offs[j + 1])]
    pad = jnp.zeros((w.shape[0], LANES - 2 * GLA_LR), w.dtype)
    order = [0, 1, 2, 3, 6, 7, 9, 10, 4, 5]
    w_cat = jnp.concatenate([part(j) for j in order] + [pad], axis=1).astype(BF16)
    return w_cat, jnp.transpose(part(8)).astype(BF16), jnp.transpose(part(11)).astype(BF16)


def _decay_up(w_up, b):
    top = jnp.concatenate([w_up[0], jnp.zeros_like(w_up[0])], axis=1)
    bot = jnp.concatenate([jnp.zeros_like(w_up[1]), w_up[1]], axis=1)
    pad = jnp.zeros((LANES - 2 * GLA_LR, 2 * GLA_QK), w_up.dtype)
    return jnp.concatenate([top, bot, pad], axis=0).astype(BF16), jnp.concatenate([b[0], b[1]])[None, :]


def kernel(x, c, ctx, c_ctx, w_mod, b_mod, g_norm1, g_norm2, w_in, gla_w_dec_up, gla_b_dec, gla_g_norm,
           diff_lambda, diff_g_norm, na_rpb, w_out, w_ffn_gate, w_ffn_up, w_ffn_down, w_router, w_moe_gate,
           w_moe_up, w_moe_down, g_final):
    nb, s_len, d = x.shape
    c_len = ctx.shape[1]
    depth = w_mod.shape[0]
    assert c_len == ATT_TILE and s_len % TOK_TILE == 0 and (nb * c_len) % TOK_TILE == 0 and nb < MOD_ROWS
    n_lat = nb * s_len
    dims = {"B": nb, "S": s_len, "n_lat": n_lat, "n_tot": n_lat + nb * c_len}

    x_all = (x.reshape(n_lat, d), ctx.reshape(nb * c_len, d))
    cc = jnp.concatenate([c, c_ctx[None, :], jnp.zeros((MOD_ROWS - nb - 1, d), F32)], axis=0)
    mod = _modulation(cc, w_mod, b_mod).reshape(depth, MOD_ROWS, 6, d)
    rope_tabs = _rope_tables(s_len)

    for l in range(depth):
        last = l == depth - 1
        w_cat, w_dvt, w_nvt = _rearranged_w_in(w_in[l])
        wup2, bdec2 = _decay_up(gla_w_dec_up[l], gla_b_dec[l])
        gla, lg, dq, dk, dvt, nq, nk, nvt = _input_projection(
            x_all, mod, l, g_norm1[l][None, :], w_cat, w_dvt, w_nvt, wup2, bdec2, rope_tabs, dims)
        o_f, o_b = _gla(gla, lg, dims)
        b_diff = _diff_attention(dq, dk, dvt, diff_lambda[l], diff_g_norm[l][None, :], l, not last, dims)
        c_na = _neighbourhood_attention(nq, nk, nvt, _na_bias_tables(na_rpb[l], s_len // GRID_W), not last, dims)
        g_gla = jnp.tile(gla_g_norm[l], GLA_HEADS)[None, :]
        n_rows = n_lat if last else dims["n_tot"]
        if l % 2 == 0:
            i = l // 2
            x_mid, h = _output_projection(x_all, o_f, o_b, gla, b_diff, c_na, mod, l, g_gla, g_norm2[l][None, :],
                                          w_out[l].astype(BF16), None, n_rows, dims)
            assert not last, "the dense channel mixer is implemented for non-final layers only"
            x_all = _dense_ffn(h, x_mid, mod, l, w_ffn_gate[i].astype(BF16), w_ffn_up[i].astype(BF16),
                               w_ffn_down[i].astype(BF16), dims)
        else:
            i = l // 2
            w_r = jnp.concatenate([w_router[i], jnp.zeros((d, LANES - N_EXPERTS), F32)], axis=1)
            x_mid, h, route = _output_projection(x_all, o_f, o_b, gla, b_diff, c_na, mod, l, g_gla,
                                                 g_norm2[l][None, :], w_out[l].astype(BF16), w_r, n_rows, dims)
            assert last, "the expert layer is implemented for the final layer only"
            pos, tails, n_sorted, tile_expert, n_active = _moe_routing(route, n_rows)
            x_sorted = _moe_dispatch(h, pos, tails, n_sorted)
            y_sorted = _moe_ffn(x_sorted, tile_expert, n_active, w_moe_gate[i].astype(BF16),
                                w_moe_up[i].astype(BF16), w_moe_down[i].astype(BF16))
            return _final_combine(x_mid, y_sorted, pos, route, mod, l, g_final[None, :], dims).reshape(nb, s_len, d)
    raise NotImplementedError("the final layer must be an expert layer")
```

```python
import functools
import math

import numpy as np
import jax
import jax.numpy as jnp
from jax import lax
from jax.experimental import pallas as pl
from jax.experimental.pallas import tpu as pltpu

F32 = jnp.float32
BF16 = jnp.bfloat16
HIGHEST = lax.Precision.HIGHEST

GRID_W = 64
HEAD_DIM = 64
ROPE_BASE = 10000.0
RMS_EPS = 1e-6
GLA_HEADS = 4
GLA_DK = 32
GLA_DV = 64
GLA_LR = 16
GLA_NORMALIZER = 16.0
GLA_CHUNK = 64
DIFF_HEADS = 4
DIFF_HD = 64
NA_HEADS = 4
NA_HD = 64
WIN_R = 8
WIN_C = 16
N_EXPERTS = 8
GLA_QK = GLA_HEADS * GLA_DK
GLA_V = GLA_HEADS * GLA_DV
DIFF_W = DIFF_HEADS * 2 * DIFF_HD
NA_W = NA_HEADS * NA_HD
IN_SIZES = (GLA_QK, GLA_QK, GLA_V, GLA_V, GLA_LR, GLA_LR, DIFF_W, DIFF_W, DIFF_W, NA_W, NA_W, NA_W)

LANES = 128
TOK_TILE = 512
ATT_TILE = 256
DIFF_QT = 2048
DIFF_GW = 256
NA_QROWS = 4
NA_KROWS = 12
FFN_CHUNK = 256
MOE_TILE = 512
MOE_FCHUNK = 3584
MOD_ROWS = 16
NEG_BIG = -1e30
LOG2E = math.log2(math.e)
VMEM_LIMIT = 56 * 1024 * 1024

C_GLA = 0
C_DQ = C_GLA + 2 * GLA_QK + 2 * GLA_V
C_DK = C_DQ + DIFF_W
C_NQ = C_DK + DIFF_W
C_NK = C_NQ + NA_W
C_DEC = C_NK + NA_W
C_END = C_DEC + LANES


def _silu(x):
    return x * (1.0 / (1.0 + jnp.exp(-x)))


def _dot(a, b):
    return jnp.dot(a, b, preferred_element_type=F32)


def _dot_nt(a, b):
    return lax.dot_general(a, b, (((1,), (1,)), ((), ())), preferred_element_type=F32)


def _dot_tn(a, b):
    return lax.dot_general(a, b, (((0,), (0,)), ((), ())), preferred_element_type=F32)


def _params(sem, vmem=VMEM_LIMIT):
    return pltpu.CompilerParams(dimension_semantics=sem, vmem_limit_bytes=vmem)


def _mod_kernel(cc_ref, w_ref, b_ref, o_ref):
    s = _silu(cc_ref[...]).astype(BF16)
    o_ref[0] = _dot(s, w_ref[0].astype(BF16)) + b_ref[0]


def _modulation(cc, w_mod, b_mod):
    depth, d, six_d = w_mod.shape
    n = six_d // d
    return pl.pallas_call(
        _mod_kernel,
        grid=(depth, n),
        in_specs=[pl.BlockSpec((MOD_ROWS, d), lambda l, j: (0, 0)),
                  pl.BlockSpec((1, d, d), lambda l, j: (l, 0, j)),
                  pl.BlockSpec((1, 1, d), lambda l, j: (l, 0, j))],
        out_specs=pl.BlockSpec((1, MOD_ROWS, d), lambda l, j: (l, 0, j)),
        out_shape=jax.ShapeDtypeStruct((depth, MOD_ROWS, six_d), F32),
        compiler_params=_params(("parallel", "parallel")),
        name="modulation",
    )(cc, w_mod, b_mod.reshape(depth, 1, six_d))


def _rope(z, a, bt, ct):
    outs = []
    for s in range(z.shape[1] // LANES):
        zs = z[:, s * LANES:(s + 1) * LANES]
        outs.append(zs * a + pltpu.roll(zs, LANES - 16, 1) * bt + pltpu.roll(zs, 16, 1) * ct)
    return jnp.concatenate(outs, axis=1)


def _split_rows(x, n_lat_tiles, width):
    if not isinstance(x, tuple):
        return [x], [pl.BlockSpec((TOK_TILE, width), lambda i: (i, 0))]
    return list(x), [pl.BlockSpec((TOK_TILE, width), lambda i: (jnp.minimum(i, n_lat_tiles - 1), 0)),
                     pl.BlockSpec((TOK_TILE, width), lambda i: (jnp.maximum(i - n_lat_tiles, 0), 0))]


def _joined_rows(n_lat_tiles, refs):
    if len(refs) == 1:
        return refs[0][...]
    return jnp.where(pl.program_id(0) < n_lat_tiles, refs[0][...], refs[1][...])


def _inproj_kernel(n_lat_tiles, n_x, *refs):
    (mod_ref, g_ref, w_ref, wvt_ref, wnt_ref, wup_ref, bdec_ref, ra_ref, rb_ref, rc_ref,
     gla_ref, lg_ref, dq_ref, dk_ref, dvt_ref, nq_ref, nk_ref, nvt_ref) = refs[n_x:]
    x = _joined_rows(n_lat_tiles, refs[:n_x])
    m = mod_ref[0, 0]
    h = x * lax.rsqrt(jnp.mean(x * x, axis=-1, keepdims=True) + RMS_EPS) * g_ref[...]
    hb = (h * (1.0 + m[1:2]) + m[0:1]).astype(BF16)

    def proj(c0, c1):
        return _dot(hb, w_ref[:, c0:c1])

    gla_ref[...] = proj(C_GLA, C_DQ).astype(BF16)
    a, bt, ct = ra_ref[...], rb_ref[...], rc_ref[...]
    dq_ref[...] = (_rope(proj(C_DQ, C_DK), a, bt, ct) * (DIFF_HD ** -0.5 * LOG2E)).astype(BF16)
    dk_ref[...] = _rope(proj(C_DK, C_NQ), a, bt, ct).astype(BF16)
    dvt_ref[0] = _dot_nt(wvt_ref[...], hb).astype(BF16)
    nq_ref[...] = (proj(C_NQ, C_NK) * (NA_HD ** -0.5 * LOG2E)).astype(BF16)
    nk_ref[...] = proj(C_NK, C_DEC).astype(BF16)
    for j in range(nvt_ref.shape[0]):
        nvt_ref[j] = _dot_nt(wnt_ref[...], hb[j * ATT_TILE:(j + 1) * ATT_TILE]).astype(BF16)
    logits = _dot(proj(C_DEC, C_END).astype(BF16), wup_ref[...]) + bdec_ref[...]
    log_sig = jnp.minimum(logits, 0.0) - jnp.log1p(jnp.exp(-jnp.abs(logits)))
    lg_ref[...] = log_sig * (1.0 / GLA_NORMALIZER)


def _input_projection(x_all, mod, layer, g1, w_cat, w_dvt, w_nvt, wup2, bdec2, rope_tabs, dims):
    n_tot, d = dims["n_tot"], g1.shape[1]
    n_lat_tiles = dims["n_lat"] // TOK_TILE
    x_args, x_specs = _split_rows(x_all, n_lat_tiles, d)
    tiles_per_batch = dims["S"] // TOK_TILE
    n_tiles = n_tot // TOK_TILE
    nb = dims["B"]

    def mod_map(i):
        return (layer, jnp.where(i < n_lat_tiles, i // tiles_per_batch, nb), 0, 0)

    def rope_map(i):
        return (jnp.where(i < n_lat_tiles, i % tiles_per_batch, tiles_per_batch), 0)

    row = lambda w: pl.BlockSpec((TOK_TILE, w), lambda i: (i, 0))
    const = lambda shape: pl.BlockSpec(shape, lambda i: (0,) * len(shape))
    tab = pl.BlockSpec((TOK_TILE, LANES), rope_map)
    sub = TOK_TILE // ATT_TILE
    out_specs = [row(C_DQ), row(2 * GLA_QK), row(DIFF_W), row(DIFF_W),
                 pl.BlockSpec((1, DIFF_W, TOK_TILE), lambda i: (i, 0, 0)), row(NA_W), row(NA_W),
                 pl.BlockSpec((sub, NA_W, ATT_TILE), lambda i: (i, 0, 0))]
    shape = lambda w, dt=BF16: jax.ShapeDtypeStruct((n_tot, w), dt)
    out_shape = [shape(C_DQ), shape(2 * GLA_QK, F32), shape(DIFF_W), shape(DIFF_W),
                 jax.ShapeDtypeStruct((n_tiles, DIFF_W, TOK_TILE), BF16), shape(NA_W), shape(NA_W),
                 jax.ShapeDtypeStruct((n_tiles * sub, NA_W, ATT_TILE), BF16)]
    return pl.pallas_call(
        functools.partial(_inproj_kernel, n_lat_tiles, len(x_args)),
        grid=(n_tiles,),
        in_specs=x_specs + [pl.BlockSpec((1, 1, 6, d), mod_map), const((1, d)), const((d, C_END)),
                            const((DIFF_W, d)), const((NA_W, d)), const((LANES, 2 * GLA_QK)),
                            const((1, 2 * GLA_QK)), tab, tab, tab],
        out_specs=out_specs,
        out_shape=out_shape,
        compiler_params=_params(("parallel",)),
        name="input_projection",
    )(*x_args, mod, g1, w_cat, w_dvt, w_nvt, wup2, bdec2, *rope_tabs)


def _gla_block_local(g_ref, l_ref, fwd, consts):
    tri, same, tri4, head_k, head_v, bd = consts
    t = g_ref.shape[0]
    q = g_ref[:, 0:GLA_QK].astype(F32) * GLA_DK ** -0.5
    k = g_ref[:, GLA_QK:2 * GLA_QK].astype(F32)
    v = g_ref[:, 2 * GLA_QK:2 * GLA_QK + GLA_V]
    lcol = 0 if fwd else GLA_QK
    lg = l_ref[:, lcol:lcol + GLA_QK]
    lg_hi = lg.astype(BF16)
    lg_lo = (lg - lg_hi.astype(F32)).astype(BF16)
    b = _dot(tri, lg_hi) + _dot(tri, lg_lo)
    total = _dot(same, lg_hi) + _dot(same, lg_lo)
    q_dec = q * jnp.exp(b)
    k_inv = (k * jnp.exp(-b)).astype(BF16)
    k_end = (k * jnp.exp(total - b)).astype(BF16)
    qs = jnp.concatenate([jnp.where(head_k == h, q_dec, 0.0) for h in range(GLA_HEADS)], axis=0)
    att = jnp.where(tri4, _dot_nt(qs.astype(BF16), k_inv), 0.0)
    ov = _dot(att.astype(BF16), v)
    o = jnp.where(head_v == 0, ov[0:t], 0.0)
    for h in range(1, GLA_HEADS):
        o = o + jnp.where(head_v == h, ov[h * t:(h + 1) * t], 0.0)
    q_dec = q_dec.astype(BF16)
    chunks = []
    for c in range(t // GLA_CHUNK):
        rows = slice(c * GLA_CHUNK, (c + 1) * GLA_CHUNK)
        chunks.append((o[rows], q_dec[rows], jnp.where(bd, _dot_tn(v[rows], k_end[rows]), 0.0),
                       jnp.exp(total[c * GLA_CHUNK:c * GLA_CHUNK + 1])))
    return chunks


def _gla_kernel(gf_ref, lf_ref, gb_ref, lb_ref, of_ref, ob_ref, sf_ref, sb_ref):
    @pl.when(pl.program_id(1) == 0)
    def _():
        sf_ref[...] = jnp.zeros_like(sf_ref)
        sb_ref[...] = jnp.zeros_like(sb_ref)

    t = gf_ref.shape[0]
    r = lax.broadcasted_iota(jnp.int32, (t, t), 0)
    c = lax.broadcasted_iota(jnp.int32, (t, t), 1)
    r4 = lax.broadcasted_iota(jnp.int32, (GLA_HEADS * t, t), 0) % t
    c4 = lax.broadcasted_iota(jnp.int32, (GLA_HEADS * t, t), 1)
    same = r // GLA_CHUNK == c // GLA_CHUNK
    same4 = r4 // GLA_CHUNK == c4 // GLA_CHUNK
    head_k = lax.broadcasted_iota(jnp.int32, (1, GLA_QK), 1) // GLA_DK
    head_v = lax.broadcasted_iota(jnp.int32, (1, GLA_V), 1) // GLA_DV
    bd = (lax.broadcasted_iota(jnp.int32, (GLA_V, GLA_QK), 0) // GLA_DV
          == lax.broadcasted_iota(jnp.int32, (GLA_V, GLA_QK), 1) // GLA_DK)
    same_b = same.astype(BF16)
    lower = (jnp.logical_and(same, c <= r).astype(BF16), same_b, jnp.logical_and(same4, c4 <= r4), head_k, head_v, bd)
    upper = (jnp.logical_and(same, c >= r).astype(BF16), same_b, jnp.logical_and(same4, c4 >= r4), head_k, head_v, bd)
    n_chunks = t // GLA_CHUNK
    scans = ((gf_ref, lf_ref, of_ref, sf_ref, True, lower, list(range(n_chunks))),
             (gb_ref, lb_ref, ob_ref, sb_ref, False, upper, list(range(n_chunks - 1, -1, -1))))
    local = [_gla_block_local(g_ref, l_ref, fwd, consts) for g_ref, l_ref, _, _, fwd, consts, _ in scans]
    for (_, _, o_ref, s_ref, _, _, order), chunks in zip(scans, local):
        s = s_ref[...]
        for cidx, (o_intra, q_dec, inc, decay) in ((cidx, chunks[cidx]) for cidx in order):
            o = o_intra + _dot_nt(q_dec, s.astype(BF16))
            o_ref[cidx * GLA_CHUNK:(cidx + 1) * GLA_CHUNK, :] = o.astype(o_ref.dtype)
            s = decay * s + inc
        s_ref[...] = s


def _gla(gla, lg, dims):
    n_tot = gla.shape[0]
    nb, steps = dims["B"], 1 + dims["S"] // ATT_TILE
    lat_blocks = dims["n_lat"] // ATT_TILE
    per_batch = dims["S"] // ATT_TILE

    def fwd_map(b, i):
        return (jnp.where(i == 0, lat_blocks + b, b * per_batch + i - 1), 0)

    def bwd_map(b, i):
        return (jnp.where(i == 0, lat_blocks + b, b * per_batch + per_batch - i), 0)

    return pl.pallas_call(
        _gla_kernel,
        grid=(nb, steps),
        in_specs=[pl.BlockSpec((ATT_TILE, C_DQ), fwd_map), pl.BlockSpec((ATT_TILE, 2 * GLA_QK), fwd_map),
                  pl.BlockSpec((ATT_TILE, C_DQ), bwd_map), pl.BlockSpec((ATT_TILE, 2 * GLA_QK), bwd_map)],
        out_specs=[pl.BlockSpec((ATT_TILE, GLA_V), fwd_map), pl.BlockSpec((ATT_TILE, GLA_V), bwd_map)],
        out_shape=[jax.ShapeDtypeStruct((n_tot, GLA_V), BF16)] * 2,
        scratch_shapes=[pltpu.VMEM((GLA_V, GLA_QK), F32), pltpu.VMEM((GLA_V, GLA_QK), F32)],
        compiler_params=_params(("parallel", "arbitrary")),
        name="gla_scan",
    )(gla, lg, gla, lg)


def _diff_kernel(lam_init, latent, *refs):
    if latent:
        q_ref, kl_ref, vl_ref, kc_ref, vc_ref, lam_ref, g_ref, o_ref, *s_refs = refs
    else:
        q_ref, kc_ref, vc_ref, lam_ref, g_ref, o_ref, *s_refs = refs
    q = q_ref[...]
    t, hw = q.shape
    c_len = kc_ref.shape[0]
    lane = lax.broadcasted_iota(jnp.int32, q.shape, 1)
    zero = jnp.zeros_like(q)
    qm = (jnp.where(lane < DIFF_HD, q, zero), jnp.where(lane >= DIFF_HD, q, zero))

    tiles = [(lambda: kc_ref[...], lambda: vc_ref[0], 0, c_len)]
    if latent:
        tiles += [(lambda j=j: kl_ref[j * TOK_TILE:(j + 1) * TOK_TILE, :], lambda j=j: vl_ref[j],
                   c_len + j * TOK_TILE, TOK_TILE) for j in range(vl_ref.shape[0])]

    gw = min(t, DIFF_GW)
    groups = [(mp, c0) for mp in range(2) for c0 in range(0, t, gw)]

    def pass1(g, tile, m):
        mp, c0 = groups[g]
        load_k, _, off, n = tile
        s = _dot_nt(load_k(), qm[mp][c0:c0 + gw])
        s_refs[g % 2][off:off + n, :] = s
        return jnp.maximum(m, jnp.max(s, axis=0, keepdims=True))

    def pass2(g, tile, m, l, acc):
        _, load_vt, off, n = tile
        p = jnp.exp2(s_refs[g % 2][off:off + n, :] - m)
        return l + jnp.sum(p, axis=0, keepdims=True), acc + _dot(load_vt(), p.astype(BF16))

    m_prev, outs = None, []
    for g in range(len(groups) + 1):
        m = jnp.full((1, gw), -jnp.inf, F32)
        l, acc = jnp.zeros((1, gw), F32), jnp.zeros((hw, gw), F32)
        for tile in tiles:
            if g < len(groups):
                m = pass1(g, tile, m)
            if g > 0:
                l, acc = pass2(g - 1, tile, m_prev, l, acc)
        if g > 0:
            outs.append(acc / l)
        m_prev = m
    n_half = len(groups) // 2
    on = [jnp.concatenate(outs[i * n_half:(i + 1) * n_half], axis=1) for i in range(2)]

    lam = lam_ref[...]
    lam_full = (jnp.exp(jnp.sum(lam[0:1] * lam[1:2], axis=-1, keepdims=True))
                - jnp.exp(jnp.sum(lam[2:3] * lam[3:4], axis=-1, keepdims=True)) + lam_init)
    o = jnp.transpose(on[0] - lam_full * on[1])
    o = o * lax.rsqrt(jnp.mean(o * o, axis=-1, keepdims=True) + RMS_EPS) * g_ref[...]
    o_ref[...] = (o * (1.0 - lam_init)).astype(BF16)


def _diff_attention(dq, dk, dvt, lam, g_diff, layer, with_ctx, dims):
    n_tot = dq.shape[0]
    nb, s_len = dims["B"], dims["S"]
    qt = min(DIFF_QT, s_len)
    n_lat_q = s_len // qt
    lat_blocks = dims["n_lat"] // ATT_TILE
    lat_tiles = dims["n_lat"] // TOK_TILE
    ctx_per_tile = TOK_TILE // ATT_TILE
    lam_init = 0.8 - 0.6 * math.exp(-0.3 * layer)
    hw = 2 * DIFF_HD
    out_shape = jax.ShapeDtypeStruct((dims["n_lat"], DIFF_W), BF16)

    k_ctx = pl.BlockSpec((ATT_TILE, hw), lambda b, h, *_: (lat_blocks + b, h))
    v_ctx = pl.BlockSpec((1, hw, ATT_TILE), lambda b, h, *_: (lat_tiles + b // ctx_per_tile, h, b % ctx_per_tile))
    small = [pl.BlockSpec((4, DIFF_HD), lambda *_: (0, 0)), pl.BlockSpec((1, hw), lambda *_: (0, 0))]
    q_lat = pl.BlockSpec((qt, hw), lambda b, h, i: (b * n_lat_q + i, h))
    out = pl.pallas_call(
        functools.partial(_diff_kernel, lam_init, True),
        grid=(nb, DIFF_HEADS, n_lat_q),
        in_specs=[q_lat, pl.BlockSpec((s_len, hw), lambda b, h, i: (b, h)),
                  pl.BlockSpec((s_len // TOK_TILE, hw, TOK_TILE), lambda b, h, i: (b, h, 0)), k_ctx, v_ctx] + small,
        out_specs=q_lat,
        out_shape=out_shape,
        scratch_shapes=[pltpu.VMEM((ATT_TILE + s_len, DIFF_GW), F32)] * 2,
        compiler_params=_params(("parallel", "parallel", "arbitrary")),
        name="diff_attention",
    )(dq, dk, dvt, dk, dvt, lam, g_diff)
    if not with_ctx:
        return out, None
    out_ctx = pl.pallas_call(
        functools.partial(_diff_kernel, lam_init, False),
        grid=(nb, DIFF_HEADS),
        in_specs=[pl.BlockSpec((ATT_TILE, hw), lambda b, h: (lat_blocks + b, h)), k_ctx, v_ctx] + small,
        out_specs=pl.BlockSpec((ATT_TILE, hw), lambda b, h: (b, h)),
        out_shape=jax.ShapeDtypeStruct((nb * ATT_TILE, DIFF_W), BF16),
        scratch_shapes=[pltpu.VMEM((ATT_TILE, ATT_TILE), F32)] * 2,
        compiler_params=_params(("parallel", "parallel")),
        name="diff_attention_context",
    )(dq, dk, dvt, lam, g_diff)
    return out, out_ctx


def _na_bias_tables(rpb, rows):
    wr = min(WIN_R, rows)
    n_groups = rows // NA_QROWS
    qc = np.arange(GRID_W)[None, :]
    kc = np.arange(GRID_W)[:, None]
    cs = np.clip(qc - WIN_C // 2, 0, GRID_W - WIN_C)
    col_valid = (kc >= cs) & (kc < cs + WIN_C)
    col_sel = ((kc - qc + WIN_C - 1)[None] == np.arange(2 * WIN_C - 1)[:, None, None]) & col_valid[None]
    row_sel = []
    for g in (0, 1, n_groups - 1):
        r0 = g * NA_QROWS
        u0 = int(np.clip(r0 - WIN_R // 2, 0, rows - NA_KROWS))
        r = r0 + np.arange(NA_QROWS)[None, :]
        kr = u0 + np.arange(NA_KROWS)[:, None]
        rs = np.clip(r - WIN_R // 2, 0, rows - wr)
        row_valid = (kr >= rs) & (kr < rs + wr)
        row_sel.append(((kr - r + WIN_R - 1)[None] == np.arange(2 * WIN_R - 1)[:, None, None]) & row_valid[None])
    row_sel = np.stack(row_sel)
    t = jnp.einsum("hrc,vrkd,cxq->vkxhdq", rpb.astype(F32) * LOG2E, jnp.asarray(row_sel, F32),
                   jnp.asarray(col_sel, F32), precision=HIGHEST)
    valid = row_sel.any(axis=1)[:, :, None, None, :, None] & col_valid[None, None, :, None, None, :]
    t = jnp.where(valid, t, NEG_BIG)
    return t.reshape(3, NA_KROWS // NA_QROWS, ATT_TILE, NA_HEADS * ATT_TILE)


def _na_kernel(rows, latent, *refs):
    if latent:
        q_ref, kl_ref, vl_ref, kc_ref, vc_ref, bias_ref, o_ref, *s_refs = refs
    else:
        q_ref, kc_ref, vc_ref, o_ref, *s_refs = refs
    q = q_ref[...]
    t = q.shape[0]
    head = lax.broadcasted_iota(jnp.int32, q.shape, 1) // NA_HD
    zero = jnp.zeros_like(q)
    qs = jnp.concatenate([jnp.where(head == h, q, zero) for h in range(NA_HEADS)], axis=0)

    n_local = NA_KROWS // NA_QROWS
    s = _dot_nt(kc_ref[...], qs)
    s_refs[0][...] = s
    m = jnp.max(s, axis=0, keepdims=True)
    values = [lambda: vc_ref[0]]
    if latent:
        t0 = jnp.clip(pl.program_id(1) - 1, 0, rows // NA_QROWS - n_local)
        k_local = kl_ref[pl.ds(pl.multiple_of(t0 * t, t), n_local * t), :]
        s = _dot_nt(k_local, qs) + bias_ref[0].reshape(n_local * t, NA_HEADS * t)
        for j in range(n_local):
            s_refs[1 + j][...] = s[j * t:(j + 1) * t]
            values.append(lambda j=j: vl_ref[t0 + j])
        m = jnp.maximum(m, jnp.max(s, axis=0, keepdims=True))
    l = jnp.zeros((1, NA_HEADS * t), F32)
    acc = [jnp.zeros((NA_HD, t), F32)] * NA_HEADS
    for i, load_vt in enumerate(values):
        p = jnp.exp2(s_refs[i][...] - m)
        l = l + jnp.sum(p, axis=0, keepdims=True)
        pb, vt = p.astype(BF16), load_vt()
        for h in range(NA_HEADS):
            acc[h] = acc[h] + _dot(vt[h * NA_HD:(h + 1) * NA_HD, :], pb[:, h * t:(h + 1) * t])
    o = jnp.concatenate([acc[h] / l[:, h * t:(h + 1) * t] for h in range(NA_HEADS)], axis=0)
    o_ref[...] = jnp.transpose(o).astype(BF16)


def _neighbourhood_attention(nq, nk, nvt, bias_tabs, with_ctx, dims):
    nb, s_len = dims["B"], dims["S"]
    rows = s_len // GRID_W
    n_groups = rows // NA_QROWS
    lat_blocks = dims["n_lat"] // ATT_TILE
    n_key_tiles = 1 + NA_KROWS // NA_QROWS

    def bias_map(b, g):
        return (jnp.where(g == 0, 0, jnp.where(g >= n_groups - 1, 2, 1)), 0, 0, 0)

    k_ctx = pl.BlockSpec((ATT_TILE, NA_W), lambda b, *_: (lat_blocks + b, 0))
    v_ctx = pl.BlockSpec((1, NA_W, ATT_TILE), lambda b, *_: (lat_blocks + b, 0, 0))
    q_lat = pl.BlockSpec((ATT_TILE, NA_W), lambda b, g: (b * n_groups + g, 0))
    scratch = lambda n: [pltpu.VMEM((ATT_TILE, NA_HEADS * ATT_TILE), F32)] * n
    lat = pl.pallas_call(
        functools.partial(_na_kernel, rows, True),
        grid=(nb, n_groups),
        in_specs=[q_lat, pl.BlockSpec((s_len, NA_W), lambda b, g: (b, 0)),
                  pl.BlockSpec((s_len // ATT_TILE, NA_W, ATT_TILE), lambda b, g: (b, 0, 0)), k_ctx, v_ctx,
                  pl.BlockSpec((1,) + bias_tabs.shape[1:], bias_map)],
        out_specs=q_lat,
        out_shape=jax.ShapeDtypeStruct((dims["n_lat"], NA_W), BF16),
        scratch_shapes=scratch(n_key_tiles),
        compiler_params=_params(("parallel", "arbitrary")),
        name="neighbourhood_attention",
    )(nq, nk, nvt, nk, nvt, bias_tabs)
    if not with_ctx:
        return lat, None
    ctx = pl.pallas_call(
        functools.partial(_na_kernel, rows, False),
        grid=(nb,),
        in_specs=[pl.BlockSpec((ATT_TILE, NA_W), lambda b: (lat_blocks + b, 0)), k_ctx, v_ctx],
        out_specs=pl.BlockSpec((ATT_TILE, NA_W), lambda b: (b, 0)),
        out_shape=jax.ShapeDtypeStruct((nb * ATT_TILE, NA_W), BF16),
        scratch_shapes=scratch(1),
        compiler_params=_params(("parallel",)),
        name="context_attention",
    )(nq, nk, nvt)
    return lat, ctx


def _route_top2(logits):
    lane = lax.broadcasted_iota(jnp.int32, logits.shape, 1)
    lane_f = lane.astype(F32)
    lg = jnp.where(lane < N_EXPERTS, logits, -jnp.inf)
    m1 = jnp.max(lg, axis=-1, keepdims=True)
    i1 = jnp.min(jnp.where(lg == m1, lane_f, float(LANES)), axis=-1, keepdims=True)
    lg2 = jnp.where(lane_f == i1, -jnp.inf, lg)
    m2 = jnp.max(lg2, axis=-1, keepdims=True)
    i2 = jnp.min(jnp.where(lg2 == m2, lane_f, float(LANES)), axis=-1, keepdims=True)
    e = jnp.exp(m2 - m1)
    w1 = 1.0 / (1.0 + e)
    w2 = e / (1.0 + e)
    return jnp.where(lane == 0, i1, jnp.where(lane == 1, i2, jnp.where(lane == 2, w1,
                                                                         jnp.where(lane == 3, w2, 0.0))))


def _outproj_kernel(moe, n_lat_tiles, n_x, *refs):
    x_in = _joined_rows(n_lat_tiles, refs[:n_x])
    of_ref, ob_ref, gate_ref, bd_ref, cn_ref, *rest = refs[n_x:]
    if moe:
        mod_ref, ggla_ref, g2_ref, w_ref, wr_ref, x_out_ref, h_ref, route_ref = rest
        b_diff, c_na = bd_ref[...], cn_ref[...]
    else:
        bdc_ref, cnc_ref, mod_ref, ggla_ref, g2_ref, w_ref, x_out_ref, h_ref = rest
        is_lat = pl.program_id(0) < n_lat_tiles
        b_diff = jnp.where(is_lat, bd_ref[...], bdc_ref[...])
        c_na = jnp.where(is_lat, cn_ref[...], cnc_ref[...])
    m = mod_ref[0, 0]
    o = of_ref[...].astype(F32) + ob_ref[...].astype(F32)
    avg = (lax.broadcasted_iota(jnp.int32, (GLA_V, GLA_V), 0) // GLA_DV
           == lax.broadcasted_iota(jnp.int32, (GLA_V, GLA_V), 1) // GLA_DV).astype(F32) * (1.0 / GLA_DV)
    ms = jnp.dot(o * o, avg, precision=HIGHEST, preferred_element_type=F32)
    a = o * lax.rsqrt(ms + RMS_EPS) * ggla_ref[...] * _silu(gate_ref[...].astype(F32))
    y = (_dot(a.astype(BF16), w_ref[0:GLA_V]) + _dot(b_diff, w_ref[GLA_V:GLA_V + DIFF_W])
         + _dot(c_na, w_ref[GLA_V + DIFF_W:]))
    x = x_in + m[2:3] * y
    x_out_ref[...] = x
    h = x * lax.rsqrt(jnp.mean(x * x, axis=-1, keepdims=True) + RMS_EPS) * g2_ref[...]
    h = h * (1.0 + m[4:5]) + m[3:4]
    h_ref[...] = h.astype(h_ref.dtype)
    if moe:
        h_hi, w = h.astype(BF16), wr_ref[...]
        h_lo = (h - h_hi.astype(F32)).astype(BF16)
        w_hi = w.astype(BF16)
        w_lo = (w - w_hi.astype(F32)).astype(BF16)
        route_ref[...] = _route_top2(_dot(h_hi, w_hi) + (_dot(h_hi, w_lo) + _dot(h_lo, w_hi)))


def _output_projection(x_all, o_f, o_b, gla, b_diff, c_na, mod, layer, g_gla, g2, w_out, w_router, n_rows, dims):
    d = w_out.shape[1]
    n_lat_tiles = dims["n_lat"] // TOK_TILE
    tiles_per_batch = dims["S"] // TOK_TILE
    nb = dims["B"]
    moe = w_router is not None
    assert moe == (b_diff[1] is None) and n_rows == (dims["n_lat"] if moe else dims["n_tot"])
    x_args, x_specs = _split_rows(x_all, n_lat_tiles, d)

    def mod_map(i):
        return (layer, jnp.where(i < n_lat_tiles, i // tiles_per_batch, nb), 0, 0)

    row = lambda w, cb=0: pl.BlockSpec((TOK_TILE, w), lambda i: (i, cb))
    lat_row = lambda w: pl.BlockSpec((TOK_TILE, w), lambda i: (jnp.minimum(i, n_lat_tiles - 1), 0))
    ctx_row = lambda w: pl.BlockSpec((TOK_TILE, w), lambda i: (jnp.maximum(i - n_lat_tiles, 0), 0))
    const = lambda shape: pl.BlockSpec(shape, lambda i: (0,) * len(shape))
    in_specs = x_specs + [row(GLA_V), row(GLA_V), row(GLA_V, (2 * GLA_QK + GLA_V) // GLA_V),
                          lat_row(DIFF_W), lat_row(NA_W)]
    args = x_args + [o_f, o_b, gla, b_diff[0], c_na[0]]
    if not moe:
        in_specs += [ctx_row(DIFF_W), ctx_row(NA_W)]
        args += [b_diff[1], c_na[1]]
    in_specs += [pl.BlockSpec((1, 1, 6, d), mod_map), const((1, GLA_V)), const((1, d)), const((d, d))]
    args += [mod, g_gla, g2, w_out]
    out_specs = [row(d), row(d)]
    out_shape = [jax.ShapeDtypeStruct((n_rows, d), F32), jax.ShapeDtypeStruct((n_rows, d), F32 if moe else BF16)]
    if moe:
        in_specs.append(const((d, LANES)))
        args.append(w_router)
        out_specs.append(row(LANES))
        out_shape.append(jax.ShapeDtypeStruct((n_rows, LANES), F32))
    return pl.pallas_call(
        functools.partial(_outproj_kernel, moe, n_lat_tiles, len(x_args)),
        grid=(n_rows // TOK_TILE,),
        in_specs=in_specs, out_specs=out_specs, out_shape=out_shape,
        compiler_params=_params(("parallel",)),
        name="output_projection",
    )(*args)


def _ffn_kernel(h_ref, x_ref, mod_ref, wg_ref, wu_ref, wd_ref, o_ref, acc_ref):
    h = h_ref[...]
    n_chunks = wg_ref.shape[1] // FFN_CHUNK
    for j in range(n_chunks):
        cols = slice(j * FFN_CHUNK, (j + 1) * FFN_CHUNK)
        act = (_silu(_dot(h, wg_ref[:, cols])) * _dot(h, wu_ref[:, cols])).astype(BF16)
        part = _dot(act, wd_ref[cols, :])
        if j == 0:
            acc_ref[...] = part
        else:
            acc_ref[...] += part
    o_ref[...] = x_ref[...] + mod_ref[0, 0][5:6] * acc_ref[...]


def _dense_ffn(h, x_all, mod, layer, wg, wu, wd, dims):
    n_rows, d = x_all.shape
    f = wg.shape[1]
    n_lat_tiles = dims["n_lat"] // TOK_TILE
    tiles_per_batch = dims["S"] // TOK_TILE
    nb = dims["B"]

    def mod_map(i):
        return (layer, jnp.where(i < n_lat_tiles, i // tiles_per_batch, nb), 0, 0)

    row = lambda: pl.BlockSpec((TOK_TILE, d), lambda i: (i, 0))
    const = lambda shape: pl.BlockSpec(shape, lambda i: (0, 0))
    return pl.pallas_call(
        _ffn_kernel,
        grid=(n_rows // TOK_TILE,),
        in_specs=[row(), row(), pl.BlockSpec((1, 1, 6, d), mod_map), const((d, f)), const((d, f)), const((f, d))],
        out_specs=row(),
        out_shape=jax.ShapeDtypeStruct((n_rows, d), F32),
        scratch_shapes=[pltpu.VMEM((TOK_TILE, d), F32)],
        compiler_params=_params(("parallel",)),
        name="dense_swiglu",
    )(h, x_all, mod, wg, wu, wd)


def _moe_routing(route, n_tok):
    e_flat = jnp.concatenate([route[:, 0], route[:, 1]]).astype(jnp.int32)
    onehot = (e_flat[:, None] == jnp.arange(N_EXPERTS, dtype=jnp.int32)[None, :]).astype(jnp.int32)
    csum = jnp.cumsum(onehot, axis=0)
    counts = csum[-1]
    padded = ((counts + MOE_TILE - 1) // MOE_TILE) * MOE_TILE
    ends = jnp.cumsum(padded)
    pos = jnp.sum(onehot * (csum - 1 + (ends - padded)[None, :]), axis=1)
    n_rows = 2 * n_tok + N_EXPERTS * MOE_TILE
    tile_start = jnp.arange(n_rows // MOE_TILE, dtype=jnp.int32) * MOE_TILE
    tile_expert = jnp.minimum(jnp.sum((tile_start[:, None] >= ends[None, :]).astype(jnp.int32), axis=1),
                              N_EXPERTS - 1).astype(jnp.int32)
    n_active = (ends[-1:] // MOE_TILE).astype(jnp.int32)
    unused = ends[-1] + jnp.arange(N_EXPERTS, dtype=jnp.int32) * MOE_TILE
    tails = jnp.concatenate([jnp.where(padded > 0, ends - MOE_TILE, -1),
                             jnp.where(unused < n_rows, unused, -1)]).astype(jnp.int32)
    n_tiles = n_tok // TOK_TILE
    pos = jnp.concatenate([pos[:n_tok].reshape(n_tiles, 1, TOK_TILE), pos[n_tok:].reshape(n_tiles, 1, TOK_TILE)],
                          axis=2)
    return pos, tails, n_rows, tile_expert, n_active


def _dispatch_kernel(tail_ref, pos_ref, h_ref, xs_out, zero_ref, sem):
    tile = h_ref.shape[0]

    @pl.when(pl.program_id(0) == 0)
    def _():
        zero_ref[...] = jnp.zeros_like(zero_ref)
        for e in range(2 * N_EXPERTS):
            @pl.when(tail_ref[e] >= 0)
            def _():
                rows = pl.ds(pl.multiple_of(tail_ref[e], MOE_TILE), MOE_TILE)
                tail = pltpu.make_async_copy(zero_ref, xs_out.at[rows], sem)
                tail.start()
                tail.wait()

    def issue(r, carry):
        for k in range(2):
            pltpu.make_async_copy(h_ref.at[pl.ds(r, 1)], xs_out.at[pl.ds(pos_ref[0, 0, k * tile + r], 1)],
                                  sem).start(priority=k)
        return carry
    lax.fori_loop(0, tile, issue, 0, unroll=8)
    for k in range(2):
        pltpu.make_async_copy(h_ref, xs_out.at[pl.ds(0, tile)], sem).wait()


def _moe_dispatch(h, pos, tails, n_rows):
    n_tok, d = h.shape
    grid_spec = pltpu.PrefetchScalarGridSpec(
        num_scalar_prefetch=1,
        grid=(n_tok // TOK_TILE,),
        in_specs=[pl.BlockSpec((1, 1, 2 * TOK_TILE), lambda i, tl: (i, 0, 0), memory_space=pltpu.SMEM),
                  pl.BlockSpec((TOK_TILE, d), lambda i, tl: (i, 0))],
        out_specs=pl.BlockSpec(memory_space=pl.ANY),
        scratch_shapes=[pltpu.VMEM((MOE_TILE, d), h.dtype), pltpu.SemaphoreType.DMA(())])
    return pl.pallas_call(
        _dispatch_kernel,
        grid_spec=grid_spec,
        out_shape=jax.ShapeDtypeStruct((n_rows, d), h.dtype),
        compiler_params=_params(("arbitrary",)),
        name="expert_dispatch",
    )(tails, pos, h)


def _moe_kernel(te_ref, nact_ref, x_ref, wg_ref, wu_ref, wd_ref, o_ref, xb_ref, acc_ref):
    i, f = pl.program_id(0), pl.program_id(1)
    active = i < nact_ref[0]
    last = f == pl.num_programs(1) - 1

    @pl.when(f == 0)
    def _():
        xb_ref[...] = x_ref[...].astype(BF16)
        acc_ref[...] = jnp.zeros_like(acc_ref)

    @pl.when(active)
    def _():
        x = xb_ref[...]
        for j in range(wg_ref.shape[2] // FFN_CHUNK):
            cols = slice(j * FFN_CHUNK, (j + 1) * FFN_CHUNK)
            act = (_silu(_dot(x, wg_ref[0, :, cols])) * _dot(x, wu_ref[0, :, cols])).astype(BF16)
            acc_ref[...] += _dot(act, wd_ref[0, cols, :])

    @pl.when(last)
    def _():
        o_ref[...] = acc_ref[...]


def _moe_ffn(x_sorted, tile_expert, n_active, wg, wu, wd):
    n_rows, d = x_sorted.shape
    ffn = wg.shape[2]
    n_tiles = n_rows // MOE_TILE
    n_f = ffn // MOE_FCHUNK

    def row(i, f, te, na):
        return jnp.minimum(i, na[0] - 1)

    def fcol(i, f, te, na):
        return jnp.where(i < na[0], f, n_f - 1)

    grid_spec = pltpu.PrefetchScalarGridSpec(
        num_scalar_prefetch=2,
        grid=(n_tiles, n_f),
        in_specs=[pl.BlockSpec((MOE_TILE, d), lambda i, f, te, na: (row(i, f, te, na), 0)),
                  pl.BlockSpec((1, d, MOE_FCHUNK), lambda i, f, te, na: (te[i], 0, fcol(i, f, te, na))),
                  pl.BlockSpec((1, d, MOE_FCHUNK), lambda i, f, te, na: (te[i], 0, fcol(i, f, te, na))),
                  pl.BlockSpec((1, MOE_FCHUNK, d), lambda i, f, te, na: (te[i], fcol(i, f, te, na), 0))],
        out_specs=pl.BlockSpec((MOE_TILE, d), lambda i, f, te, na: (i, 0)),
        scratch_shapes=[pltpu.VMEM((MOE_TILE, d), BF16), pltpu.VMEM((MOE_TILE, d), F32)])
    return pl.pallas_call(
        _moe_kernel,
        grid_spec=grid_spec,
        out_shape=jax.ShapeDtypeStruct((n_rows, d), F32),
        compiler_params=_params(("arbitrary", "arbitrary")),
        name="expert_swiglu",
    )(tile_expert, n_active, x_sorted, wg, wu, wd)


def _final_kernel(pos_ref, next_pos_ref, x_ref, route_ref, mod_ref, g_ref, y_hbm, o_ref, buf_ref, sem):
    tile = x_ref.shape[0]
    i, n = pl.program_id(0), pl.num_programs(0)
    slot = i % 2

    def fetch(p_ref, s):
        def issue(r, carry):
            for k in range(2):
                pltpu.make_async_copy(y_hbm.at[pl.ds(p_ref[0, 0, k * tile + r], 1)],
                                      buf_ref.at[s, k, pl.ds(r, 1)], sem.at[s]).start(priority=k)
            return carry
        lax.fori_loop(0, tile, issue, 0, unroll=8)

    @pl.when(i == 0)
    def _():
        fetch(pos_ref, slot)

    @pl.when(i + 1 < n)
    def _():
        fetch(next_pos_ref, 1 - slot)

    for k in range(2):
        pltpu.make_async_copy(y_hbm.at[pl.ds(0, tile)], buf_ref.at[slot, k], sem.at[slot]).wait()
    route = route_ref[...]
    y = route[:, 2:3] * buf_ref[slot, 0] + route[:, 3:4] * buf_ref[slot, 1]
    x = x_ref[...] + mod_ref[0, 0][5:6] * y
    o_ref[...] = x * lax.rsqrt(jnp.mean(x * x, axis=-1, keepdims=True) + RMS_EPS) * g_ref[...]


def _final_combine(x_lat, y_sorted, pos, route, mod, layer, g_final, dims):
    n_rows, d = x_lat.shape
    n_tiles = n_rows // TOK_TILE
    tiles_per_batch = dims["S"] // TOK_TILE
    return pl.pallas_call(
        _final_kernel,
        grid=(n_tiles,),
        in_specs=[pl.BlockSpec((1, 1, 2 * TOK_TILE), lambda i: (i, 0, 0), memory_space=pltpu.SMEM),
                  pl.BlockSpec((1, 1, 2 * TOK_TILE), lambda i: (jnp.minimum(i + 1, n_tiles - 1), 0, 0),
                               memory_space=pltpu.SMEM),
                  pl.BlockSpec((TOK_TILE, d), lambda i: (i, 0)),
                  pl.BlockSpec((TOK_TILE, LANES), lambda i: (i, 0)),
                  pl.BlockSpec((1, 1, 6, d), lambda i: (layer, i // tiles_per_batch, 0, 0)),
                  pl.BlockSpec((1, d), lambda i: (0, 0)),
                  pl.BlockSpec(memory_space=pl.ANY)],
        out_specs=pl.BlockSpec((TOK_TILE, d), lambda i: (i, 0)),
        out_shape=jax.ShapeDtypeStruct((n_rows, d), F32),
        scratch_shapes=[pltpu.VMEM((2, 2, TOK_TILE, d), F32), pltpu.SemaphoreType.DMA((2,))],
        compiler_params=_params(("arbitrary",)),
        name="expert_combine_final_norm",
    )(pos, pos, x_lat, route, mod, g_final, y_sorted)


def _rope_tables(s_len):
    t = np.arange(s_len)
    lane = np.arange(LANES) % HEAD_DIM
    quarter = HEAD_DIM // 4
    inv = 1.0 / (ROPE_BASE ** (jnp.arange(quarter, dtype=F32) / quarter))
    pos = np.where(lane[None, :] < HEAD_DIM // 2, (t // GRID_W)[:, None], (t % GRID_W)[:, None]).astype(np.float32)
    ang = jnp.asarray(pos) * inv[lane % quarter][None, :]
    first = jnp.asarray((lane % (HEAD_DIM // 2)) < quarter)[None, :]
    cos, sin = jnp.cos(ang), jnp.sin(ang)
    ident = jnp.zeros((TOK_TILE, LANES), F32)
    return (jnp.concatenate([cos, ident + 1.0]),
            jnp.concatenate([jnp.where(first, -sin, 0.0), ident]),
            jnp.concatenate([jnp.where(first, 0.0, sin), ident]))


def _rearranged_w_in(w):
    offs = np.concatenate([[0], np.cumsum(IN_SIZES)])
    part = lambda j: w[:, int(offs[j]):int(offs[j + 1])]
    pad = jnp.zeros((w.shape[0], LANES - 2 * GLA_LR), w.dtype)
    order = [0, 1, 2, 3, 6, 7, 9, 10, 4, 5]
    w_cat = jnp.concatenate([part(j) for j in order] + [pad], axis=1).astype(BF16)
    return w_cat, jnp.transpose(part(8)).astype(BF16), jnp.transpose(part(11)).astype(BF16)


def _decay_up(w_up, b):
    top = jnp.concatenate([w_up[0], jnp.zeros_like(w_up[0])], axis=1)
    bot = jnp.concatenate([jnp.zeros_like(w_up[1]), w_up[1]], axis=1)
    pad = jnp.zeros((LANES - 2 * GLA_LR, 2 * GLA_QK), w_up.dtype)
    return jnp.concatenate([top, bot, pad], axis=0).astype(BF16), jnp.concatenate([b[0], b[1]])[None, :]


def kernel(x, c, ctx, c_ctx, w_mod, b_mod, g_norm1, g_norm2, w_in, gla_w_dec_up, gla_b_dec, gla_g_norm,
           diff_lambda, diff_g_norm, na_rpb, w_out, w_ffn_gate, w_ffn_up, w_ffn_down, w_router, w_moe_gate,
           w_moe_up, w_moe_down, g_final):
    nb, s_len, d = x.shape
    c_len = ctx.shape[1]
    depth = w_mod.shape[0]
    assert c_len == ATT_TILE and s_len % TOK_TILE == 0 and (nb * c_len) % TOK_TILE == 0 and nb < MOD_ROWS
    n_lat = nb * s_len
    dims = {"B": nb, "S": s_len, "n_lat": n_lat, "n_tot": n_lat + nb * c_len}

    x_all = (x.reshape(n_lat, d), ctx.reshape(nb * c_len, d))
    cc = jnp.concatenate([c, c_ctx[None, :], jnp.zeros((MOD_ROWS - nb - 1, d), F32)], axis=0)
    mod = _modulation(cc, w_mod, b_mod).reshape(depth, MOD_ROWS, 6, d)
    rope_tabs = _rope_tables(s_len)

    for l in range(depth):
        last = l == depth - 1
        w_cat, w_dvt, w_nvt = _rearranged_w_in(w_in[l])
        wup2, bdec2 = _decay_up(gla_w_dec_up[l], gla_b_dec[l])
        gla, lg, dq, dk, dvt, nq, nk, nvt = _input_projection(
            x_all, mod, l, g_norm1[l][None, :], w_cat, w_dvt, w_nvt, wup2, bdec2, rope_tabs, dims)
        o_f, o_b = _gla(gla, lg, dims)
        b_diff = _diff_attention(dq, dk, dvt, diff_lambda[l], diff_g_norm[l][None, :], l, not last, dims)
        c_na = _neighbourhood_attention(nq, nk, nvt, _na_bias_tables(na_rpb[l], s_len // GRID_W), not last, dims)
        g_gla = jnp.tile(gla_g_norm[l], GLA_HEADS)[None, :]
        n_rows = n_lat if last else dims["n_tot"]
        if l % 2 == 0:
            i = l // 2
            x_mid, h = _output_projection(x_all, o_f, o_b, gla, b_diff, c_na, mod, l, g_gla, g_norm2[l][None, :],
                                          w_out[l].astype(BF16), None, n_rows, dims)
            assert not last, "the dense channel mixer is implemented for non-final layers only"
            x_all = _dense_ffn(h, x_mid, mod, l, w_ffn_gate[i].astype(BF16), w_ffn_up[i].astype(BF16),
                               w_ffn_down[i].astype(BF16), dims)
        else:
            i = l // 2
            w_r = jnp.concatenate([w_router[i], jnp.zeros((d, LANES - N_EXPERTS), F32)], axis=1)
            x_mid, h, route = _output_projection(x_all, o_f, o_b, gla, b_diff, c_na, mod, l, g_gla,
                                                 g_norm2[l][None, :], w_out[l].astype(BF16), w_r, n_rows, dims)
            assert last, "the expert layer is implemented for the final layer only"
            pos, tails, n_sorted, tile_expert, n_active = _moe_routing(route, n_rows)
            x_sorted = _moe_dispatch(h, pos, tails, n_sorted)
            y_sorted = _moe_ffn(x_sorted, tile_expert, n_active, w_moe_gate[i].astype(BF16),
                                w_moe_up[i].astype(BF16), w_moe_down[i].astype(BF16))
            return _final_combine(x_mid, y_sorted, pos, route, mod, l, g_final[None, :], dims).reshape(nb, s_len, d)
    raise NotImplementedError("the final layer must be an expert layer")
```
